```python
import math
import jax, jax.numpy as jnp
from jax import lax
import numpy as np

D_MODEL = 1024
BATCH = 2
SEQ = 8192
DEPTH = 1
DEC_BATCH = 8
DEC_SEQ = 4096
PAST_LEN = 128

HEAD_DIM = 64
N_ATTN_HEADS = 8
ATTN_WIDTH = N_ATTN_HEADS * HEAD_DIM
DILATED_BRANCHES = ((128, 1), (512, 4), (2048, 16))
N_BUCKETS = 32
MAX_DISTANCE = 1024
N_SGU_GROUPS = 8
SGU_GROUP_DIM = 64
SGU_WIDTH = N_SGU_GROUPS * SGU_GROUP_DIM
SGU_CHUNK = 128
MIX_WIDTH = ATTN_WIDTH + SGU_WIDTH
IN_COLS = 3 * ATTN_WIDTH + 2 * SGU_WIDTH
N_EXPERTS = 32
TOP_K = 4
D_EXPERT = 1024
SWIGLU_LIMIT = 7.0
SWIGLU_ALPHA = 1.702
MOE_BLOCK = 256
EPS = 1e-6
NEG_INF = -1e30

kernel_name = "hybrid_sgu_dilated_attn_moe_encoder"


def _rms_norm(x, g):
    xf = x.astype(jnp.float32)
    y = xf * lax.rsqrt(jnp.mean(xf * xf, axis=-1, keepdims=True) + EPS)
    return (y * g.astype(jnp.float32)).astype(x.dtype)


def _t5_bucket(rel):
    nb = N_BUCKETS // 2
    bucket = (rel > 0).astype(np.int32) * nb
    n = np.abs(rel)
    max_exact = nb // 2
    large = max_exact + (np.log(np.maximum(n, 1) / max_exact)
                         / np.log(MAX_DISTANCE / max_exact) * (nb - max_exact)).astype(np.int32)
    large = np.minimum(large, nb - 1)
    return (bucket + np.where(n < max_exact, n, large)).astype(np.int32)


def _dilated_branch(q, k, v, rel_bias, window, dil):
    B, S, H, Dh = q.shape
    side = window // (2 * dil)
    blk = side
    unit = dil * blk
    Sp = -(-S // unit) * unit
    M = Sp // dil
    nb = M // blk

    def to_blocks(t):
        t = jnp.pad(t, ((0, 0), (0, Sp - S), (0, 0), (0, 0)))
        t = t.reshape(B, M, dil, H, Dh).transpose(0, 2, 1, 3, 4)
        return t.reshape(B, dil, nb, blk, H, Dh)

    def to_windows(tb):
        tp = jnp.pad(tb, ((0, 0), (0, 0), (1, 1), (0, 0), (0, 0), (0, 0)))
        return jnp.concatenate([tp[:, :, :-2], tp[:, :, 1:-1], tp[:, :, 2:]], axis=3)

    qb = to_blocks(q)
    kw = to_windows(to_blocks(k))
    vw = to_windows(to_blocks(v))

    valid = (np.arange(Sp) < S).reshape(M, dil).T.reshape(dil, nb, blk)
    vp = np.pad(valid, ((0, 0), (1, 1), (0, 0)))
    vwin = np.concatenate([vp[:, :-2], vp[:, 1:-1], vp[:, 2:]], axis=2)
    diff = np.arange(3 * blk)[None, :] - blk - np.arange(blk)[:, None]
    band = np.abs(diff) <= side
    mask = band[None, None, None, :, :] & vwin[:, :, None, None, :]
    bucket = _t5_bucket(diff * dil)
    bias = rel_bias[jnp.asarray(bucket)].transpose(2, 0, 1).astype(jnp.float32)

    logits = jnp.einsum('brnqhc,brnkhc->brnhqk', qb, kw,
                        preferred_element_type=jnp.float32) + bias
    logits = jnp.where(mask, logits, NEG_INF)
    lse = jax.nn.logsumexp(logits, axis=-1)
    p = jnp.exp(logits - lse[..., None])
    o = jnp.einsum('brnhqk,brnkhc->brnqhc', p.astype(v.dtype), vw)
    o = o.reshape(B, dil, M, H, Dh).transpose(0, 2, 1, 3, 4).reshape(B, Sp, H, Dh)[:, :S]
    lse = lse.transpose(0, 1, 2, 4, 3).reshape(B, dil, M, H).transpose(0, 2, 1, 3)
    lse = lse.reshape(B, Sp, H)[:, :S]
    return o, lse


def _dilated_attention(q, k, v, rel_bias):
    outs, lses = [], []
    for window, dil in DILATED_BRANCHES:
        o, l = _dilated_branch(q, k, v, rel_bias, window, dil)
        outs.append(o)
        lses.append(l)
    w = jax.nn.softmax(jnp.stack(lses, axis=0), axis=0)
    o = jnp.stack(outs, axis=0).astype(jnp.float32)
    return jnp.sum(w[..., None] * o, axis=0).astype(q.dtype)


def _spatial_gating(u, vs, sgu_norm, w_s, b_s):
    B, S, _ = u.shape
    vs = _rms_norm(vs, sgu_norm)
    vs = vs.reshape(B, S // SGU_CHUNK, SGU_CHUNK, N_SGU_GROUPS, SGU_GROUP_DIM)
    s = jnp.einsum('gpq,bnqgc->bnpgc', w_s, vs) + b_s.T[None, None, :, :, None]
    return u * s.reshape(B, S, SGU_WIDTH)


def _moe(xt, w_router, b_router, w_gu, b_gu, w_down, b_down):
    T, D = xt.shape
    logits = (xt @ w_router).astype(jnp.float32) + b_router.astype(jnp.float32)
    top_v, top_e = lax.top_k(logits, TOP_K)
    gates = jax.nn.softmax(top_v, axis=-1)
    n_assign = T * TOP_K
    flat_e = top_e.reshape(-1)
    flat_g = gates.reshape(-1)
    flat_tok = jnp.arange(n_assign, dtype=jnp.int32) // TOP_K
    order = jnp.argsort(flat_e, stable=True)
    sorted_e = flat_e[order]
    counts = jnp.bincount(flat_e, length=N_EXPERTS).astype(jnp.int32)
    padded = (counts + MOE_BLOCK - 1) // MOE_BLOCK * MOE_BLOCK
    start = jnp.cumsum(counts) - counts
    pad_end = jnp.cumsum(padded)
    pad_start = pad_end - padded
    rank = jnp.arange(n_assign, dtype=jnp.int32) - start[sorted_e]
    dest = pad_start[sorted_e] + rank
    n_blocks = -(-n_assign // MOE_BLOCK) + N_EXPERTS
    n_rows = n_blocks * MOE_BLOCK
    row_tok = jnp.full((n_rows,), T, jnp.int32).at[dest].set(flat_tok[order])
    row_gate = jnp.zeros((n_rows,), jnp.float32).at[dest].set(flat_g[order])
    block_e = jnp.minimum(
        jnp.searchsorted(pad_end, jnp.arange(n_blocks, dtype=jnp.int32) * MOE_BLOCK, side='right'),
        N_EXPERTS - 1)
    x_rows = jnp.concatenate([xt, jnp.zeros((1, D), xt.dtype)], axis=0)[row_tok]
    x_rows = x_rows.reshape(n_blocks, MOE_BLOCK, D)

    def expert_block(args):
        xb, e = args
        gu = xb @ w_gu[e] + b_gu[e]
        gate = jnp.minimum(gu[:, :D_EXPERT], SWIGLU_LIMIT)
        up = jnp.clip(gu[:, D_EXPERT:], -SWIGLU_LIMIT, SWIGLU_LIMIT)
        glu = gate * jax.nn.sigmoid(SWIGLU_ALPHA * gate)
        return ((up + 1) * glu) @ w_down[e] + b_down[e]

    y_rows = lax.map(expert_block, (x_rows, block_e)).reshape(n_rows, D)
    y = jax.ops.segment_sum(y_rows.astype(jnp.float32) * row_gate[:, None], row_tok,
                            num_segments=T + 1)
    return y[:T].astype(xt.dtype)


def _layer(x, norm1, w_in, q_gain, k_gain, rel_bias, sgu_norm, w_s, b_s,
           out_norm_a, out_norm_b, w_o, norm2, w_router, b_router, w_gu, b_gu, w_down, b_down):
    B, S, D = x.shape
    h = _rms_norm(x, norm1)
    z = h @ w_in
    q, k, v, zu, zv = jnp.split(
        z, [ATTN_WIDTH, 2 * ATTN_WIDTH, 3 * ATTN_WIDTH, 3 * ATTN_WIDTH + SGU_WIDTH], axis=-1)
    q = _rms_norm(q.reshape(B, S, N_ATTN_HEADS, HEAD_DIM), q_gain) * (HEAD_DIM ** -0.5)
    k = _rms_norm(k.reshape(B, S, N_ATTN_HEADS, HEAD_DIM), k_gain)
    v = v.reshape(B, S, N_ATTN_HEADS, HEAD_DIM)
    attn = _dilated_attention(q, k, v, rel_bias).reshape(B, S, ATTN_WIDTH)
    sgu = _spatial_gating(jax.nn.gelu(zu, approximate=False), jax.nn.gelu(zv, approximate=False),
                          sgu_norm, w_s, b_s)
    mixed = jnp.concatenate([_rms_norm(sgu, out_norm_a), _rms_norm(attn, out_norm_b)], axis=-1)
    x = x + mixed @ w_o
    h2 = _rms_norm(x, norm2)
    x = x + _moe(h2.reshape(B * S, D), w_router, b_router, w_gu, b_gu, w_down, b_down).reshape(B, S, D)
    return x


def setup_inputs(seed: int = 0) -> dict:
    key = jax.random.key(seed)
    ks = jax.random.split(key, 22)
    nrm = jax.random.normal
    f32 = jnp.float32
    return {
        "x_prompt": nrm(ks[0], (BATCH, SEQ, D_MODEL), f32),
        "x_sample": nrm(ks[1], (DEC_BATCH, DEC_SEQ, D_MODEL), f32),
        "norm1": 1.0 + 0.05 * nrm(ks[2], (DEPTH, D_MODEL), f32),
        "w_in": nrm(ks[3], (DEPTH, D_MODEL, IN_COLS), f32) * D_MODEL ** -0.5,
        "q_gain": 1.0 + 0.05 * nrm(ks[4], (DEPTH, HEAD_DIM), f32),
        "k_gain": 1.0 + 0.05 * nrm(ks[5], (DEPTH, HEAD_DIM), f32),
        "rel_bias": 0.5 * nrm(ks[6], (N_BUCKETS, N_ATTN_HEADS), f32),
        "sgu_norm": 1.0 + 0.05 * nrm(ks[7], (DEPTH, SGU_WIDTH), f32),
        "w_s": nrm(ks[8], (DEPTH, N_SGU_GROUPS, SGU_CHUNK, SGU_CHUNK), f32) * SGU_CHUNK ** -0.5,
        "b_s": 1.0 + 0.1 * nrm(ks[9], (DEPTH, N_SGU_GROUPS, SGU_CHUNK), f32),
        "out_norm_a": 1.0 + 0.05 * nrm(ks[10], (DEPTH, SGU_WIDTH), f32),
        "out_norm_b": 1.0 + 0.05 * nrm(ks[11], (DEPTH, ATTN_WIDTH), f32),
        "w_o": nrm(ks[12], (DEPTH, MIX_WIDTH, D_MODEL), f32) * MIX_WIDTH ** -0.5,
        "norm2": 1.0 + 0.05 * nrm(ks[13], (DEPTH, D_MODEL), f32),
        "w_router": nrm(ks[14], (DEPTH, D_MODEL, N_EXPERTS), f32) * D_MODEL ** -0.5,
        "b_router": 0.01 * nrm(ks[15], (DEPTH, N_EXPERTS), f32),
        "w_gu": nrm(ks[16], (DEPTH, N_EXPERTS, D_MODEL, 2 * D_EXPERT), f32) * D_MODEL ** -0.5,
        "b_gu": 0.01 * nrm(ks[17], (DEPTH, N_EXPERTS, 2 * D_EXPERT), f32),
        "w_down": nrm(ks[18], (DEPTH, N_EXPERTS, D_EXPERT, D_MODEL), f32) * D_EXPERT ** -0.5,
        "b_down": 0.01 * nrm(ks[19], (DEPTH, N_EXPERTS, D_MODEL), f32),
    }


def reference(x_prompt, x_sample, norm1, w_in, q_gain, k_gain, rel_bias, sgu_norm, w_s, b_s,
              out_norm_a, out_norm_b, w_o, norm2, w_router, b_router, w_gu, b_gu, w_down, b_down):
    def trunk(x):
        for l in range(DEPTH):
            x = _layer(x, norm1[l], w_in[l], q_gain[l], k_gain[l], rel_bias, sgu_norm[l], w_s[l],
                       b_s[l], out_norm_a[l], out_norm_b[l], w_o[l], norm2[l], w_router[l],
                       b_router[l], w_gu[l], b_gu[l], w_down[l], b_down[l])
        return x
    y_prompt = trunk(x_prompt)
    y_sample = trunk(x_sample)
    return (y_prompt, y_sample)
```

```python
import functools
import math

import numpy as np
import jax
import jax.numpy as jnp
from jax import lax
from jax.experimental import pallas as pl
from jax.experimental.pallas import tpu as pltpu

F32 = jnp.float32
BF16 = jnp.bfloat16

EPS = 1e-6
NEG_INF = -1e30
HEAD_DIM = 64
SGU_CHUNK = 128
WINDOWS = ((128, 1), (512, 4), (2048, 16))
N_BUCKETS = 32
MAX_DISTANCE = 1024
TOP_K = 4
SWIGLU_LIMIT = 7.0
SWIGLU_ALPHA = 1.702

LANES = 128
TOK_TILE = 512
ATT_TILE = 2048
ATT_HALO = 1024
ATT_QB = 128
ATT_SIDE = 64
MOE_ROWS = 512
VMEM_LIMIT = 56 * 1024 * 1024


def _dot(a, b):
    return jnp.dot(a, b, preferred_element_type=F32)


def _dot_nt(a, b):
    return lax.dot_general(a, b, (((1,), (1,)), ((), ())), preferred_element_type=F32)


def _rms(x, g):
    ms = jnp.mean(x * x, axis=-1, keepdims=True)
    return x * lax.rsqrt(ms + EPS) * g


def _gelu(x):
    return 0.5 * x * (1.0 + lax.erf(x * (1.0 / math.sqrt(2.0))))


def _split_bf16(x):
    hi = x.astype(BF16)
    lo = (x - hi.astype(F32)).astype(BF16)
    return hi, lo


def _inproj_kernel(n1, attn_w, sgu_w, xp_ref, xs_ref, norm1_ref, win_ref, gsum_ref, gexp_ref,
                   qkg_ref, sgun_ref, wsp_ref, bsb_ref, ona_ref, q_ref, k_ref, v_ref, a_ref):
    i = pl.program_id(0)
    x = jnp.where(i < n1, xp_ref[...], xs_ref[...])
    h = _rms(x, norm1_ref[...]).astype(BF16)

    zqk = _dot(h, win_ref[:, 0:2 * attn_w])
    sq_hi, sq_lo = _split_bf16(zqk * zqk)
    ss = _dot(sq_hi, gsum_ref[...]) + _dot(sq_lo, gsum_ref[...])
    inv = lax.rsqrt(ss * (1.0 / HEAD_DIM) + EPS)
    inv_hi, inv_lo = _split_bf16(inv)
    invb = _dot(inv_hi, gexp_ref[...]) + _dot(inv_lo, gexp_ref[...])
    qk = zqk * invb * qkg_ref[...]
    q_ref[...] = qk[:, 0:attn_w]
    k_ref[...] = qk[:, attn_w:2 * attn_w]
    v_ref[...] = _dot(h, win_ref[:, 2 * attn_w:3 * attn_w])

    c0 = 3 * attn_w
    u = _gelu(_dot(h, win_ref[:, c0:c0 + sgu_w]))
    gv = _gelu(_dot(h, win_ref[:, c0 + sgu_w:c0 + 2 * sgu_w]))
    vsn = _rms(gv, sgun_ref[...]).astype(BF16)
    lane = lax.broadcasted_iota(jnp.int32, (SGU_CHUNK, LANES), 1)
    lo_half = lane < HEAD_DIM
    zero = jnp.zeros((SGU_CHUNK, LANES), BF16)
    for c in range(TOK_TILE // SGU_CHUNK):
        r0 = c * SGU_CHUNK
        parts = []
        for j in range(sgu_w // LANES):
            blk = vsn[r0:r0 + SGU_CHUNK, j * LANES:(j + 1) * LANES]
            rhs = jnp.concatenate([jnp.where(lo_half, blk, zero), jnp.where(lo_half, zero, blk)], axis=0)
            parts.append(_dot(wsp_ref[j], rhs))
        s = jnp.concatenate(parts, axis=1) + bsb_ref[...]
        a = u[r0:r0 + SGU_CHUNK, :] * s
        a_ref[r0:r0 + SGU_CHUNK, :] = _rms(a, ona_ref[...]).astype(BF16)


def _inproj(xp, xs, norm1, w_in, q_gain, k_gain, sgu_norm, w_s, b_s, out_norm_a):
    tp, d_model = xp.shape
    ts = xs.shape[0]
    t = tp + ts
    n_heads_w = q_gain.shape[0]
    sgu_w = sgu_norm.shape[0]
    attn_w = (w_in.shape[1] - 2 * sgu_w) // 3
    n_heads = attn_w // n_heads_w
    n_groups = w_s.shape[0]
    assert n_heads_w == HEAD_DIM and sgu_w // n_groups == HEAD_DIM and w_s.shape[1] == SGU_CHUNK
    assert tp % TOK_TILE == 0 and ts % TOK_TILE == 0 and 2 * n_heads <= LANES
    n1 = tp // TOK_TILE

    heads = np.arange(2 * attn_w) // HEAD_DIM
    gsum = (heads[:, None] == np.arange(LANES)[None, :]).astype(np.float32)
    gexp = gsum.T
    qkg = jnp.concatenate([jnp.tile(q_gain, n_heads) * (HEAD_DIM ** -0.5), jnp.tile(k_gain, n_heads)])
    wsp = jnp.concatenate([w_s[0::2], w_s[1::2]], axis=2).astype(BF16)
    bsb = jnp.repeat(b_s.T, HEAD_DIM, axis=1)

    const = lambda shape: pl.BlockSpec(shape, lambda i: (0,) * len(shape))
    tok = lambda w: pl.BlockSpec((TOK_TILE, w), lambda i: (i, 0))
    return pl.pallas_call(
        functools.partial(_inproj_kernel, n1, attn_w, sgu_w),
        grid=(t // TOK_TILE,),
        in_specs=[
            pl.BlockSpec((TOK_TILE, d_model), lambda i: (jnp.minimum(i, n1 - 1), 0)),
            pl.BlockSpec((TOK_TILE, d_model), lambda i: (jnp.maximum(i - n1, 0), 0)),
            const((1, d_model)), const(w_in.shape), const((2 * attn_w, LANES)), const((LANES, 2 * attn_w)),
            const((1, 2 * attn_w)), const((1, sgu_w)), const(wsp.shape), const((SGU_CHUNK, sgu_w)),
            const((1, sgu_w)),
        ],
        out_specs=[tok(attn_w), tok(attn_w), tok(attn_w), tok(sgu_w)],
        out_shape=[jax.ShapeDtypeStruct((t, attn_w), F32)] * 3 + [jax.ShapeDtypeStruct((t, sgu_w), BF16)],
        compiler_params=pltpu.CompilerParams(dimension_semantics=("arbitrary",), vmem_limit_bytes=VMEM_LIMIT),
        name="inproj",
    )(xp, xs, norm1[None], w_in.astype(BF16), jnp.asarray(gsum, BF16), jnp.asarray(gexp, BF16),
      qkg[None], sgu_norm[None], wsp, bsb, out_norm_a[None])


def _t5_bucket(rel):
    nb = N_BUCKETS // 2
    bucket = (rel > 0).astype(np.int32) * nb
    n = np.abs(rel)
    max_exact = nb // 2
    large = max_exact + (np.log(np.maximum(n, 1) / max_exact)
                         / np.log(MAX_DISTANCE / max_exact) * (nb - max_exact)).astype(np.int32)
    large = np.minimum(large, nb - 1)
    return (bucket + np.where(n < max_exact, n, large)).astype(np.int32)


def _branch_layout():
    out, kbase = [], 0
    for bi, (window, dil) in enumerate(WINDOWS):
        assert window // (2 * dil) == ATT_SIDE
        m = ATT_TILE // dil
        seg = m + 2 * ATT_SIDE
        out.append((dil, m, seg, kbase, bi * ATT_TILE))
        kbase += dil * seg
    return out, kbase


def _strided(ref, start, size, stride):
    if stride == 1:
        return ref[start:start + size, :]
    return ref[pl.ds(start, size, stride=stride), :]


def _attn_kernel(tiles_p, per_seq_p, per_seq_s, q_ref, km_ref, kp_ref, kn_ref, vm_ref, vp_ref, vn_ref,
                 bias_ref, o_ref, qs, ks, vs, o_scr, l_scr):
    i = pl.program_id(0)
    w = jnp.where(i < tiles_p, i % per_seq_p, (i - tiles_p) % per_seq_s)
    last = jnp.where(i < tiles_p, per_seq_p - 1, per_seq_s - 1)
    left_ok = w > 0
    right_ok = w < last
    layout, _ = _branch_layout()

    for dil, m, seg, kbase, qbase in layout:
        for r in range(dil):
            o = kbase + r * seg
            for dst, main, prev, nxt in ((ks, km_ref, kp_ref, kn_ref), (vs, vm_ref, vp_ref, vn_ref)):
                dst[o:o + ATT_SIDE, :] = _strided(prev, ATT_HALO - ATT_SIDE * dil + r, ATT_SIDE, dil).astype(BF16)
                dst[o + ATT_SIDE:o + ATT_SIDE + m, :] = _strided(main, r, m, dil).astype(BF16)
                dst[o + ATT_SIDE + m:o + seg, :] = _strided(nxt, r, ATT_SIDE, dil).astype(BF16)
            qs[qbase + r * m:qbase + (r + 1) * m, :] = _strided(q_ref, r, m, dil).astype(BF16)

    lane = lax.broadcasted_iota(jnp.int32, (ATT_QB, LANES), 1)
    head0 = lane < HEAD_DIM
    col = lax.broadcasted_iota(jnp.int32, (1, 2 * ATT_QB), 1)
    for bi, (dil, m, seg, kbase, qbase) in enumerate(layout):
        nblk = m // ATT_QB

        def cell(idx, carry, bi=bi, nblk=nblk, seg=seg, kbase=kbase, qbase=qbase):
            r = idx // nblk
            blk = idx % nblk
            qoff = pl.multiple_of(qbase + idx * ATT_QB, ATT_QB)
            koff = pl.multiple_of(kbase + r * seg + blk * ATT_QB, ATT_QB)
            left_bad = jnp.logical_and(blk == 0, jnp.logical_not(left_ok))
            right_bad = jnp.logical_and(blk == nblk - 1, jnp.logical_not(right_ok))
            edge = (jnp.where(jnp.logical_and(col < ATT_SIDE, left_bad), NEG_INF, 0.0)
                    + jnp.where(jnp.logical_and(col >= 2 * ATT_QB - ATT_SIDE, right_bad), NEG_INF, 0.0))
            qc = qs[pl.ds(qoff, ATT_QB), :]
            kc = ks[pl.ds(koff, 2 * ATT_QB), :]
            vc = vs[pl.ds(koff, 2 * ATT_QB), :]
            zero = jnp.zeros_like(qc)
            outs, lses = [], []
            for hsel in (head0, jnp.logical_not(head0)):
                s = _dot_nt(jnp.where(hsel, qc, zero), kc) + bias_ref[len(outs), bi] + edge
                mx = jnp.max(s, axis=-1, keepdims=True)
                p = jnp.exp(s - mx)
                den = jnp.sum(p, axis=-1, keepdims=True)
                outs.append(_dot(p.astype(BF16), vc) / den)
                lses.append(mx + jnp.log(den))
            o_scr[pl.ds(qoff, ATT_QB), :] = jnp.where(head0, outs[0], outs[1])
            l_scr[pl.ds(qoff, ATT_QB), :] = jnp.where(head0, lses[0], lses[1])
            return carry

        lax.fori_loop(0, dil * nblk, cell, 0)

    big = WINDOWS[-1][1]
    rows = ATT_TILE // big
    for r in range(big):
        os_, ls_ = [], []
        for dil, m, seg, kbase, qbase in layout:
            start = qbase + (r % dil) * m + r // dil
            os_.append(_strided(o_scr, start, rows, big // dil))
            ls_.append(_strided(l_scr, start, rows, big // dil))
        mx = jnp.maximum(jnp.maximum(ls_[0], ls_[1]), ls_[2])
        ws = [jnp.exp(l - mx) for l in ls_]
        num = ws[0] * os_[0] + ws[1] * os_[1] + ws[2] * os_[2]
        o_ref[pl.ds(r, rows, stride=big), :] = num / (ws[0] + ws[1] + ws[2])


def _attention(q, k, v, rel_bias, tp, seq_p, seq_s):
    t, attn_w = q.shape
    n_heads = attn_w // HEAD_DIM
    assert seq_p % ATT_TILE == 0 and seq_s % ATT_TILE == 0 and ATT_TILE == 2 * ATT_HALO
    assert ATT_TILE // WINDOWS[-1][1] == ATT_QB
    layout, krows = _branch_layout()

    ii = np.arange(ATT_QB)[:, None]
    jj = np.arange(2 * ATT_QB)[None, :]
    rel = jj - ATT_SIDE - ii
    band = np.abs(rel) <= ATT_SIDE
    buckets = np.stack([_t5_bucket(rel * dil) for _, dil in WINDOWS])
    bias = jnp.transpose(rel_bias[jnp.asarray(buckets)], (3, 0, 1, 2)).astype(F32)
    bias = jnp.where(jnp.asarray(band)[None, None], bias, NEG_INF)

    halo_blocks = t // ATT_HALO
    per_tile = ATT_TILE // ATT_HALO
    main = pl.BlockSpec((ATT_TILE, LANES), lambda i, j: (i, j))
    prev = pl.BlockSpec((ATT_HALO, LANES), lambda i, j: (jnp.maximum(i * per_tile - 1, 0), j))
    nxt = pl.BlockSpec((ATT_HALO, LANES), lambda i, j: (jnp.minimum((i + 1) * per_tile, halo_blocks - 1), j))
    n_q = len(WINDOWS) * ATT_TILE
    return pl.pallas_call(
        functools.partial(_attn_kernel, tp // ATT_TILE, seq_p // ATT_TILE, seq_s // ATT_TILE),
        grid=(t // ATT_TILE, attn_w // LANES),
        in_specs=[main, main, prev, nxt, main, prev, nxt,
                  pl.BlockSpec((LANES // HEAD_DIM, len(WINDOWS), ATT_QB, 2 * ATT_QB), lambda i, j: (j, 0, 0, 0))],
        out_specs=main,
        out_shape=jax.ShapeDtypeStruct((t, attn_w), F32),
        scratch_shapes=[pltpu.VMEM((n_q, LANES), BF16), pltpu.VMEM((krows, LANES), BF16),
                        pltpu.VMEM((krows, LANES), BF16), pltpu.VMEM((n_q, LANES), F32),
                        pltpu.VMEM((n_q, LANES), F32)],
        compiler_params=pltpu.CompilerParams(dimension_semantics=("arbitrary", "arbitrary"),
                                             vmem_limit_bytes=VMEM_LIMIT),
        name="attn",
    )(q, k, k, k, v, v, v, bias)


def _outproj_kernel(n1, sgu_w, a_ref, attn_ref, xp_ref, xs_ref, onb_ref, wo_ref, norm2_ref, wrh_ref, wrl_ref,
                    br_ref, x1_ref, h2_ref, lg_ref):
    i = pl.program_id(0)
    x = jnp.where(i < n1, xp_ref[...], xs_ref[...])
    bn = _rms(attn_ref[...], onb_ref[...]).astype(BF16)
    x1 = x + _dot(a_ref[...], wo_ref[0:sgu_w, :]) + _dot(bn, wo_ref[sgu_w:, :])
    x1_ref[...] = x1
    h2 = _rms(x1, norm2_ref[...])
    h2_ref[...] = h2.astype(BF16)
    hi, lo = _split_bf16(h2)
    lg_ref[...] = _dot(hi, wrh_ref[...]) + _dot(lo, wrh_ref[...]) + _dot(hi, wrl_ref[...]) + br_ref[...]


def _outproj(a_n, attn, xp, xs, out_norm_b, w_o, norm2, w_router, b_router):
    t, sgu_w = a_n.shape
    attn_w = attn.shape[1]
    d_model = w_o.shape[1]
    n_exp = w_router.shape[1]
    assert n_exp <= LANES
    n1 = xp.shape[0] // TOK_TILE
    wr = jnp.pad(w_router, ((0, 0), (0, LANES - n_exp)))
    wr_hi = wr.astype(BF16)
    wr_lo = (wr - wr_hi.astype(F32)).astype(BF16)
    br = jnp.pad(b_router, (0, LANES - n_exp))[None]

    const = lambda shape: pl.BlockSpec(shape, lambda i: (0,) * len(shape))
    tok = lambda w: pl.BlockSpec((TOK_TILE, w), lambda i: (i, 0))
    return pl.pallas_call(
        functools.partial(_outproj_kernel, n1, sgu_w),
        grid=(t // TOK_TILE,),
        in_specs=[
            tok(sgu_w), tok(attn_w),
            pl.BlockSpec((TOK_TILE, d_model), lambda i: (jnp.minimum(i, n1 - 1), 0)),
            pl.BlockSpec((TOK_TILE, d_model), lambda i: (jnp.maximum(i - n1, 0), 0)),
            const((1, attn_w)), const(w_o.shape), const((1, d_model)), const((d_model, LANES)),
            const((d_model, LANES)), const((1, LANES)),
        ],
        out_specs=[tok(d_model), tok(d_model), tok(LANES)],
        out_shape=[jax.ShapeDtypeStruct((t, d_model), F32), jax.ShapeDtypeStruct((t, d_model), BF16),
                   jax.ShapeDtypeStruct((t, LANES), F32)],
        compiler_params=pltpu.CompilerParams(dimension_semantics=("arbitrary",), vmem_limit_bytes=VMEM_LIMIT),
        name="outproj",
    )(a_n, attn, xp, xs, out_norm_b[None], w_o.astype(BF16), norm2[None], wr_hi, wr_lo, br)


def _moe_kernel(d_exp, be_ref, nu_ref, x_ref, g_ref, wgu_ref, bgu_ref, wd_ref, bd_ref, y_ref, wgu_s, wd_s, act_s):
    b = pl.program_id(0)
    prev_e = be_ref[jnp.maximum(b - 1, 0)]
    fresh = jnp.logical_or(b == 0, be_ref[b] != prev_e)
    used = b < nu_ref[0]

    @pl.when(jnp.logical_and(fresh, used))
    def _():
        wgu_s[...] = wgu_ref[...].astype(BF16)
        wd_s[...] = wd_ref[...].astype(BF16)

    @pl.when(used)
    def _():
        x = x_ref[...]
        step = 512
        for n in range(0, d_exp, step):
            gate = _dot(x, wgu_s[:, n:n + step]) + bgu_ref[:, n:n + step]
            up = _dot(x, wgu_s[:, d_exp + n:d_exp + n + step]) + bgu_ref[:, d_exp + n:d_exp + n + step]
            gate = jnp.minimum(gate, SWIGLU_LIMIT)
            up = jnp.clip(up, -SWIGLU_LIMIT, SWIGLU_LIMIT)
            glu = gate * jax.nn.sigmoid(SWIGLU_ALPHA * gate)
            act_s[:, n:n + step] = ((up + 1.0) * glu).astype(BF16)
        y = _dot(act_s[...], wd_s[...]) + bd_ref[...]
        y_ref[...] = y * g_ref[...]

    @pl.when(jnp.logical_not(used))
    def _():
        y_ref[...] = jnp.zeros_like(y_ref)


def _moe_experts(x_rows, row_gate, block_e, n_used, w_gu, b_gu, w_down, b_down):
    n_rows, d_model = x_rows.shape
    n_exp, _, two_de = w_gu.shape
    d_exp = two_de // 2
    n_blocks = n_rows // MOE_ROWS
    grid_spec = pltpu.PrefetchScalarGridSpec(
        num_scalar_prefetch=2,
        grid=(n_blocks,),
        in_specs=[
            pl.BlockSpec((MOE_ROWS, d_model), lambda b, be, nu: (b, 0)),
            pl.BlockSpec((MOE_ROWS, 1), lambda b, be, nu: (b, 0)),
            pl.BlockSpec((None, d_model, two_de), lambda b, be, nu: (be[b], 0, 0)),
            pl.BlockSpec((None, 1, two_de), lambda b, be, nu: (be[b], 0, 0)),
            pl.BlockSpec((None, d_exp, d_model), lambda b, be, nu: (be[b], 0, 0)),
            pl.BlockSpec((None, 1, d_model), lambda b, be, nu: (be[b], 0, 0)),
        ],
        out_specs=pl.BlockSpec((MOE_ROWS, d_model), lambda b, be, nu: (b, 0)),
        scratch_shapes=[pltpu.VMEM((d_model, two_de), BF16), pltpu.VMEM((d_exp, d_model), BF16),
                        pltpu.VMEM((MOE_ROWS, d_exp), BF16)],
    )
    return pl.pallas_call(
        functools.partial(_moe_kernel, d_exp),
        grid_spec=grid_spec,
        out_shape=jax.ShapeDtypeStruct((n_rows, d_model), F32),
        compiler_params=pltpu.CompilerParams(dimension_semantics=("arbitrary",), vmem_limit_bytes=VMEM_LIMIT),
        name="moe",
    )(block_e, n_used, x_rows, row_gate[:, None], w_gu, b_gu[:, None, :], w_down, b_down[:, None, :])


def _route(logits, n_exp):
    t = logits.shape[0]
    top_v, top_e = lax.top_k(logits, TOP_K)
    gates = jax.nn.softmax(top_v, axis=-1)
    n_assign = t * TOP_K
    flat_e = top_e.reshape(-1)
    flat_g = gates.reshape(-1)
    flat_tok = jnp.arange(n_assign, dtype=jnp.int32) // TOP_K
    order = jnp.argsort(flat_e, stable=True)
    sorted_e = flat_e[order]
    counts = jnp.bincount(flat_e, length=n_exp).astype(jnp.int32)
    padded = (counts + MOE_ROWS - 1) // MOE_ROWS * MOE_ROWS
    start = jnp.cumsum(counts) - counts
    pad_end = jnp.cumsum(padded)
    pad_start = pad_end - padded
    rank = jnp.arange(n_assign, dtype=jnp.int32) - start[sorted_e]
    dest = pad_start[sorted_e] + rank
    n_blocks = -(-n_assign // MOE_ROWS) + n_exp
    n_rows = n_blocks * MOE_ROWS
    row_tok = jnp.zeros((n_rows,), jnp.int32).at[dest].set(flat_tok[order])
    row_gate = jnp.zeros((n_rows,), F32).at[dest].set(flat_g[order])
    block_e = jnp.minimum(
        jnp.searchsorted(pad_end, jnp.arange(n_blocks, dtype=jnp.int32) * MOE_ROWS, side='right'),
        n_exp - 1).astype(jnp.int32)
    n_used = (pad_end[-1] // MOE_ROWS).astype(jnp.int32).reshape(1)
    dest_of_assign = jnp.zeros((n_assign,), jnp.int32).at[order].set(dest)
    return row_tok, row_gate, block_e, n_used, dest_of_assign


def kernel(x_prompt, x_sample, norm1, w_in, q_gain, k_gain, rel_bias, sgu_norm, w_s, b_s, out_norm_a, out_norm_b,
           w_o, norm2, w_router, b_router, w_gu, b_gu, w_down, b_down):
    assert norm1.shape[0] == 1, "single-layer trunk"
    bp, sp, d_model = x_prompt.shape
    bs, ss, _ = x_sample.shape
    xp = x_prompt.reshape(bp * sp, d_model)
    xs = x_sample.reshape(bs * ss, d_model)
    tp, t = bp * sp, bp * sp + bs * ss

    q, k, v, a_n = _inproj(xp, xs, norm1[0], w_in[0], q_gain[0], k_gain[0], sgu_norm[0], w_s[0], b_s[0],
                           out_norm_a[0])
    attn = _attention(q, k, v, rel_bias, tp, sp, ss)
    x1, h2, logits = _outproj(a_n, attn, xp, xs, out_norm_b[0], w_o[0], norm2[0], w_router[0], b_router[0])

    n_exp = w_router.shape[-1]
    row_tok, row_gate, block_e, n_used, dest_of_assign = _route(logits[:, :n_exp], n_exp)
    x_rows = jnp.take(h2, row_tok, axis=0)
    y_rows = _moe_experts(x_rows, row_gate, block_e, n_used, w_gu[0], b_gu[0], w_down[0], b_down[0])
    y = jnp.take(y_rows, dest_of_assign, axis=0).reshape(t, TOP_K, d_model).sum(axis=1)
    out = x1 + y
    return (out[:tp].reshape(bp, sp, d_model), out[tp:].reshape(bs, ss, d_model))
```

```python
import functools
import math

import numpy as np
import jax
import jax.numpy as jnp
from jax import lax
from jax.experimental import pallas as pl
from jax.experimental.pallas import tpu as pltpu

F32 = jnp.float32
BF16 = jnp.bfloat16

EPS = 1e-6
NEG_INF = -1e30
HEAD_DIM = 64
SGU_CHUNK = 128
WINDOWS = ((128, 1), (512, 4), (2048, 16))
N_BUCKETS = 32
MAX_DISTANCE = 1024
TOP_K = 4
SWIGLU_LIMIT = 7.0
SWIGLU_ALPHA = 1.702

LANES = 128
TOK_TILE = 512
ATT_TILE = 2048
ATT_HALO = 1024
ATT_QB = 128
ATT_SIDE = 64
MOE_ROWS = 512
VMEM_LIMIT = 56 * 1024 * 1024


def _dot(a, b):
    return jnp.dot(a, b, preferred_element_type=F32)


def _dot_nt(a, b):
    return lax.dot_general(a, b, (((1,), (1,)), ((), ())), preferred_element_type=F32)


def _rms(x, g):
    ms = jnp.mean(x * x, axis=-1, keepdims=True)
    return x * lax.rsqrt(ms + EPS) * g


def _gelu(x):
    return 0.5 * x * (1.0 + lax.erf(x * (1.0 / math.sqrt(2.0))))


def _split_bf16(x):
    hi = x.astype(BF16)
    lo = (x - hi.astype(F32)).astype(BF16)
    return hi, lo


def _inproj_kernel(n1, attn_w, sgu_w, xp_ref, xs_ref, norm1_ref, win_ref, gsum_ref, gexp_ref,
                   qkg_ref, sgun_ref, wsp_ref, bsb_ref, ona_ref, q_ref, k_ref, v_ref, a_ref):
    i = pl.program_id(0)
    x = jnp.where(i < n1, xp_ref[...], xs_ref[...])
    h = _rms(x, norm1_ref[...]).astype(BF16)

    zqk = _dot(h, win_ref[:, 0:2 * attn_w])
    sq_hi, sq_lo = _split_bf16(zqk * zqk)
    ss = _dot(sq_hi, gsum_ref[...]) + _dot(sq_lo, gsum_ref[...])
    inv = lax.rsqrt(ss * (1.0 / HEAD_DIM) + EPS)
    inv_hi, inv_lo = _split_bf16(inv)
    invb = _dot(inv_hi, gexp_ref[...]) + _dot(inv_lo, gexp_ref[...])
    qk = zqk * invb * qkg_ref[...]
    q_ref[...] = qk[:, 0:attn_w]
    k_ref[...] = qk[:, attn_w:2 * attn_w]
    v_ref[...] = _dot(h, win_ref[:, 2 * attn_w:3 * attn_w])

    c0 = 3 * attn_w
    u = _gelu(_dot(h, win_ref[:, c0:c0 + sgu_w]))
    gv = _gelu(_dot(h, win_ref[:, c0 + sgu_w:c0 + 2 * sgu_w]))
    vsn = _rms(gv, sgun_ref[...]).astype(BF16)
    lane = lax.broadcasted_iota(jnp.int32, (SGU_CHUNK, LANES), 1)
    lo_half = lane < HEAD_DIM
    zero = jnp.zeros((SGU_CHUNK, LANES), BF16)
    for c in range(TOK_TILE // SGU_CHUNK):
        r0 = c * SGU_CHUNK
        parts = []
        for j in range(sgu_w // LANES):
            blk = vsn[r0:r0 + SGU_CHUNK, j * LANES:(j + 1) * LANES]
            rhs = jnp.concatenate([jnp.where(lo_half, blk, zero), jnp.where(lo_half, zero, blk)], axis=0)
            parts.append(_dot(wsp_ref[j], rhs))
        s = jnp.concatenate(parts, axis=1) + bsb_ref[...]
        a = u[r0:r0 + SGU_CHUNK, :] * s
        a_ref[r0:r0 + SGU_CHUNK, :] = _rms(a, ona_ref[...]).astype(BF16)


def _inproj(xp, xs, norm1, w_in, q_gain, k_gain, sgu_norm, w_s, b_s, out_norm_a):
    tp, d_model = xp.shape
    ts = xs.shape[0]
    t = tp + ts
    n_heads_w = q_gain.shape[0]
    sgu_w = sgu_norm.shape[0]
    attn_w = (w_in.shape[1] - 2 * sgu_w) // 3
    n_heads = attn_w // n_heads_w
    n_groups = w_s.shape[0]
    assert n_heads_w == HEAD_DIM and sgu_w // n_groups == HEAD_DIM and w_s.shape[1] == SGU_CHUNK
    assert tp % TOK_TILE == 0 and ts % TOK_TILE == 0 and 2 * n_heads <= LANES
    n1 = tp // TOK_TILE

    heads = np.arange(2 * attn_w) // HEAD_DIM
    gsum = (heads[:, None] == np.arange(LANES)[None, :]).astype(np.float32)
    gexp = gsum.T
    qkg = jnp.concatenate([jnp.tile(q_gain, n_heads) * (HEAD_DIM ** -0.5), jnp.tile(k_gain, n_heads)])
    wsp = jnp.concatenate([w_s[0::2], w_s[1::2]], axis=2).astype(BF16)
    bsb = jnp.repeat(b_s.T, HEAD_DIM, axis=1)

    const = lambda shape: pl.BlockSpec(shape, lambda i: (0,) * len(shape))
    tok = lambda w: pl.BlockSpec((TOK_TILE, w), lambda i: (i, 0))
    return pl.pallas_call(
        functools.partial(_inproj_kernel, n1, attn_w, sgu_w),
        grid=(t // TOK_TILE,),
        in_specs=[
            pl.BlockSpec((TOK_TILE, d_model), lambda i: (jnp.minimum(i, n1 - 1), 0)),
            pl.BlockSpec((TOK_TILE, d_model), lambda i: (jnp.maximum(i - n1, 0), 0)),
            const((1, d_model)), const(w_in.shape), const((2 * attn_w, LANES)), const((LANES, 2 * attn_w)),
            const((1, 2 * attn_w)), const((1, sgu_w)), const(wsp.shape), const((SGU_CHUNK, sgu_w)),
            const((1, sgu_w)),
        ],
        out_specs=[tok(attn_w), tok(attn_w), tok(attn_w), tok(sgu_w)],
        out_shape=[jax.ShapeDtypeStruct((t, attn_w), F32)] * 3 + [jax.ShapeDtypeStruct((t, sgu_w), BF16)],
        compiler_params=pltpu.CompilerParams(dimension_semantics=("arbitrary",), vmem_limit_bytes=VMEM_LIMIT),
        name="inproj",
    )(xp, xs, norm1[None], w_in.astype(BF16), jnp.asarray(gsum, BF16), jnp.asarray(gexp, BF16),
      qkg[None], sgu_norm[None], wsp, bsb, out_norm_a[None])


def _t5_bucket(rel):
    nb = N_BUCKETS // 2
    bucket = (rel > 0).astype(np.int32) * nb
    n = np.abs(rel)
    max_exact = nb // 2
    large = max_exact + (np.log(np.maximum(n, 1) / max_exact)
                         / np.log(MAX_DISTANCE / max_exact) * (nb - max_exact)).astype(np.int32)
    large = np.minimum(large, nb - 1)
    return (bucket + np.where(n < max_exact, n, large)).astype(np.int32)


def _branch_layout():
    out, kbase = [], 0
    for bi, (window, dil) in enumerate(WINDOWS):
        assert window // (2 * dil) == ATT_SIDE
        m = ATT_TILE // dil
        seg = m + 2 * ATT_SIDE
        out.append((dil, m, seg, kbase, bi * ATT_TILE))
        kbase += dil * seg
    return out, kbase


def _strided(ref, start, size, stride):
    if stride == 1:
        return ref[start:start + size, :]
    return ref[pl.ds(start, size, stride=stride), :]


def _attn_kernel(tiles_p, per_seq_p, per_seq_s, q_ref, km_ref, kp_ref, kn_ref, vm_ref, vp_ref, vn_ref,
                 bias_ref, o_ref, qs, ks, vs, o_scr, l_scr):
    i = pl.program_id(0)
    w = jnp.where(i < tiles_p, i % per_seq_p, (i - tiles_p) % per_seq_s)
    last = jnp.where(i < tiles_p, per_seq_p - 1, per_seq_s - 1)
    left_ok = w > 0
    right_ok = w < last
    layout, _ = _branch_layout()

    for dil, m, seg, kbase, qbase in layout:
        for r in range(dil):
            o = kbase + r * seg
            for dst, main, prev, nxt in ((ks, km_ref, kp_ref, kn_ref), (vs, vm_ref, vp_ref, vn_ref)):
                dst[o:o + ATT_SIDE, :] = _strided(prev, ATT_HALO - ATT_SIDE * dil + r, ATT_SIDE, dil).astype(BF16)
                dst[o + ATT_SIDE:o + ATT_SIDE + m, :] = _strided(main, r, m, dil).astype(BF16)
                dst[o + ATT_SIDE + m:o + seg, :] = _strided(nxt, r, ATT_SIDE, dil).astype(BF16)
            qs[qbase + r * m:qbase + (r + 1) * m, :] = _strided(q_ref, r, m, dil).astype(BF16)

    lane = lax.broadcasted_iota(jnp.int32, (ATT_QB, LANES), 1)
    head0 = lane < HEAD_DIM
    col = lax.broadcasted_iota(jnp.int32, (1, 2 * ATT_QB), 1)
    for bi, (dil, m, seg, kbase, qbase) in enumerate(layout):
        nblk = m // ATT_QB

        def cell(idx, carry, bi=bi, nblk=nblk, seg=seg, kbase=kbase, qbase=qbase):
            r = idx // nblk
            blk = idx % nblk
            qoff = pl.multiple_of(qbase + idx * ATT_QB, ATT_QB)
            koff = pl.multiple_of(kbase + r * seg + blk * ATT_QB, ATT_QB)
            left_bad = jnp.logical_and(blk == 0, jnp.logical_not(left_ok))
            right_bad = jnp.logical_and(blk == nblk - 1, jnp.logical_not(right_ok))
            edge = (jnp.where(jnp.logical_and(col < ATT_SIDE, left_bad), NEG_INF, 0.0)
                    + jnp.where(jnp.logical_and(col >= 2 * ATT_QB - ATT_SIDE, right_bad), NEG_INF, 0.0))
            qc = qs[pl.ds(qoff, ATT_QB), :]
            kc = ks[pl.ds(koff, 2 * ATT_QB), :]
            vc = vs[pl.ds(koff, 2 * ATT_QB), :]
            zero = jnp.zeros_like(qc)
            outs, lses = [], []
            for hsel in (head0, jnp.logical_not(head0)):
                s = _dot_nt(jnp.where(hsel, qc, zero), kc) + bias_ref[len(outs), bi] + edge
                mx = jnp.max(s, axis=-1, keepdims=True)
                p = jnp.exp(s - mx)
                den = jnp.sum(p, axis=-1, keepdims=True)
                outs.append(_dot(p.astype(BF16), vc) / den)
                lses.append(mx + jnp.log(den))
            o_scr[pl.ds(qoff, ATT_QB), :] = jnp.where(head0, outs[0], outs[1])
            l_scr[pl.ds(qoff, ATT_QB), :] = jnp.where(head0, lses[0], lses[1])
            return carry

        lax.fori_loop(0, dil * nblk, cell, 0)

    big = WINDOWS[-1][1]
    rows = ATT_TILE // big
    for r in range(big):
        os_, ls_ = [], []
        for dil, m, seg, kbase, qbase in layout:
            start = qbase + (r % dil) * m + r // dil
            os_.append(_strided(o_scr, start, rows, big // dil))
            ls_.append(_strided(l_scr, start, rows, big // dil))
        mx = jnp.maximum(jnp.maximum(ls_[0], ls_[1]), ls_[2])
        ws = [jnp.exp(l - mx) for l in ls_]
        num = ws[0] * os_[0] + ws[1] * os_[1] + ws[2] * os_[2]
        o_ref[pl.ds(r, rows, stride=big), :] = num / (ws[0] + ws[1] + ws[2])


def _attention(q, k, v, rel_bias, tp, seq_p, seq_s):
    t, attn_w = q.shape
    n_heads = attn_w // HEAD_DIM
    assert seq_p % ATT_TILE == 0 and seq_s % ATT_TILE == 0 and ATT_TILE == 2 * ATT_HALO
    assert ATT_TILE // WINDOWS[-1][1] == ATT_QB
    layout, krows = _branch_layout()

    ii = np.arange(ATT_QB)[:, None]
    jj = np.arange(2 * ATT_QB)[None, :]
    rel = jj - ATT_SIDE - ii
    band = np.abs(rel) <= ATT_SIDE
    buckets = np.stack([_t5_bucket(rel * dil) for _, dil in WINDOWS])
    bias = jnp.transpose(rel_bias[jnp.asarray(buckets)], (3, 0, 1, 2)).astype(F32)
    bias = jnp.where(jnp.asarray(band)[None, None], bias, NEG_INF)

    halo_blocks = t // ATT_HALO
    per_tile = ATT_TILE // ATT_HALO
    main = pl.BlockSpec((ATT_TILE, LANES), lambda i, j: (i, j))
    prev = pl.BlockSpec((ATT_HALO, LANES), lambda i, j: (jnp.maximum(i * per_tile - 1, 0), j))
    nxt = pl.BlockSpec((ATT_HALO, LANES), lambda i, j: (jnp.minimum((i + 1) * per_tile, halo_blocks - 1), j))
    n_q = len(WINDOWS) * ATT_TILE
    return pl.pallas_call(
        functools.partial(_attn_kernel, tp // ATT_TILE, seq_p // ATT_TILE, seq_s // ATT_TILE),
        grid=(t // ATT_TILE, attn_w // LANES),
        in_specs=[main, main, prev, nxt, main, prev, nxt,
                  pl.BlockSpec((LANES // HEAD_DIM, len(WINDOWS), ATT_QB, 2 * ATT_QB), lambda i, j: (j, 0, 0, 0))],
        out_specs=main,
        out_shape=jax.ShapeDtypeStruct((t, attn_w), F32),
        scratch_shapes=[pltpu.VMEM((n_q, LANES), BF16), pltpu.VMEM((krows, LANES), BF16),
                        pltpu.VMEM((krows, LANES), BF16), pltpu.VMEM((n_q, LANES), F32),
                        pltpu.VMEM((n_q, LANES), F32)],
        compiler_params=pltpu.CompilerParams(dimension_semantics=("arbitrary", "arbitrary"),
                                             vmem_limit_bytes=VMEM_LIMIT),
        name="attn",
    )(q, k, k, k, v, v, v, bias)


def _outproj_kernel(n1, sgu_w, n_exp, a_ref, attn_ref, xp_ref, xs_ref, onb_ref, wo_ref, norm2_ref, wrh_ref, wrl_ref,
                    br_ref, tri_ref, x1_ref, h2_ref, meta_ref, cnt_ref, carry):
    i = pl.program_id(0)

    @pl.when(i == 0)
    def _():
        carry[...] = jnp.zeros_like(carry)

    x = jnp.where(i < n1, xp_ref[...], xs_ref[...])
    bn = _rms(attn_ref[...], onb_ref[...]).astype(BF16)
    x1 = x + _dot(a_ref[...], wo_ref[0:sgu_w, :]) + _dot(bn, wo_ref[sgu_w:, :])
    x1_ref[...] = x1
    h2 = _rms(x1, norm2_ref[...])
    h2_ref[...] = h2.astype(BF16)
    hi, lo = _split_bf16(h2)
    logits = _dot(hi, wrh_ref[...]) + _dot(lo, wrh_ref[...]) + _dot(hi, wrl_ref[...]) + br_ref[...]

    lane = lax.broadcasted_iota(jnp.int32, logits.shape, 1)
    work = jnp.where(lane < n_exp, logits, -jnp.inf)
    chosen = jnp.zeros(logits.shape, jnp.bool_)
    experts, values = [], []
    for _ in range(TOP_K):
        top = jnp.max(work, axis=-1, keepdims=True)
        idx = jnp.min(jnp.where(work == top, lane, LANES), axis=-1, keepdims=True)
        hit = lane == idx
        chosen = jnp.logical_or(chosen, hit)
        work = jnp.where(hit, -jnp.inf, work)
        experts.append(idx)
        values.append(top)
    exps = [jnp.exp(v - values[0]) for v in values]
    den = exps[0] + exps[1] + exps[2] + exps[3]

    onehot = jnp.where(chosen, 1.0, 0.0)
    before = _dot(tri_ref[...], onehot.astype(BF16)) + carry[0:1, :]
    carry[0:1, :] = carry[0:1, :] + jnp.sum(onehot, axis=0, keepdims=True)
    cnt_ref[...] = carry[...]

    meta = jnp.zeros(logits.shape, F32)
    for kk in range(TOP_K):
        rank = jnp.sum(jnp.where(lane == experts[kk], before, 0.0), axis=-1, keepdims=True)
        meta = jnp.where(lane == kk, experts[kk].astype(F32), meta)
        meta = jnp.where(lane == TOP_K + kk, exps[kk] / den, meta)
        meta = jnp.where(lane == 2 * TOP_K + kk, rank, meta)
    meta_ref[...] = meta


def _outproj(a_n, attn, xp, xs, out_norm_b, w_o, norm2, w_router, b_router):
    t, sgu_w = a_n.shape
    attn_w = attn.shape[1]
    d_model = w_o.shape[1]
    n_exp = w_router.shape[1]
    assert n_exp <= LANES
    n1 = xp.shape[0] // TOK_TILE
    wr = jnp.pad(w_router, ((0, 0), (0, LANES - n_exp)))
    wr_hi = wr.astype(BF16)
    wr_lo = (wr - wr_hi.astype(F32)).astype(BF16)
    br = jnp.pad(b_router, (0, LANES - n_exp))[None]
    tri = np.tril(np.ones((TOK_TILE, TOK_TILE), np.float32), -1)

    const = lambda shape: pl.BlockSpec(shape, lambda i: (0,) * len(shape))
    tok = lambda w: pl.BlockSpec((TOK_TILE, w), lambda i: (i, 0))
    return pl.pallas_call(
        functools.partial(_outproj_kernel, n1, sgu_w, n_exp),
        grid=(t // TOK_TILE,),
        in_specs=[
            tok(sgu_w), tok(attn_w),
            pl.BlockSpec((TOK_TILE, d_model), lambda i: (jnp.minimum(i, n1 - 1), 0)),
            pl.BlockSpec((TOK_TILE, d_model), lambda i: (jnp.maximum(i - n1, 0), 0)),
            const((1, attn_w)), const(w_o.shape), const((1, d_model)), const((d_model, LANES)),
            const((d_model, LANES)), const((1, LANES)), const((TOK_TILE, TOK_TILE)),
        ],
        out_specs=[tok(d_model), tok(d_model), tok(LANES), const((8, LANES))],
        out_shape=[jax.ShapeDtypeStruct((t, d_model), F32), jax.ShapeDtypeStruct((t, d_model), BF16),
                   jax.ShapeDtypeStruct((t, LANES), F32), jax.ShapeDtypeStruct((8, LANES), F32)],
        scratch_shapes=[pltpu.VMEM((8, LANES), F32)],
        compiler_params=pltpu.CompilerParams(dimension_semantics=("arbitrary",), vmem_limit_bytes=VMEM_LIMIT),
        name="outproj",
    )(a_n, attn, xp, xs, out_norm_b[None], w_o.astype(BF16), norm2[None], wr_hi, wr_lo, br, jnp.asarray(tri, BF16))


def _moe_kernel(d_exp, blk_ref, exp_ref, lo_ref, hi_ref, x_ref, wgu_ref, bgu_ref, wd_ref, bd_ref, y_ref,
                wgu_s, wd_s, act_s):
    it = pl.program_id(0)
    prev = jnp.maximum(it - 1, 0)
    new_expert = jnp.logical_or(it == 0, exp_ref[it] != exp_ref[prev])
    new_block = jnp.logical_or(it == 0, blk_ref[it] != blk_ref[prev])
    lo, hi = lo_ref[it], hi_ref[it]
    active = hi > lo

    @pl.when(jnp.logical_and(new_expert, active))
    def _():
        wgu_s[...] = wgu_ref[...].astype(BF16)
        wd_s[...] = wd_ref[...].astype(BF16)

    @pl.when(active)
    def _():
        x = x_ref[...]
        step = 512
        for n in range(0, d_exp, step):
            gate = _dot(x, wgu_s[:, n:n + step]) + bgu_ref[:, n:n + step]
            up = _dot(x, wgu_s[:, d_exp + n:d_exp + n + step]) + bgu_ref[:, d_exp + n:d_exp + n + step]
            gate = jnp.minimum(gate, SWIGLU_LIMIT)
            up = jnp.clip(up, -SWIGLU_LIMIT, SWIGLU_LIMIT)
            glu = gate * jax.nn.sigmoid(SWIGLU_ALPHA * gate)
            act_s[:, n:n + step] = ((up + 1.0) * glu).astype(BF16)
        y = (_dot(act_s[...], wd_s[...]) + bd_ref[...]).astype(y_ref.dtype)
        row = lax.broadcasted_iota(jnp.int32, (MOE_ROWS, 1), 0)
        mine = jnp.logical_and(row >= lo, row < hi)

        @pl.when(new_block)
        def _():
            y_ref[...] = jnp.where(mine, y, jnp.zeros_like(y))

        @pl.when(jnp.logical_not(new_block))
        def _():
            y_ref[...] = jnp.where(mine, y, y_ref[...])


def _moe_experts(x_rows, items, w_gu, b_gu, w_down, b_down):
    n_rows, d_model = x_rows.shape
    n_exp, _, two_de = w_gu.shape
    d_exp = two_de // 2
    n_items = items[0].shape[0]
    grid_spec = pltpu.PrefetchScalarGridSpec(
        num_scalar_prefetch=4,
        grid=(n_items,),
        in_specs=[
            pl.BlockSpec((MOE_ROWS, d_model), lambda i, blk, ex, lo, hi: (blk[i], 0)),
            pl.BlockSpec((None, d_model, two_de), lambda i, blk, ex, lo, hi: (ex[i], 0, 0)),
            pl.BlockSpec((None, 1, two_de), lambda i, blk, ex, lo, hi: (ex[i], 0, 0)),
            pl.BlockSpec((None, d_exp, d_model), lambda i, blk, ex, lo, hi: (ex[i], 0, 0)),
            pl.BlockSpec((None, 1, d_model), lambda i, blk, ex, lo, hi: (ex[i], 0, 0)),
        ],
        out_specs=pl.BlockSpec((MOE_ROWS, d_model), lambda i, blk, ex, lo, hi: (blk[i], 0)),
        scratch_shapes=[pltpu.VMEM((d_model, two_de), BF16), pltpu.VMEM((d_exp, d_model), BF16),
                        pltpu.VMEM((MOE_ROWS, d_exp), BF16)],
    )
    return pl.pallas_call(
        functools.partial(_moe_kernel, d_exp),
        grid_spec=grid_spec,
        out_shape=jax.ShapeDtypeStruct((n_rows, d_model), BF16),
        compiler_params=pltpu.CompilerParams(dimension_semantics=("arbitrary",), vmem_limit_bytes=VMEM_LIMIT),
        name="moe",
    )(*items, x_rows, w_gu, b_gu[:, None, :], w_down, b_down[:, None, :])


def _route(meta, counts_f, n_exp):
    t = meta.shape[0]
    n_assign = t * TOP_K
    assert n_assign % MOE_ROWS == 0
    top_e = meta[:, 0:TOP_K].astype(jnp.int32)
    rank = meta[:, 2 * TOP_K:3 * TOP_K].astype(jnp.int32)
    counts = counts_f[0, :n_exp].astype(jnp.int32)
    end = jnp.cumsum(counts)
    start = end - counts
    experts = jnp.arange(n_exp, dtype=jnp.int32)
    start_of = jnp.sum(jnp.where(top_e[..., None] == experts, start, 0), axis=-1)
    pos = start_of + rank

    shift = (n_assign - 1).bit_length()
    assert (n_exp << shift) < 2 ** 31
    keys = (top_e.reshape(-1) << shift) + jnp.arange(n_assign, dtype=jnp.int32)
    sorted_tok = (jnp.sort(keys) & ((1 << shift) - 1)) // TOP_K

    n_blocks = n_assign // MOE_ROWS
    n_items = n_blocks + n_exp
    first_blk = start // MOE_ROWS
    n_it = jnp.where(counts > 0, (end - 1) // MOE_ROWS - first_blk + 1, 0)
    it_end = jnp.cumsum(n_it)
    it_start = it_end - n_it
    i = jnp.arange(n_items, dtype=jnp.int32)
    valid = i < it_end[-1]
    e_i = jnp.minimum(jnp.searchsorted(it_end, i, side='right'), n_exp - 1).astype(jnp.int32)
    e_last = e_i[jnp.maximum(it_end[-1] - 1, 0)]
    e_i = jnp.where(valid, e_i, e_last)
    blk_i = jnp.where(valid, first_blk[e_i] + i - it_start[e_i], n_blocks - 1)
    lo_i = jnp.where(valid, jnp.clip(start[e_i] - blk_i * MOE_ROWS, 0, MOE_ROWS), 0)
    hi_i = jnp.where(valid, jnp.clip(end[e_i] - blk_i * MOE_ROWS, 0, MOE_ROWS), 0)
    items = tuple(a.astype(jnp.int32) for a in (blk_i, e_i, lo_i, hi_i))
    return pos, sorted_tok, items


def _combine_kernel(d_model, x1_ref, yg_ref, meta_ref, o_ref):
    acc = x1_ref[...]
    for kk in range(TOP_K):
        gate = meta_ref[:, TOP_K + kk:TOP_K + kk + 1]
        acc = acc + gate * yg_ref[:, kk * d_model:(kk + 1) * d_model].astype(F32)
    o_ref[...] = acc


def _combine(x1, yg, meta, row0, rows):
    d_model = x1.shape[1]
    b0 = row0 // TOK_TILE
    tok = lambda w: pl.BlockSpec((TOK_TILE, w), lambda i: (i + b0, 0))
    return pl.pallas_call(
        functools.partial(_combine_kernel, d_model),
        grid=(rows // TOK_TILE,),
        in_specs=[tok(d_model), tok(TOP_K * d_model), tok(LANES)],
        out_specs=pl.BlockSpec((TOK_TILE, d_model), lambda i: (i, 0)),
        out_shape=jax.ShapeDtypeStruct((rows, d_model), F32),
        compiler_params=pltpu.CompilerParams(dimension_semantics=("arbitrary",), vmem_limit_bytes=VMEM_LIMIT),
        name="combine",
    )(x1, yg, meta)


def kernel(x_prompt, x_sample, norm1, w_in, q_gain, k_gain, rel_bias, sgu_norm, w_s, b_s, out_norm_a, out_norm_b,
           w_o, norm2, w_router, b_router, w_gu, b_gu, w_down, b_down):
    assert norm1.shape[0] == 1, "single-layer trunk"
    bp, sp, d_model = x_prompt.shape
    bs, ss, _ = x_sample.shape
    xp = x_prompt.reshape(bp * sp, d_model)
    xs = x_sample.reshape(bs * ss, d_model)
    tp, t = bp * sp, bp * sp + bs * ss

    q, k, v, a_n = _inproj(xp, xs, norm1[0], w_in[0], q_gain[0], k_gain[0], sgu_norm[0], w_s[0], b_s[0],
                           out_norm_a[0])
    attn = _attention(q, k, v, rel_bias, tp, sp, ss)
    x1, h2, meta, counts = _outproj(a_n, attn, xp, xs, out_norm_b[0], w_o[0], norm2[0], w_router[0], b_router[0])

    pos, sorted_tok, items = _route(meta, counts, w_router.shape[-1])
    x_rows = jnp.take(h2, sorted_tok, axis=0)
    y_rows = _moe_experts(x_rows, items, w_gu[0], b_gu[0], w_down[0], b_down[0])
    yg = jnp.take(y_rows, pos.reshape(-1), axis=0).reshape(t, TOP_K * d_model)
    out_p = _combine(x1, yg, meta, 0, tp)
    out_s = _combine(x1, yg, meta, tp, t - tp)
    return (out_p.reshape(bp, sp, d_model), out_s.reshape(bs, ss, d_model))
```

```python
import functools
import math

import numpy as np
import jax
import jax.numpy as jnp
from jax import lax
from jax.experimental import pallas as pl
from jax.experimental.pallas import tpu as pltpu

F32 = jnp.float32
BF16 = jnp.bfloat16

EPS = 1e-6
NEG_INF = -1e30
HEAD_DIM = 64
SGU_CHUNK = 128
WINDOWS = ((128, 1), (512, 4), (2048, 16))
N_BUCKETS = 32
MAX_DISTANCE = 1024
TOP_K = 4
SWIGLU_LIMIT = 7.0
SWIGLU_ALPHA = 1.702

LANES = 128
TOK_TILE = 512
ATT_TILE = 2048
ATT_HALO = 1024
ATT_QB = 128
ATT_SIDE = 64
MOE_ROWS = 512
VMEM_LIMIT = 56 * 1024 * 1024


def _dot(a, b):
    return jnp.dot(a, b, preferred_element_type=F32)


def _dot_nt(a, b):
    return lax.dot_general(a, b, (((1,), (1,)), ((), ())), preferred_element_type=F32)


def _rms(x, g):
    ms = jnp.mean(x * x, axis=-1, keepdims=True)
    return x * lax.rsqrt(ms + EPS) * g


def _gelu(x):
    return 0.5 * x * (1.0 + lax.erf(x * (1.0 / math.sqrt(2.0))))


def _split_bf16(x):
    hi = x.astype(BF16)
    lo = (x - hi.astype(F32)).astype(BF16)
    return hi, lo


def _inproj_kernel(n1, attn_w, sgu_w, xp_ref, xs_ref, norm1_ref, win_ref, gsum_ref, gexp_ref,
                   qkg_ref, sgun_ref, wsp_ref, bsb_ref, ona_ref, q_ref, k_ref, v_ref, a_ref):
    i = pl.program_id(0)
    x = jnp.where(i < n1, xp_ref[...], xs_ref[...])
    h = _rms(x, norm1_ref[...]).astype(BF16)

    zqk = _dot(h, win_ref[:, 0:2 * attn_w])
    sq_hi, sq_lo = _split_bf16(zqk * zqk)
    ss = _dot(sq_hi, gsum_ref[...]) + _dot(sq_lo, gsum_ref[...])
    inv = lax.rsqrt(ss * (1.0 / HEAD_DIM) + EPS)
    inv_hi, inv_lo = _split_bf16(inv)
    invb = _dot(inv_hi, gexp_ref[...]) + _dot(inv_lo, gexp_ref[...])
    qk = zqk * invb * qkg_ref[...]
    q_ref[...] = qk[:, 0:attn_w]
    k_ref[...] = qk[:, attn_w:2 * attn_w]
    v_ref[...] = _dot(h, win_ref[:, 2 * attn_w:3 * attn_w])

    c0 = 3 * attn_w
    u = _gelu(_dot(h, win_ref[:, c0:c0 + sgu_w]))
    gv = _gelu(_dot(h, win_ref[:, c0 + sgu_w:c0 + 2 * sgu_w]))
    vsn = _rms(gv, sgun_ref[...]).astype(BF16)
    lane = lax.broadcasted_iota(jnp.int32, (SGU_CHUNK, LANES), 1)
    lo_half = lane < HEAD_DIM
    zero = jnp.zeros((SGU_CHUNK, LANES), BF16)
    for c in range(TOK_TILE // SGU_CHUNK):
        r0 = c * SGU_CHUNK
        parts = []
        for j in range(sgu_w // LANES):
            blk = vsn[r0:r0 + SGU_CHUNK, j * LANES:(j + 1) * LANES]
            rhs = jnp.concatenate([jnp.where(lo_half, blk, zero), jnp.where(lo_half, zero, blk)], axis=0)
            parts.append(_dot(wsp_ref[j], rhs))
        s = jnp.concatenate(parts, axis=1) + bsb_ref[...]
        a = u[r0:r0 + SGU_CHUNK, :] * s
        a_ref[r0:r0 + SGU_CHUNK, :] = _rms(a, ona_ref[...]).astype(BF16)


def _inproj(xp, xs, norm1, w_in, q_gain, k_gain, sgu_norm, w_s, b_s, out_norm_a):
    tp, d_model = xp.shape
    ts = xs.shape[0]
    t = tp + ts
    n_heads_w = q_gain.shape[0]
    sgu_w = sgu_norm.shape[0]
    attn_w = (w_in.shape[1] - 2 * sgu_w) // 3
    n_heads = attn_w // n_heads_w
    n_groups = w_s.shape[0]
    assert n_heads_w == HEAD_DIM and sgu_w // n_groups == HEAD_DIM and w_s.shape[1] == SGU_CHUNK
    assert tp % TOK_TILE == 0 and ts % TOK_TILE == 0 and 2 * n_heads <= LANES
    n1 = tp // TOK_TILE

    heads = np.arange(2 * attn_w) // HEAD_DIM
    gsum = (heads[:, None] == np.arange(LANES)[None, :]).astype(np.float32)
    gexp = gsum.T
    qkg = jnp.concatenate([jnp.tile(q_gain, n_heads) * (HEAD_DIM ** -0.5), jnp.tile(k_gain, n_heads)])
    wsp = jnp.concatenate([w_s[0::2], w_s[1::2]], axis=2).astype(BF16)
    bsb = jnp.repeat(b_s.T, HEAD_DIM, axis=1)

    const = lambda shape: pl.BlockSpec(shape, lambda i: (0,) * len(shape))
    tok = lambda w: pl.BlockSpec((TOK_TILE, w), lambda i: (i, 0))
    return pl.pallas_call(
        functools.partial(_inproj_kernel, n1, attn_w, sgu_w),
        grid=(t // TOK_TILE,),
        in_specs=[
            pl.BlockSpec((TOK_TILE, d_model), lambda i: (jnp.minimum(i, n1 - 1), 0)),
            pl.BlockSpec((TOK_TILE, d_model), lambda i: (jnp.maximum(i - n1, 0), 0)),
            const((1, d_model)), const(w_in.shape), const((2 * attn_w, LANES)), const((LANES, 2 * attn_w)),
            const((1, 2 * attn_w)), const((1, sgu_w)), const(wsp.shape), const((SGU_CHUNK, sgu_w)),
            const((1, sgu_w)),
        ],
        out_specs=[tok(attn_w), tok(attn_w), tok(attn_w), tok(sgu_w)],
        out_shape=[jax.ShapeDtypeStruct((t, attn_w), F32)] * 3 + [jax.ShapeDtypeStruct((t, sgu_w), BF16)],
        compiler_params=pltpu.CompilerParams(dimension_semantics=("arbitrary",), vmem_limit_bytes=VMEM_LIMIT),
        name="inproj",
    )(xp, xs, norm1[None], w_in.astype(BF16), jnp.asarray(gsum, BF16), jnp.asarray(gexp, BF16),
      qkg[None], sgu_norm[None], wsp, bsb, out_norm_a[None])


def _t5_bucket(rel):
    nb = N_BUCKETS // 2
    bucket = (rel > 0).astype(np.int32) * nb
    n = np.abs(rel)
    max_exact = nb // 2
    large = max_exact + (np.log(np.maximum(n, 1) / max_exact)
                         / np.log(MAX_DISTANCE / max_exact) * (nb - max_exact)).astype(np.int32)
    large = np.minimum(large, nb - 1)
    return (bucket + np.where(n < max_exact, n, large)).astype(np.int32)


def _branch_layout():
    out, kbase = [], 0
    for bi, (window, dil) in enumerate(WINDOWS):
        assert window // (2 * dil) == ATT_SIDE
        m = ATT_TILE // dil
        seg = m + 2 * ATT_SIDE
        out.append((dil, m, seg, kbase, bi * ATT_TILE))
        kbase += dil * seg
    return out, kbase


def _strided(ref, start, size, stride):
    if stride == 1:
        return ref[start:start + size, :]
    return ref[pl.ds(start, size, stride=stride), :]


def _attn_kernel(tiles_p, per_seq_p, per_seq_s, q_ref, km_ref, kp_ref, kn_ref, vm_ref, vp_ref, vn_ref,
                 bias_ref, o_ref, qs, ks, vs, o_scr, l_scr):
    i = pl.program_id(0)
    w = jnp.where(i < tiles_p, i % per_seq_p, (i - tiles_p) % per_seq_s)
    last = jnp.where(i < tiles_p, per_seq_p - 1, per_seq_s - 1)
    left_ok = w > 0
    right_ok = w < last
    layout, _ = _branch_layout()

    for dil, m, seg, kbase, qbase in layout:
        for r in range(dil):
            o = kbase + r * seg
            for dst, main, prev, nxt in ((ks, km_ref, kp_ref, kn_ref), (vs, vm_ref, vp_ref, vn_ref)):
                dst[o:o + ATT_SIDE, :] = _strided(prev, ATT_HALO - ATT_SIDE * dil + r, ATT_SIDE, dil).astype(BF16)
                dst[o + ATT_SIDE:o + ATT_SIDE + m, :] = _strided(main, r, m, dil).astype(BF16)
                dst[o + ATT_SIDE + m:o + seg, :] = _strided(nxt, r, ATT_SIDE, dil).astype(BF16)
            qs[qbase + r * m:qbase + (r + 1) * m, :] = _strided(q_ref, r, m, dil).astype(BF16)

    lane = lax.broadcasted_iota(jnp.int32, (ATT_QB, LANES), 1)
    head0 = lane < HEAD_DIM
    col = lax.broadcasted_iota(jnp.int32, (1, 2 * ATT_QB), 1)
    for bi, (dil, m, seg, kbase, qbase) in enumerate(layout):
        nblk = m // ATT_QB

        def cell(idx, carry, bi=bi, nblk=nblk, seg=seg, kbase=kbase, qbase=qbase):
            r = idx // nblk
            blk = idx % nblk
            qoff = pl.multiple_of(qbase + idx * ATT_QB, ATT_QB)
            koff = pl.multiple_of(kbase + r * seg + blk * ATT_QB, ATT_QB)
            left_bad = jnp.logical_and(blk == 0, jnp.logical_not(left_ok))
            right_bad = jnp.logical_and(blk == nblk - 1, jnp.logical_not(right_ok))
            edge = (jnp.where(jnp.logical_and(col < ATT_SIDE, left_bad), NEG_INF, 0.0)
                    + jnp.where(jnp.logical_and(col >= 2 * ATT_QB - ATT_SIDE, right_bad), NEG_INF, 0.0))
            qc = qs[pl.ds(qoff, ATT_QB), :]
            kc = ks[pl.ds(koff, 2 * ATT_QB), :]
            vc = vs[pl.ds(koff, 2 * ATT_QB), :]
            zero = jnp.zeros_like(qc)
            outs, lses = [], []
            for hsel in (head0, jnp.logical_not(head0)):
                s = _dot_nt(jnp.where(hsel, qc, zero), kc) + bias_ref[len(outs), bi] + edge
                mx = jnp.max(s, axis=-1, keepdims=True)
                p = jnp.exp(s - mx)
                den = jnp.sum(p, axis=-1, keepdims=True)
                outs.append(_dot(p.astype(BF16), vc) / den)
                lses.append(mx + jnp.log(den))
            o_scr[pl.ds(qoff, ATT_QB), :] = jnp.where(head0, outs[0], outs[1])
            l_scr[pl.ds(qoff, ATT_QB), :] = jnp.where(head0, lses[0], lses[1])
            return carry

        lax.fori_loop(0, dil * nblk, cell, 0)

    big = WINDOWS[-1][1]
    rows = ATT_TILE // big
    for r in range(big):
        os_, ls_ = [], []
        for dil, m, seg, kbase, qbase in layout:
            start = qbase + (r % dil) * m + r // dil
            os_.append(_strided(o_scr, start, rows, big // dil))
            ls_.append(_strided(l_scr, start, rows, big // dil))
        mx = jnp.maximum(jnp.maximum(ls_[0], ls_[1]), ls_[2])
        ws = [jnp.exp(l - mx) for l in ls_]
        num = ws[0] * os_[0] + ws[1] * os_[1] + ws[2] * os_[2]
        o_ref[pl.ds(r, rows, stride=big), :] = num / (ws[0] + ws[1] + ws[2])


def _attention(q, k, v, rel_bias, tp, seq_p, seq_s):
    t, attn_w = q.shape
    n_heads = attn_w // HEAD_DIM
    assert seq_p % ATT_TILE == 0 and seq_s % ATT_TILE == 0 and ATT_TILE == 2 * ATT_HALO
    assert ATT_TILE // WINDOWS[-1][1] == ATT_QB
    layout, krows = _branch_layout()

    ii = np.arange(ATT_QB)[:, None]
    jj = np.arange(2 * ATT_QB)[None, :]
    rel = jj - ATT_SIDE - ii
    band = np.abs(rel) <= ATT_SIDE
    buckets = np.stack([_t5_bucket(rel * dil) for _, dil in WINDOWS])
    bucket_of = jnp.asarray(buckets)[None]
    bias = jnp.zeros((n_heads,) + buckets.shape, F32)
    for b in range(N_BUCKETS):
        bias = jnp.where(bucket_of == b, rel_bias[b].astype(F32)[:, None, None, None], bias)
    bias = jnp.where(jnp.asarray(band)[None, None], bias, NEG_INF)

    halo_blocks = t // ATT_HALO
    per_tile = ATT_TILE // ATT_HALO
    main = pl.BlockSpec((ATT_TILE, LANES), lambda i, j: (i, j))
    prev = pl.BlockSpec((ATT_HALO, LANES), lambda i, j: (jnp.maximum(i * per_tile - 1, 0), j))
    nxt = pl.BlockSpec((ATT_HALO, LANES), lambda i, j: (jnp.minimum((i + 1) * per_tile, halo_blocks - 1), j))
    n_q = len(WINDOWS) * ATT_TILE
    return pl.pallas_call(
        functools.partial(_attn_kernel, tp // ATT_TILE, seq_p // ATT_TILE, seq_s // ATT_TILE),
        grid=(t // ATT_TILE, attn_w // LANES),
        in_specs=[main, main, prev, nxt, main, prev, nxt,
                  pl.BlockSpec((LANES // HEAD_DIM, len(WINDOWS), ATT_QB, 2 * ATT_QB), lambda i, j: (j, 0, 0, 0))],
        out_specs=main,
        out_shape=jax.ShapeDtypeStruct((t, attn_w), F32),
        scratch_shapes=[pltpu.VMEM((n_q, LANES), BF16), pltpu.VMEM((krows, LANES), BF16),
                        pltpu.VMEM((krows, LANES), BF16), pltpu.VMEM((n_q, LANES), F32),
                        pltpu.VMEM((n_q, LANES), F32)],
        compiler_params=pltpu.CompilerParams(dimension_semantics=("arbitrary", "arbitrary"),
                                             vmem_limit_bytes=VMEM_LIMIT),
        name="attn",
    )(q, k, k, k, v, v, v, bias)


def _outproj_kernel(n1, sgu_w, n_exp, a_ref, attn_ref, xp_ref, xs_ref, onb_ref, wo_ref, norm2_ref, wrh_ref, wrl_ref,
                    br_ref, tri_ref, x1_ref, h2_ref, meta_ref, cnt_ref, carry):
    i = pl.program_id(0)

    @pl.when(i == 0)
    def _():
        carry[...] = jnp.zeros_like(carry)

    x = jnp.where(i < n1, xp_ref[...], xs_ref[...])
    bn = _rms(attn_ref[...], onb_ref[...]).astype(BF16)
    x1 = x + _dot(a_ref[...], wo_ref[0:sgu_w, :]) + _dot(bn, wo_ref[sgu_w:, :])
    x1_ref[...] = x1
    h2 = _rms(x1, norm2_ref[...])
    h2_ref[...] = h2.astype(BF16)
    hi, lo = _split_bf16(h2)
    logits = _dot(hi, wrh_ref[...]) + _dot(lo, wrh_ref[...]) + _dot(hi, wrl_ref[...]) + br_ref[...]

    lane = lax.broadcasted_iota(jnp.int32, logits.shape, 1)
    work = jnp.where(lane < n_exp, logits, -jnp.inf)
    chosen = jnp.zeros(logits.shape, jnp.bool_)
    experts, values = [], []
    for _ in range(TOP_K):
        top = jnp.max(work, axis=-1, keepdims=True)
        idx = jnp.min(jnp.where(work == top, lane, LANES), axis=-1, keepdims=True)
        hit = lane == idx
        chosen = jnp.logical_or(chosen, hit)
        work = jnp.where(hit, -jnp.inf, work)
        experts.append(idx)
        values.append(top)
    exps = [jnp.exp(v - values[0]) for v in values]
    den = exps[0] + exps[1] + exps[2] + exps[3]

    onehot = jnp.where(chosen, 1.0, 0.0)
    before = _dot(tri_ref[...], onehot.astype(BF16)) + carry[0:1, :]
    carry[0:1, :] = carry[0:1, :] + jnp.sum(onehot, axis=0, keepdims=True)
    cnt_ref[...] = carry[...]

    meta = jnp.zeros(logits.shape, F32)
    for kk in range(TOP_K):
        rank = jnp.sum(jnp.where(lane == experts[kk], before, 0.0), axis=-1, keepdims=True)
        meta = jnp.where(lane == kk, experts[kk].astype(F32), meta)
        meta = jnp.where(lane == TOP_K + kk, exps[kk] / den, meta)
        meta = jnp.where(lane == 2 * TOP_K + kk, rank, meta)
    meta_ref[...] = meta


def _outproj(a_n, attn, xp, xs, out_norm_b, w_o, norm2, w_router, b_router):
    t, sgu_w = a_n.shape
    attn_w = attn.shape[1]
    d_model = w_o.shape[1]
    n_exp = w_router.shape[1]
    assert n_exp <= LANES
    n1 = xp.shape[0] // TOK_TILE
    wr = jnp.pad(w_router, ((0, 0), (0, LANES - n_exp)))
    wr_hi = wr.astype(BF16)
    wr_lo = (wr - wr_hi.astype(F32)).astype(BF16)
    br = jnp.pad(b_router, (0, LANES - n_exp))[None]
    tri = np.tril(np.ones((TOK_TILE, TOK_TILE), np.float32), -1)

    const = lambda shape: pl.BlockSpec(shape, lambda i: (0,) * len(shape))
    tok = lambda w: pl.BlockSpec((TOK_TILE, w), lambda i: (i, 0))
    return pl.pallas_call(
        functools.partial(_outproj_kernel, n1, sgu_w, n_exp),
        grid=(t // TOK_TILE,),
        in_specs=[
            tok(sgu_w), tok(attn_w),
            pl.BlockSpec((TOK_TILE, d_model), lambda i: (jnp.minimum(i, n1 - 1), 0)),
            pl.BlockSpec((TOK_TILE, d_model), lambda i: (jnp.maximum(i - n1, 0), 0)),
            const((1, attn_w)), const(w_o.shape), const((1, d_model)), const((d_model, LANES)),
            const((d_model, LANES)), const((1, LANES)), const((TOK_TILE, TOK_TILE)),
        ],
        out_specs=[tok(d_model), tok(d_model), tok(LANES), const((8, LANES))],
        out_shape=[jax.ShapeDtypeStruct((t, d_model), F32), jax.ShapeDtypeStruct((t, d_model), BF16),
                   jax.ShapeDtypeStruct((t, LANES), F32), jax.ShapeDtypeStruct((8, LANES), F32)],
        scratch_shapes=[pltpu.VMEM((8, LANES), F32)],
        compiler_params=pltpu.CompilerParams(dimension_semantics=("arbitrary",), vmem_limit_bytes=VMEM_LIMIT),
        name="outproj",
    )(a_n, attn, xp, xs, out_norm_b[None], w_o.astype(BF16), norm2[None], wr_hi, wr_lo, br, jnp.asarray(tri, BF16))


def _moe_kernel(d_exp, blk_ref, exp_ref, lo_ref, hi_ref, x_ref, wgu_ref, bgu_ref, wd_ref, bd_ref, y_ref,
                wgu_s, wd_s, act_s):
    it = pl.program_id(0)
    prev = jnp.maximum(it - 1, 0)
    new_expert = jnp.logical_or(it == 0, exp_ref[it] != exp_ref[prev])
    new_block = jnp.logical_or(it == 0, blk_ref[it] != blk_ref[prev])
    lo, hi = lo_ref[it], hi_ref[it]
    active = hi > lo

    @pl.when(jnp.logical_and(new_expert, active))
    def _():
        wgu_s[...] = wgu_ref[...].astype(BF16)
        wd_s[...] = wd_ref[...].astype(BF16)

    @pl.when(active)
    def _():
        x = x_ref[...]
        step = 512
        for n in range(0, d_exp, step):
            gate = _dot(x, wgu_s[:, n:n + step]) + bgu_ref[:, n:n + step]
            up = _dot(x, wgu_s[:, d_exp + n:d_exp + n + step]) + bgu_ref[:, d_exp + n:d_exp + n + step]
            gate = jnp.minimum(gate, SWIGLU_LIMIT)
            up = jnp.clip(up, -SWIGLU_LIMIT, SWIGLU_LIMIT)
            glu = gate * jax.nn.sigmoid(SWIGLU_ALPHA * gate)
            act_s[:, n:n + step] = ((up + 1.0) * glu).astype(BF16)
        y = (_dot(act_s[...], wd_s[...]) + bd_ref[...]).astype(y_ref.dtype)
        row = lax.broadcasted_iota(jnp.int32, (MOE_ROWS, 1), 0)
        mine = jnp.logical_and(row >= lo, row < hi)

        @pl.when(new_block)
        def _():
            y_ref[...] = jnp.where(mine, y, jnp.zeros_like(y))

        @pl.when(jnp.logical_not(new_block))
        def _():
            y_ref[...] = jnp.where(mine, y, y_ref[...])


def _moe_experts(x_rows, items, w_gu, b_gu, w_down, b_down):
    n_rows, d_model = x_rows.shape
    n_exp, _, two_de = w_gu.shape
    d_exp = two_de // 2
    n_items = items[0].shape[0]
    grid_spec = pltpu.PrefetchScalarGridSpec(
        num_scalar_prefetch=4,
        grid=(n_items,),
        in_specs=[
            pl.BlockSpec((MOE_ROWS, d_model), lambda i, blk, ex, lo, hi: (blk[i], 0)),
            pl.BlockSpec((None, d_model, two_de), lambda i, blk, ex, lo, hi: (ex[i], 0, 0)),
            pl.BlockSpec((None, 1, two_de), lambda i, blk, ex, lo, hi: (ex[i], 0, 0)),
            pl.BlockSpec((None, d_exp, d_model), lambda i, blk, ex, lo, hi: (ex[i], 0, 0)),
            pl.BlockSpec((None, 1, d_model), lambda i, blk, ex, lo, hi: (ex[i], 0, 0)),
        ],
        out_specs=pl.BlockSpec((MOE_ROWS, d_model), lambda i, blk, ex, lo, hi: (blk[i], 0)),
        scratch_shapes=[pltpu.VMEM((d_model, two_de), BF16), pltpu.VMEM((d_exp, d_model), BF16),
                        pltpu.VMEM((MOE_ROWS, d_exp), BF16)],
    )
    return pl.pallas_call(
        functools.partial(_moe_kernel, d_exp),
        grid_spec=grid_spec,
        out_shape=jax.ShapeDtypeStruct((n_rows, d_model), BF16),
        compiler_params=pltpu.CompilerParams(dimension_semantics=("arbitrary",), vmem_limit_bytes=VMEM_LIMIT),
        name="moe",
    )(*items, x_rows, w_gu, b_gu[:, None, :], w_down, b_down[:, None, :])


def _route(meta, counts_f, n_exp):
    t = meta.shape[0]
    n_assign = t * TOP_K
    assert n_assign % MOE_ROWS == 0
    top_e = meta[:, 0:TOP_K].astype(jnp.int32)
    rank = meta[:, 2 * TOP_K:3 * TOP_K].astype(jnp.int32)
    counts = counts_f[0, :n_exp].astype(jnp.int32)
    end = jnp.cumsum(counts)
    start = end - counts
    experts = jnp.arange(n_exp, dtype=jnp.int32)
    start_of = jnp.sum(jnp.where(top_e[..., None] == experts, start, 0), axis=-1)
    pos = start_of + rank

    shift = (n_assign - 1).bit_length()
    assert (n_exp << shift) < 2 ** 31
    keys = (top_e.reshape(-1) << shift) + jnp.arange(n_assign, dtype=jnp.int32)
    sorted_tok = (jnp.sort(keys) & ((1 << shift) - 1)) // TOP_K

    n_blocks = n_assign // MOE_ROWS
    n_items = n_blocks + n_exp
    first_blk = start // MOE_ROWS
    n_it = jnp.where(counts > 0, (end - 1) // MOE_ROWS - first_blk + 1, 0)
    it_end = jnp.cumsum(n_it)
    it_start = it_end - n_it
    i = jnp.arange(n_items, dtype=jnp.int32)
    valid = i < it_end[-1]
    e_i = jnp.sum(jnp.minimum(i, it_end[-1] - 1)[:, None] >= it_end[None, :], axis=1).astype(jnp.int32)
    e_i = jnp.minimum(e_i, n_exp - 1)
    pick = lambda table: jnp.sum(jnp.where(e_i[:, None] == experts[None, :], table[None, :], 0), axis=1)
    blk_i = jnp.where(valid, pick(first_blk) + i - pick(it_start), n_blocks - 1)
    lo_i = jnp.where(valid, jnp.clip(pick(start) - blk_i * MOE_ROWS, 0, MOE_ROWS), 0)
    hi_i = jnp.where(valid, jnp.clip(pick(end) - blk_i * MOE_ROWS, 0, MOE_ROWS), 0)
    items = tuple(a.astype(jnp.int32) for a in (blk_i, e_i, lo_i, hi_i))
    return pos, sorted_tok, items


def _combine_kernel(x1_ref, yg_ref, meta_ref, o_ref):
    acc = x1_ref[...]
    for kk in range(TOP_K):
        gate = meta_ref[:, TOP_K + kk:TOP_K + kk + 1]
        acc = acc + gate * yg_ref[kk].astype(F32)
    o_ref[...] = acc


def _combine(x1, yg, meta, row0, rows):
    d_model = x1.shape[1]
    b0 = row0 // TOK_TILE
    tok = lambda w: pl.BlockSpec((TOK_TILE, w), lambda i: (i + b0, 0))
    return pl.pallas_call(
        _combine_kernel,
        grid=(rows // TOK_TILE,),
        in_specs=[tok(d_model), pl.BlockSpec((TOP_K, TOK_TILE, d_model), lambda i: (0, i + b0, 0)), tok(LANES)],
        out_specs=pl.BlockSpec((TOK_TILE, d_model), lambda i: (i, 0)),
        out_shape=jax.ShapeDtypeStruct((rows, d_model), F32),
        compiler_params=pltpu.CompilerParams(dimension_semantics=("arbitrary",), vmem_limit_bytes=VMEM_LIMIT),
        name="combine",
    )(x1, yg, meta)


def kernel(x_prompt, x_sample, norm1, w_in, q_gain, k_gain, rel_bias, sgu_norm, w_s, b_s, out_norm_a, out_norm_b,
           w_o, norm2, w_router, b_router, w_gu, b_gu, w_down, b_down):
    assert norm1.shape[0] == 1, "single-layer trunk"
    bp, sp, d_model = x_prompt.shape
    bs, ss, _ = x_sample.shape
    xp = x_prompt.reshape(bp * sp, d_model)
    xs = x_sample.reshape(bs * ss, d_model)
    tp, t = bp * sp, bp * sp + bs * ss

    q, k, v, a_n = _inproj(xp, xs, norm1[0], w_in[0], q_gain[0], k_gain[0], sgu_norm[0], w_s[0], b_s[0],
                           out_norm_a[0])
    attn = _attention(q, k, v, rel_bias, tp, sp, ss)
    x1, h2, meta, counts = _outproj(a_n, attn, xp, xs, out_norm_b[0], w_o[0], norm2[0], w_router[0], b_router[0])

    pos, sorted_tok, items = _route(meta, counts, w_router.shape[-1])
    x_rows = h2.at[sorted_tok].get(mode="promise_in_bounds")
    y_rows = _moe_experts(x_rows, items, w_gu[0], b_gu[0], w_down[0], b_down[0])
    yg = y_rows.at[pos.T.reshape(-1)].get(mode="promise_in_bounds").reshape(TOP_K, t, d_model)
    out_p = _combine(x1, yg, meta, 0, tp)
    out_s = _combine(x1, yg, meta, tp, t - tp)
    return (out_p.reshape(bp, sp, d_model), out_s.reshape(bs, ss, d_model))
```

```python
import functools
import math

import numpy as np
import jax
import jax.numpy as jnp
from jax import lax
from jax.experimental import pallas as pl
from jax.experimental.pallas import tpu as pltpu

F32 = jnp.float32
BF16 = jnp.bfloat16

EPS = 1e-6
NEG_INF = -1e30
HEAD_DIM = 64
SGU_CHUNK = 128
WINDOWS = ((128, 1), (512, 4), (2048, 16))
N_BUCKETS = 32
MAX_DISTANCE = 1024
TOP_K = 4
SWIGLU_LIMIT = 7.0
SWIGLU_ALPHA = 1.702

LANES = 128
TOK_TILE = 512
ATT_TILE = 2048
ATT_HALO = 1024
ATT_QB = 128
ATT_SIDE = 64
MOE_ROWS = 512
VMEM_LIMIT = 56 * 1024 * 1024


def _dot(a, b):
    return jnp.dot(a, b, preferred_element_type=F32)


def _dot_nt(a, b):
    return lax.dot_general(a, b, (((1,), (1,)), ((), ())), preferred_element_type=F32)


def _rms(x, g):
    ms = jnp.mean(x * x, axis=-1, keepdims=True)
    return x * lax.rsqrt(ms + EPS) * g


def _gelu(x):
    return 0.5 * x * (1.0 + lax.erf(x * (1.0 / math.sqrt(2.0))))


def _split_bf16(x):
    hi = x.astype(BF16)
    lo = (x - hi.astype(F32)).astype(BF16)
    return hi, lo


def _inproj_kernel(n1, attn_w, sgu_w, xp_ref, xs_ref, norm1_ref, win_ref, gsum_ref, gexp_ref,
                   qkg_ref, sgun_ref, wsp_ref, bsb_ref, ona_ref, q_ref, k_ref, v_ref, a_ref):
    i = pl.program_id(0)
    x = jnp.where(i < n1, xp_ref[...], xs_ref[...])
    h = _rms(x, norm1_ref[...]).astype(BF16)

    zqk = _dot(h, win_ref[:, 0:2 * attn_w])
    sq_hi, sq_lo = _split_bf16(zqk * zqk)
    ss = _dot(sq_hi, gsum_ref[...]) + _dot(sq_lo, gsum_ref[...])
    inv = lax.rsqrt(ss * (1.0 / HEAD_DIM) + EPS)
    inv_hi, inv_lo = _split_bf16(inv)
    invb = _dot(inv_hi, gexp_ref[...]) + _dot(inv_lo, gexp_ref[...])
    qk = zqk * invb * qkg_ref[...]
    q_ref[...] = qk[:, 0:attn_w]
    k_ref[...] = qk[:, attn_w:2 * attn_w]
    v_ref[...] = _dot(h, win_ref[:, 2 * attn_w:3 * attn_w])

    c0 = 3 * attn_w
    u = _gelu(_dot(h, win_ref[:, c0:c0 + sgu_w]))
    gv = _gelu(_dot(h, win_ref[:, c0 + sgu_w:c0 + 2 * sgu_w]))
    vsn = _rms(gv, sgun_ref[...]).astype(BF16)
    lane = lax.broadcasted_iota(jnp.int32, (SGU_CHUNK, LANES), 1)
    lo_half = lane < HEAD_DIM
    zero = jnp.zeros((SGU_CHUNK, LANES), BF16)
    for c in range(TOK_TILE // SGU_CHUNK):
        r0 = c * SGU_CHUNK
        parts = []
        for j in range(sgu_w // LANES):
            blk = vsn[r0:r0 + SGU_CHUNK, j * LANES:(j + 1) * LANES]
            rhs = jnp.concatenate([jnp.where(lo_half, blk, zero), jnp.where(lo_half, zero, blk)], axis=0)
            parts.append(_dot(wsp_ref[j], rhs))
        s = jnp.concatenate(parts, axis=1) + bsb_ref[...]
        a = u[r0:r0 + SGU_CHUNK, :] * s
        a_ref[r0:r0 + SGU_CHUNK, :] = _rms(a, ona_ref[...]).astype(BF16)


def _inproj(xp, xs, norm1, w_in, q_gain, k_gain, sgu_norm, w_s, b_s, out_norm_a):
    tp, d_model = xp.shape
    ts = xs.shape[0]
    t = tp + ts
    n_heads_w = q_gain.shape[0]
    sgu_w = sgu_norm.shape[0]
    attn_w = (w_in.shape[1] - 2 * sgu_w) // 3
    n_heads = attn_w // n_heads_w
    n_groups = w_s.shape[0]
    assert n_heads_w == HEAD_DIM and sgu_w // n_groups == HEAD_DIM and w_s.shape[1] == SGU_CHUNK
    assert tp % TOK_TILE == 0 and ts % TOK_TILE == 0 and 2 * n_heads <= LANES
    n1 = tp // TOK_TILE

    heads = np.arange(2 * attn_w) // HEAD_DIM
    gsum = (heads[:, None] == np.arange(LANES)[None, :]).astype(np.float32)
    gexp = gsum.T
    qkg = jnp.concatenate([jnp.tile(q_gain, n_heads) * (HEAD_DIM ** -0.5), jnp.tile(k_gain, n_heads)])
    wsp = jnp.concatenate([w_s[0::2], w_s[1::2]], axis=2).astype(BF16)
    bsb = jnp.repeat(b_s.T, HEAD_DIM, axis=1)

    const = lambda shape: pl.BlockSpec(shape, lambda i: (0,) * len(shape))
    tok = lambda w: pl.BlockSpec((TOK_TILE, w), lambda i: (i, 0))
    return pl.pallas_call(
        functools.partial(_inproj_kernel, n1, attn_w, sgu_w),
        grid=(t // TOK_TILE,),
        in_specs=[
            pl.BlockSpec((TOK_TILE, d_model), lambda i: (jnp.minimum(i, n1 - 1), 0)),
            pl.BlockSpec((TOK_TILE, d_model), lambda i: (jnp.maximum(i - n1, 0), 0)),
            const((1, d_model)), const(w_in.shape), const((2 * attn_w, LANES)), const((LANES, 2 * attn_w)),
            const((1, 2 * attn_w)), const((1, sgu_w)), const(wsp.shape), const((SGU_CHUNK, sgu_w)),
            const((1, sgu_w)),
        ],
        out_specs=[tok(attn_w), tok(attn_w), tok(attn_w), tok(sgu_w)],
        out_shape=[jax.ShapeDtypeStruct((t, attn_w), F32)] * 3 + [jax.ShapeDtypeStruct((t, sgu_w), BF16)],
        compiler_params=pltpu.CompilerParams(dimension_semantics=("arbitrary",), vmem_limit_bytes=VMEM_LIMIT),
        name="inproj",
    )(xp, xs, norm1[None], w_in.astype(BF16), jnp.asarray(gsum, BF16), jnp.asarray(gexp, BF16),
      qkg[None], sgu_norm[None], wsp, bsb, out_norm_a[None])


def _t5_bucket(rel):
    nb = N_BUCKETS // 2
    bucket = (rel > 0).astype(np.int32) * nb
    n = np.abs(rel)
    max_exact = nb // 2
    large = max_exact + (np.log(np.maximum(n, 1) / max_exact)
                         / np.log(MAX_DISTANCE / max_exact) * (nb - max_exact)).astype(np.int32)
    large = np.minimum(large, nb - 1)
    return (bucket + np.where(n < max_exact, n, large)).astype(np.int32)


def _branch_layout():
    out, kbase = [], 0
    for bi, (window, dil) in enumerate(WINDOWS):
        assert window // (2 * dil) == ATT_SIDE
        m = ATT_TILE // dil
        seg = m + 2 * ATT_SIDE
        out.append((dil, m, seg, kbase, bi * ATT_TILE))
        kbase += dil * seg
    return out, kbase


def _strided(ref, start, size, stride):
    if stride == 1:
        return ref[start:start + size, :]
    return ref[pl.ds(start, size, stride=stride), :]


def _attn_kernel(tiles_p, per_seq_p, per_seq_s, q_ref, km_ref, kp_ref, kn_ref, vm_ref, vp_ref, vn_ref,
                 bias_ref, o_ref, qs, ks, vs, o_scr, l_scr, s_scr, p_scr):
    i = pl.program_id(0)
    w = jnp.where(i < tiles_p, i % per_seq_p, (i - tiles_p) % per_seq_s)
    last = jnp.where(i < tiles_p, per_seq_p - 1, per_seq_s - 1)
    left_ok = w > 0
    right_ok = w < last
    layout, _ = _branch_layout()

    for dil, m, seg, kbase, qbase in layout:
        for r in range(dil):
            o = kbase + r * seg
            for dst, main, prev, nxt in ((ks, km_ref, kp_ref, kn_ref), (vs, vm_ref, vp_ref, vn_ref)):
                dst[o:o + ATT_SIDE, :] = _strided(prev, ATT_HALO - ATT_SIDE * dil + r, ATT_SIDE, dil).astype(BF16)
                dst[o + ATT_SIDE:o + ATT_SIDE + m, :] = _strided(main, r, m, dil).astype(BF16)
                dst[o + ATT_SIDE + m:o + seg, :] = _strided(nxt, r, ATT_SIDE, dil).astype(BF16)
            qs[qbase + r * m:qbase + (r + 1) * m, :] = _strided(q_ref, r, m, dil).astype(BF16)

    lane = lax.broadcasted_iota(jnp.int32, (ATT_QB, LANES), 1)
    head0 = lane < HEAD_DIM
    head0_half = lax.broadcasted_iota(jnp.int32, (ATT_QB // 2, LANES), 1) < HEAD_DIM
    col = lax.broadcasted_iota(jnp.int32, (1, 2 * ATT_QB), 1)
    for bi, (dil, m, seg, kbase, qbase) in enumerate(layout):
        nblk = m // ATT_QB

        ncell = dil * nblk

        def offsets(idx, nblk=nblk, seg=seg, kbase=kbase, qbase=qbase):
            r = idx // nblk
            blk = idx % nblk
            qoff = pl.multiple_of(qbase + idx * ATT_QB, ATT_QB)
            koff = pl.multiple_of(kbase + r * seg + blk * ATT_QB, ATT_QB)
            return blk, qoff, koff

        def logits_cell(idx, carry, bi=bi, nblk=nblk):
            blk, qoff, koff = offsets(idx)
            left_bad = jnp.logical_and(blk == 0, jnp.logical_not(left_ok))
            right_bad = jnp.logical_and(blk == nblk - 1, jnp.logical_not(right_ok))
            edge = (jnp.where(jnp.logical_and(col < ATT_SIDE, left_bad), NEG_INF, 0.0)
                    + jnp.where(jnp.logical_and(col >= 2 * ATT_QB - ATT_SIDE, right_bad), NEG_INF, 0.0))
            qc = qs[pl.ds(qoff, ATT_QB), :]
            kc = ks[pl.ds(koff, 2 * ATT_QB), :]
            zero = jnp.zeros_like(qc)
            for h, hsel in enumerate((head0, jnp.logical_not(head0))):
                s_scr[2 * idx + h] = _dot_nt(jnp.where(hsel, qc, zero), kc) + bias_ref[h, bi] + edge
            return carry

        def softmax_cell(idx, carry):
            _, qoff, _ = offsets(idx)
            half = ATT_QB // 2
            for part in range(2):
                dens, lses = [], []
                for h in range(2):
                    s = s_scr[2 * idx + h, part * half:(part + 1) * half, :]
                    mx = jnp.max(s, axis=-1, keepdims=True)
                    p = jnp.exp(s - mx)
                    den = jnp.sum(p, axis=-1, keepdims=True)
                    p_scr[2 * idx + h, part * half:(part + 1) * half, :] = p.astype(BF16)
                    dens.append(den)
                    lses.append(mx + jnp.log(den))
                rows = pl.ds(qoff + part * half, half)
                o_scr[rows, :] = jnp.where(head0_half, dens[0], dens[1])
                l_scr[rows, :] = jnp.where(head0_half, lses[0], lses[1])
            return carry

        def values_cell(idx, carry):
            _, qoff, koff = offsets(idx)
            vc = vs[pl.ds(koff, 2 * ATT_QB), :]
            o0 = _dot(p_scr[2 * idx], vc)
            o1 = _dot(p_scr[2 * idx + 1], vc)
            rows = pl.ds(qoff, ATT_QB)
            o_scr[rows, :] = jnp.where(head0, o0, o1) / o_scr[rows, :]
            return carry

        lax.fori_loop(0, ncell, logits_cell, 0, unroll=2)
        lax.fori_loop(0, ncell, softmax_cell, 0)
        lax.fori_loop(0, ncell, values_cell, 0, unroll=2)

    big = WINDOWS[-1][1]
    rows = ATT_TILE // big
    for r in range(big):
        os_, ls_ = [], []
        for dil, m, seg, kbase, qbase in layout:
            start = qbase + (r % dil) * m + r // dil
            os_.append(_strided(o_scr, start, rows, big // dil))
            ls_.append(_strided(l_scr, start, rows, big // dil))
        mx = jnp.maximum(jnp.maximum(ls_[0], ls_[1]), ls_[2])
        ws = [jnp.exp(l - mx) for l in ls_]
        num = ws[0] * os_[0] + ws[1] * os_[1] + ws[2] * os_[2]
        o_ref[pl.ds(r, rows, stride=big), :] = num / (ws[0] + ws[1] + ws[2])


def _attention(q, k, v, rel_bias, tp, seq_p, seq_s):
    t, attn_w = q.shape
    n_heads = attn_w // HEAD_DIM
    assert seq_p % ATT_TILE == 0 and seq_s % ATT_TILE == 0 and ATT_TILE == 2 * ATT_HALO
    assert ATT_TILE // WINDOWS[-1][1] == ATT_QB
    layout, krows = _branch_layout()

    ii = np.arange(ATT_QB)[:, None]
    jj = np.arange(2 * ATT_QB)[None, :]
    rel = jj - ATT_SIDE - ii
    band = np.abs(rel) <= ATT_SIDE
    buckets = np.stack([_t5_bucket(rel * dil) for _, dil in WINDOWS])
    bucket_of = jnp.asarray(buckets)[None]
    bias = jnp.zeros((n_heads,) + buckets.shape, F32)
    for b in range(N_BUCKETS):
        bias = jnp.where(bucket_of == b, rel_bias[b].astype(F32)[:, None, None, None], bias)
    bias = jnp.where(jnp.asarray(band)[None, None], bias, NEG_INF)

    halo_blocks = t // ATT_HALO
    per_tile = ATT_TILE // ATT_HALO
    main = pl.BlockSpec((ATT_TILE, LANES), lambda i, j: (i, j))
    prev = pl.BlockSpec((ATT_HALO, LANES), lambda i, j: (jnp.maximum(i * per_tile - 1, 0), j))
    nxt = pl.BlockSpec((ATT_HALO, LANES), lambda i, j: (jnp.minimum((i + 1) * per_tile, halo_blocks - 1), j))
    n_q = len(WINDOWS) * ATT_TILE
    return pl.pallas_call(
        functools.partial(_attn_kernel, tp // ATT_TILE, seq_p // ATT_TILE, seq_s // ATT_TILE),
        grid=(t // ATT_TILE, attn_w // LANES),
        in_specs=[main, main, prev, nxt, main, prev, nxt,
                  pl.BlockSpec((LANES // HEAD_DIM, len(WINDOWS), ATT_QB, 2 * ATT_QB), lambda i, j: (j, 0, 0, 0))],
        out_specs=main,
        out_shape=jax.ShapeDtypeStruct((t, attn_w), F32),
        scratch_shapes=[pltpu.VMEM((n_q, LANES), BF16), pltpu.VMEM((krows, LANES), BF16),
                        pltpu.VMEM((krows, LANES), BF16), pltpu.VMEM((n_q, LANES), F32),
                        pltpu.VMEM((n_q, LANES), F32),
                        pltpu.VMEM((2 * ATT_TILE // ATT_QB, ATT_QB, 2 * ATT_QB), F32),
                        pltpu.VMEM((2 * ATT_TILE // ATT_QB, ATT_QB, 2 * ATT_QB), BF16)],
        compiler_params=pltpu.CompilerParams(dimension_semantics=("arbitrary", "arbitrary"),
                                             vmem_limit_bytes=VMEM_LIMIT),
        name="attn",
    )(q, k, k, k, v, v, v, bias)


def _outproj_kernel(n1, sgu_w, n_exp, a_ref, attn_ref, xp_ref, xs_ref, onb_ref, wo_ref, norm2_ref, wrh_ref, wrl_ref,
                    br_ref, tri_ref, x1_ref, h2_ref, meta_ref, cnt_ref, carry):
    i = pl.program_id(0)

    @pl.when(i == 0)
    def _():
        carry[...] = jnp.zeros_like(carry)

    x = jnp.where(i < n1, xp_ref[...], xs_ref[...])
    bn = _rms(attn_ref[...], onb_ref[...]).astype(BF16)
    x1 = x + _dot(a_ref[...], wo_ref[0:sgu_w, :]) + _dot(bn, wo_ref[sgu_w:, :])
    x1_ref[...] = x1
    h2 = _rms(x1, norm2_ref[...])
    h2_ref[...] = h2.astype(BF16)
    hi, lo = _split_bf16(h2)
    logits = _dot(hi, wrh_ref[...]) + _dot(lo, wrh_ref[...]) + _dot(hi, wrl_ref[...]) + br_ref[...]

    lane = lax.broadcasted_iota(jnp.int32, logits.shape, 1)
    work = jnp.where(lane < n_exp, logits, -jnp.inf)
    chosen = jnp.zeros(logits.shape, jnp.bool_)
    experts, values = [], []
    for _ in range(TOP_K):
        top = jnp.max(work, axis=-1, keepdims=True)
        idx = jnp.min(jnp.where(work == top, lane, LANES), axis=-1, keepdims=True)
        hit = lane == idx
        chosen = jnp.logical_or(chosen, hit)
        work = jnp.where(hit, -jnp.inf, work)
        experts.append(idx)
        values.append(top)
    exps = [jnp.exp(v - values[0]) for v in values]
    den = exps[0] + exps[1] + exps[2] + exps[3]

    onehot = jnp.where(chosen, 1.0, 0.0)
    before = _dot(tri_ref[...], onehot.astype(BF16)) + carry[0:1, :]
    carry[0:1, :] = carry[0:1, :] + jnp.sum(onehot, axis=0, keepdims=True)
    cnt_ref[...] = carry[...]

    meta = jnp.zeros(logits.shape, F32)
    for kk in range(TOP_K):
        rank = jnp.sum(jnp.where(lane == experts[kk], before, 0.0), axis=-1, keepdims=True)
        meta = jnp.where(lane == kk, experts[kk].astype(F32), meta)
        meta = jnp.where(lane == TOP_K + kk, exps[kk] / den, meta)
        meta = jnp.where(lane == 2 * TOP_K + kk, rank, meta)
    meta_ref[...] = meta


def _outproj(a_n, attn, xp, xs, out_norm_b, w_o, norm2, w_router, b_router):
    t, sgu_w = a_n.shape
    attn_w = attn.shape[1]
    d_model = w_o.shape[1]
    n_exp = w_router.shape[1]
    assert n_exp <= LANES
    n1 = xp.shape[0] // TOK_TILE
    wr = jnp.pad(w_router, ((0, 0), (0, LANES - n_exp)))
    wr_hi = wr.astype(BF16)
    wr_lo = (wr - wr_hi.astype(F32)).astype(BF16)
    br = jnp.pad(b_router, (0, LANES - n_exp))[None]
    tri = np.tril(np.ones((TOK_TILE, TOK_TILE), np.float32), -1)

    const = lambda shape: pl.BlockSpec(shape, lambda i: (0,) * len(shape))
    tok = lambda w: pl.BlockSpec((TOK_TILE, w), lambda i: (i, 0))
    return pl.pallas_call(
        functools.partial(_outproj_kernel, n1, sgu_w, n_exp),
        grid=(t // TOK_TILE,),
        in_specs=[
            tok(sgu_w), tok(attn_w),
            pl.BlockSpec((TOK_TILE, d_model), lambda i: (jnp.minimum(i, n1 - 1), 0)),
            pl.BlockSpec((TOK_TILE, d_model), lambda i: (jnp.maximum(i - n1, 0), 0)),
            const((1, attn_w)), const(w_o.shape), const((1, d_model)), const((d_model, LANES)),
            const((d_model, LANES)), const((1, LANES)), const((TOK_TILE, TOK_TILE)),
        ],
        out_specs=[tok(d_model), tok(d_model), tok(LANES), const((8, LANES))],
        out_shape=[jax.ShapeDtypeStruct((t, d_model), F32), jax.ShapeDtypeStruct((t, d_model), BF16),
                   jax.ShapeDtypeStruct((t, LANES), F32), jax.ShapeDtypeStruct((8, LANES), F32)],
        scratch_shapes=[pltpu.VMEM((8, LANES), F32)],
        compiler_params=pltpu.CompilerParams(dimension_semantics=("arbitrary",), vmem_limit_bytes=VMEM_LIMIT),
        name="outproj",
    )(a_n, attn, xp, xs, out_norm_b[None], w_o.astype(BF16), norm2[None], wr_hi, wr_lo, br, jnp.asarray(tri, BF16))


def _moe_kernel(d_exp, blk_ref, exp_ref, lo_ref, hi_ref, x_ref, wgu_ref, bgu_ref, wd_ref, bd_ref, y_ref,
                wgu_s, wd_s, act_s):
    it = pl.program_id(0)
    prev = jnp.maximum(it - 1, 0)
    new_expert = jnp.logical_or(it == 0, exp_ref[it] != exp_ref[prev])
    new_block = jnp.logical_or(it == 0, blk_ref[it] != blk_ref[prev])
    lo, hi = lo_ref[it], hi_ref[it]
    active = hi > lo

    @pl.when(jnp.logical_and(new_expert, active))
    def _():
        wgu_s[...] = wgu_ref[...].astype(BF16)
        wd_s[...] = wd_ref[...].astype(BF16)

    @pl.when(active)
    def _():
        x = x_ref[...]
        step = 512
        for n in range(0, d_exp, step):
            gate = _dot(x, wgu_s[:, n:n + step]) + bgu_ref[:, n:n + step]
            up = _dot(x, wgu_s[:, d_exp + n:d_exp + n + step]) + bgu_ref[:, d_exp + n:d_exp + n + step]
            gate = jnp.minimum(gate, SWIGLU_LIMIT)
            up = jnp.clip(up, -SWIGLU_LIMIT, SWIGLU_LIMIT)
            glu = gate * jax.nn.sigmoid(SWIGLU_ALPHA * gate)
            act_s[:, n:n + step] = ((up + 1.0) * glu).astype(BF16)
        y = (_dot(act_s[...], wd_s[...]) + bd_ref[...]).astype(y_ref.dtype)
        row = lax.broadcasted_iota(jnp.int32, (MOE_ROWS, 1), 0)
        mine = jnp.logical_and(row >= lo, row < hi)

        @pl.when(new_block)
        def _():
            y_ref[...] = jnp.where(mine, y, jnp.zeros_like(y))

        @pl.when(jnp.logical_not(new_block))
        def _():
            y_ref[...] = jnp.where(mine, y, y_ref[...])


def _moe_experts(x_rows, items, w_gu, b_gu, w_down, b_down):
    n_rows, d_model = x_rows.shape
    n_exp, _, two_de = w_gu.shape
    d_exp = two_de // 2
    n_items = items[0].shape[0]
    grid_spec = pltpu.PrefetchScalarGridSpec(
        num_scalar_prefetch=4,
        grid=(n_items,),
        in_specs=[
            pl.BlockSpec((MOE_ROWS, d_model), lambda i, blk, ex, lo, hi: (blk[i], 0)),
            pl.BlockSpec((None, d_model, two_de), lambda i, blk, ex, lo, hi: (ex[i], 0, 0)),
            pl.BlockSpec((None, 1, two_de), lambda i, blk, ex, lo, hi: (ex[i], 0, 0)),
            pl.BlockSpec((None, d_exp, d_model), lambda i, blk, ex, lo, hi: (ex[i], 0, 0)),
            pl.BlockSpec((None, 1, d_model), lambda i, blk, ex, lo, hi: (ex[i], 0, 0)),
        ],
        out_specs=pl.BlockSpec((MOE_ROWS, d_model), lambda i, blk, ex, lo, hi: (blk[i], 0)),
        scratch_shapes=[pltpu.VMEM((d_model, two_de), BF16), pltpu.VMEM((d_exp, d_model), BF16),
                        pltpu.VMEM((MOE_ROWS, d_exp), BF16)],
    )
    return pl.pallas_call(
        functools.partial(_moe_kernel, d_exp),
        grid_spec=grid_spec,
        out_shape=jax.ShapeDtypeStruct((n_rows, d_model), BF16),
        compiler_params=pltpu.CompilerParams(dimension_semantics=("arbitrary",), vmem_limit_bytes=VMEM_LIMIT),
        name="moe",
    )(*items, x_rows, w_gu, b_gu[:, None, :], w_down, b_down[:, None, :])


def _route(meta, counts_f, n_exp):
    t = meta.shape[0]
    n_assign = t * TOP_K
    assert n_assign % MOE_ROWS == 0
    top_e = meta[:, 0:TOP_K].astype(jnp.int32)
    rank = meta[:, 2 * TOP_K:3 * TOP_K].astype(jnp.int32)
    counts = counts_f[0, :n_exp].astype(jnp.int32)
    end = jnp.cumsum(counts)
    start = end - counts
    experts = jnp.arange(n_exp, dtype=jnp.int32)
    start_of = jnp.sum(jnp.where(top_e[..., None] == experts, start, 0), axis=-1)
    pos = start_of + rank

    shift = (n_assign - 1).bit_length()
    assert (n_exp << shift) < 2 ** 31
    keys = (top_e.reshape(-1) << shift) + jnp.arange(n_assign, dtype=jnp.int32)
    sorted_tok = (jnp.sort(keys) & ((1 << shift) - 1)) // TOP_K

    n_blocks = n_assign // MOE_ROWS
    n_items = n_blocks + n_exp
    first_blk = start // MOE_ROWS
    n_it = jnp.where(counts > 0, (end - 1) // MOE_ROWS - first_blk + 1, 0)
    it_end = jnp.cumsum(n_it)
    it_start = it_end - n_it
    i = jnp.arange(n_items, dtype=jnp.int32)
    valid = i < it_end[-1]
    e_i = jnp.sum(jnp.minimum(i, it_end[-1] - 1)[:, None] >= it_end[None, :], axis=1).astype(jnp.int32)
    e_i = jnp.minimum(e_i, n_exp - 1)
    pick = lambda table: jnp.sum(jnp.where(e_i[:, None] == experts[None, :], table[None, :], 0), axis=1)
    blk_i = jnp.where(valid, pick(first_blk) + i - pick(it_start), n_blocks - 1)
    lo_i = jnp.where(valid, jnp.clip(pick(start) - blk_i * MOE_ROWS, 0, MOE_ROWS), 0)
    hi_i = jnp.where(valid, jnp.clip(pick(end) - blk_i * MOE_ROWS, 0, MOE_ROWS), 0)
    items = tuple(a.astype(jnp.int32) for a in (blk_i, e_i, lo_i, hi_i))
    return pos, sorted_tok, items


def _combine_kernel(x1_ref, yg_ref, meta_ref, o_ref):
    acc = x1_ref[...]
    for kk in range(TOP_K):
        gate = meta_ref[:, TOP_K + kk:TOP_K + kk + 1]
        acc = acc + gate * yg_ref[kk].astype(F32)
    o_ref[...] = acc


def _combine(x1, yg, meta, row0, rows):
    d_model = x1.shape[1]
    b0 = row0 // TOK_TILE
    tok = lambda w: pl.BlockSpec((TOK_TILE, w), lambda i: (i + b0, 0))
    return pl.pallas_call(
        _combine_kernel,
        grid=(rows // TOK_TILE,),
        in_specs=[tok(d_model), pl.BlockSpec((TOP_K, TOK_TILE, d_model), lambda i: (0, i + b0, 0)), tok(LANES)],
        out_specs=pl.BlockSpec((TOK_TILE, d_model), lambda i: (i, 0)),
        out_shape=jax.ShapeDtypeStruct((rows, d_model), F32),
        compiler_params=pltpu.CompilerParams(dimension_semantics=("arbitrary",), vmem_limit_bytes=VMEM_LIMIT),
        name="combine",
    )(x1, yg, meta)


def kernel(x_prompt, x_sample, norm1, w_in, q_gain, k_gain, rel_bias, sgu_norm, w_s, b_s, out_norm_a, out_norm_b,
           w_o, norm2, w_router, b_router, w_gu, b_gu, w_down, b_down):
    assert norm1.shape[0] == 1, "single-layer trunk"
    bp, sp, d_model = x_prompt.shape
    bs, ss, _ = x_sample.shape
    xp = x_prompt.reshape(bp * sp, d_model)
    xs = x_sample.reshape(bs * ss, d_model)
    tp, t = bp * sp, bp * sp + bs * ss

    q, k, v, a_n = _inproj(xp, xs, norm1[0], w_in[0], q_gain[0], k_gain[0], sgu_norm[0], w_s[0], b_s[0],
                           out_norm_a[0])
    attn = _attention(q, k, v, rel_bias, tp, sp, ss)
    x1, h2, meta, counts = _outproj(a_n, attn, xp, xs, out_norm_b[0], w_o[0], norm2[0], w_router[0], b_router[0])

    pos, sorted_tok, items = _route(meta, counts, w_router.shape[-1])
    x_rows = h2.at[sorted_tok].get(mode="promise_in_bounds")
    y_rows = _moe_experts(x_rows, items, w_gu[0], b_gu[0], w_down[0], b_down[0])
    yg = y_rows.at[pos.T.reshape(-1)].get(mode="promise_in_bounds").reshape(TOP_K, t, d_model)
    out_p = _combine(x1, yg, meta, 0, tp)
    out_s = _combine(x1, yg, meta, tp, t - tp)
    return (out_p.reshape(bp, sp, d_model), out_s.reshape(bs, ss, d_model))
```

```python
import functools
import math

import numpy as np
import jax
import jax.numpy as jnp
from jax import lax
from jax.experimental import pallas as pl
from jax.experimental.pallas import tpu as pltpu

F32 = jnp.float32
BF16 = jnp.bfloat16

EPS = 1e-6
NEG_INF = -1e30
HEAD_DIM = 64
SGU_CHUNK = 128
WINDOWS = ((128, 1), (512, 4), (2048, 16))
N_BUCKETS = 32
MAX_DISTANCE = 1024
TOP_K = 4
SWIGLU_LIMIT = 7.0
SWIGLU_ALPHA = 1.702

LANES = 128
TOK_TILE = 512
ATT_TILE = 2048
ATT_HALO = 1024
ATT_QB = 128
ATT_SIDE = 64
ATT_PIPE = 8
MOE_ROWS = 512
VMEM_LIMIT = 56 * 1024 * 1024


def _dot(a, b):
    return jnp.dot(a, b, preferred_element_type=F32)


def _dot_nt(a, b):
    return lax.dot_general(a, b, (((1,), (1,)), ((), ())), preferred_element_type=F32)


def _rms(x, g):
    ms = jnp.mean(x * x, axis=-1, keepdims=True)
    return x * lax.rsqrt(ms + EPS) * g


def _gelu(x):
    return 0.5 * x * (1.0 + lax.erf(x * (1.0 / math.sqrt(2.0))))


def _split_bf16(x):
    hi = x.astype(BF16)
    lo = (x - hi.astype(F32)).astype(BF16)
    return hi, lo


def _inproj_kernel(n1, attn_w, sgu_w, xp_ref, xs_ref, norm1_ref, win_ref, gsum_ref, gexp_ref,
                   qkg_ref, sgun_ref, wsp_ref, bsb_ref, ona_ref, q_ref, k_ref, v_ref, a_ref):
    i = pl.program_id(0)
    x = jnp.where(i < n1, xp_ref[...], xs_ref[...])
    h = _rms(x, norm1_ref[...]).astype(BF16)

    zqk = _dot(h, win_ref[:, 0:2 * attn_w])
    sq_hi, sq_lo = _split_bf16(zqk * zqk)
    ss = _dot(sq_hi, gsum_ref[...]) + _dot(sq_lo, gsum_ref[...])
    inv = lax.rsqrt(ss * (1.0 / HEAD_DIM) + EPS)
    inv_hi, inv_lo = _split_bf16(inv)
    invb = _dot(inv_hi, gexp_ref[...]) + _dot(inv_lo, gexp_ref[...])
    qk = zqk * invb * qkg_ref[...]
    q_ref[...] = qk[:, 0:attn_w]
    k_ref[...] = qk[:, attn_w:2 * attn_w]
    v_ref[...] = _dot(h, win_ref[:, 2 * attn_w:3 * attn_w])

    c0 = 3 * attn_w
    u = _gelu(_dot(h, win_ref[:, c0:c0 + sgu_w]))
    gv = _gelu(_dot(h, win_ref[:, c0 + sgu_w:c0 + 2 * sgu_w]))
    vsn = _rms(gv, sgun_ref[...]).astype(BF16)
    lane = lax.broadcasted_iota(jnp.int32, (SGU_CHUNK, LANES), 1)
    lo_half = lane < HEAD_DIM
    zero = jnp.zeros((SGU_CHUNK, LANES), BF16)
    for c in range(TOK_TILE // SGU_CHUNK):
        r0 = c * SGU_CHUNK
        parts = []
        for j in range(sgu_w // LANES):
            blk = vsn[r0:r0 + SGU_CHUNK, j * LANES:(j + 1) * LANES]
            rhs = jnp.concatenate([jnp.where(lo_half, blk, zero), jnp.where(lo_half, zero, blk)], axis=0)
            parts.append(_dot(wsp_ref[j], rhs))
        s = jnp.concatenate(parts, axis=1) + bsb_ref[...]
        a = u[r0:r0 + SGU_CHUNK, :] * s
        a_ref[r0:r0 + SGU_CHUNK, :] = _rms(a, ona_ref[...]).astype(BF16)


def _inproj(xp, xs, norm1, w_in, q_gain, k_gain, sgu_norm, w_s, b_s, out_norm_a):
    tp, d_model = xp.shape
    ts = xs.shape[0]
    t = tp + ts
    n_heads_w = q_gain.shape[0]
    sgu_w = sgu_norm.shape[0]
    attn_w = (w_in.shape[1] - 2 * sgu_w) // 3
    n_heads = attn_w // n_heads_w
    n_groups = w_s.shape[0]
    assert n_heads_w == HEAD_DIM and sgu_w // n_groups == HEAD_DIM and w_s.shape[1] == SGU_CHUNK
    assert tp % TOK_TILE == 0 and ts % TOK_TILE == 0 and 2 * n_heads <= LANES
    n1 = tp // TOK_TILE

    heads = np.arange(2 * attn_w) // HEAD_DIM
    gsum = (heads[:, None] == np.arange(LANES)[None, :]).astype(np.float32)
    gexp = gsum.T
    qkg = jnp.concatenate([jnp.tile(q_gain, n_heads) * (HEAD_DIM ** -0.5), jnp.tile(k_gain, n_heads)])
    wsp = jnp.concatenate([w_s[0::2], w_s[1::2]], axis=2).astype(BF16)
    bsb = jnp.repeat(b_s.T, HEAD_DIM, axis=1)

    const = lambda shape: pl.BlockSpec(shape, lambda i: (0,) * len(shape))
    tok = lambda w: pl.BlockSpec((TOK_TILE, w), lambda i: (i, 0))
    return pl.pallas_call(
        functools.partial(_inproj_kernel, n1, attn_w, sgu_w),
        grid=(t // TOK_TILE,),
        in_specs=[
            pl.BlockSpec((TOK_TILE, d_model), lambda i: (jnp.minimum(i, n1 - 1), 0)),
            pl.BlockSpec((TOK_TILE, d_model), lambda i: (jnp.maximum(i - n1, 0), 0)),
            const((1, d_model)), const(w_in.shape), const((2 * attn_w, LANES)), const((LANES, 2 * attn_w)),
            const((1, 2 * attn_w)), const((1, sgu_w)), const(wsp.shape), const((SGU_CHUNK, sgu_w)),
            const((1, sgu_w)),
        ],
        out_specs=[tok(attn_w), tok(attn_w), tok(attn_w), tok(sgu_w)],
        out_shape=[jax.ShapeDtypeStruct((t, attn_w), F32)] * 3 + [jax.ShapeDtypeStruct((t, sgu_w), BF16)],
        compiler_params=pltpu.CompilerParams(dimension_semantics=("arbitrary",), vmem_limit_bytes=VMEM_LIMIT),
        name="inproj",
    )(xp, xs, norm1[None], w_in.astype(BF16), jnp.asarray(gsum, BF16), jnp.asarray(gexp, BF16),
      qkg[None], sgu_norm[None], wsp, bsb, out_norm_a[None])


def _t5_bucket(rel):
    nb = N_BUCKETS // 2
    bucket = (rel > 0).astype(np.int32) * nb
    n = np.abs(rel)
    max_exact = nb // 2
    large = max_exact + (np.log(np.maximum(n, 1) / max_exact)
                         / np.log(MAX_DISTANCE / max_exact) * (nb - max_exact)).astype(np.int32)
    large = np.minimum(large, nb - 1)
    return (bucket + np.where(n < max_exact, n, large)).astype(np.int32)


def _branch_layout():
    out, kbase = [], 0
    for bi, (window, dil) in enumerate(WINDOWS):
        assert window // (2 * dil) == ATT_SIDE
        m = ATT_TILE // dil
        seg = m + 2 * ATT_SIDE
        out.append((dil, m, seg, kbase, bi * ATT_TILE))
        kbase += dil * seg
    return out, kbase


def _strided(ref, start, size, stride):
    if stride == 1:
        return ref[start:start + size, :]
    return ref[pl.ds(start, size, stride=stride), :]


def _attn_kernel(tiles_p, per_seq_p, per_seq_s, q_ref, km_ref, kp_ref, kn_ref, vm_ref, vp_ref, vn_ref,
                 bias_ref, o_ref, qs, kts, vs, o_scr, l_scr, s_scr, p_scr, m_scr, k4, v4, q4):
    i = pl.program_id(0)
    w = jnp.where(i < tiles_p, i % per_seq_p, (i - tiles_p) % per_seq_s)
    last = jnp.where(i < tiles_p, per_seq_p - 1, per_seq_s - 1)
    left_ok = w > 0
    right_ok = w < last
    layout, _ = _branch_layout()
    mid = WINDOWS[1][1]
    assert WINDOWS[2][1] == mid * mid
    halo4, main4 = ATT_HALO // mid, ATT_TILE // mid
    win4 = 2 * halo4 + main4

    @pl.when(jnp.logical_and(i == 0, pl.program_id(1) == 0))
    def _():
        vs[:, LANES:] = jnp.ones((vs.shape[0], LANES), BF16)

    for r in range(mid):
        for dst, main, prev, nxt in ((k4, km_ref, kp_ref, kn_ref), (v4, vm_ref, vp_ref, vn_ref)):
            dst[r * win4:r * win4 + halo4, :] = _strided(prev, r, halo4, mid)
            dst[r * win4 + halo4:r * win4 + halo4 + main4, :] = _strided(main, r, main4, mid)
            dst[r * win4 + halo4 + main4:(r + 1) * win4, :] = _strided(nxt, r, halo4, mid)
        q4[r * main4:(r + 1) * main4, :] = _strided(q_ref, r, main4, mid)

    def key_rows(src_k, src_v, start, stride, kbase_rows):
        kc = _strided(src_k, start, ATT_QB, stride)
        kts[kbase_rows // ATT_QB] = kc.T.astype(BF16)
        vs[kbase_rows:kbase_rows + ATT_QB, 0:LANES] = _strided(src_v, start, ATT_QB, stride).astype(BF16)

    for bi, (dil, m, seg, kbase, qbase) in enumerate(layout):
        nchunk = seg // ATT_QB
        for r in range(dil):
            o = kbase + r * seg
            if dil == 1:
                for c in range(nchunk):
                    lo = c * ATT_QB - ATT_SIDE
                    if c == 0:
                        kc = jnp.concatenate([kp_ref[ATT_HALO - ATT_SIDE:, :], km_ref[0:ATT_QB - ATT_SIDE, :]], axis=0)
                        vc = jnp.concatenate([vp_ref[ATT_HALO - ATT_SIDE:, :], vm_ref[0:ATT_QB - ATT_SIDE, :]], axis=0)
                    elif c == nchunk - 1:
                        kc = jnp.concatenate([km_ref[lo:, :], kn_ref[0:ATT_SIDE, :]], axis=0)
                        vc = jnp.concatenate([vm_ref[lo:, :], vn_ref[0:ATT_SIDE, :]], axis=0)
                    else:
                        kc = km_ref[lo:lo + ATT_QB, :]
                        vc = vm_ref[lo:lo + ATT_QB, :]
                    kts[(o + c * ATT_QB) // ATT_QB] = kc.T.astype(BF16)
                    vs[o + c * ATT_QB:o + (c + 1) * ATT_QB, 0:LANES] = vc.astype(BF16)
                qs[qbase:qbase + m, :] = q_ref[...].astype(BF16)
            else:
                r4, s = r % mid, r // mid
                step = dil // mid
                first = r4 * win4 + (halo4 - ATT_SIDE * step) + s
                for c in range(nchunk):
                    key_rows(k4, v4, first + c * ATT_QB * step, step, o + c * ATT_QB)
                qs[qbase + r * m:qbase + (r + 1) * m, :] = _strided(q4, r4 * main4 + s, m, step).astype(BF16)

    lane = lax.broadcasted_iota(jnp.int32, (ATT_QB, LANES), 1)
    head0 = lane < HEAD_DIM
    col = lax.broadcasted_iota(jnp.int32, (1, 2 * ATT_QB), 1)
    cells = ATT_TILE // ATT_QB
    total = len(layout) * cells
    blk_bits = [(m // ATT_QB).bit_length() - 1 for _, m, _, _, _ in layout]
    seg_chunks = [seg // ATT_QB for _, _, seg, _, _ in layout]
    base_chunks = [kbase // ATT_QB for _, _, _, kbase, _ in layout]

    def cell_params(g):
        bi = g // cells
        idx = g % cells
        if isinstance(g, int):
            pick = lambda vals: vals[bi]
        else:
            pick = lambda vals: jnp.where(bi == 0, vals[0], jnp.where(bi == 1, vals[1], vals[2]))
        bits = pick(blk_bits)
        last_blk = (1 << bits) - 1
        blk = idx & last_blk
        return bi, blk, last_blk, pick(base_chunks) + (idx >> bits) * pick(seg_chunks) + blk

    def stage_logits(g, slot):
        bi, blk, last_blk, kchunk = cell_params(g)
        left_bad = jnp.logical_and(blk == 0, jnp.logical_not(left_ok))
        right_bad = jnp.logical_and(blk == last_blk, jnp.logical_not(right_ok))
        edge = (jnp.where(jnp.logical_and(col < ATT_SIDE, left_bad), NEG_INF, 0.0)
                + jnp.where(jnp.logical_and(col >= 2 * ATT_QB - ATT_SIDE, right_bad), NEG_INF, 0.0))
        qc = qs[pl.ds(pl.multiple_of(g * ATT_QB, ATT_QB), ATT_QB), :]
        zero = jnp.zeros_like(qc)
        lhs = jnp.concatenate([jnp.where(head0, qc, zero), jnp.where(head0, zero, qc)], axis=0)
        kt = jnp.concatenate([kts[kchunk], kts[kchunk + 1]], axis=1)
        s_scr[slot] = _dot(lhs, kt) + bias_ref[bi] + edge

    def stage_softmax(slot):
        part = ATT_QB // 2
        for c in range(2 * ATT_QB // part):
            s = s_scr[slot, c * part:(c + 1) * part, :]
            mx = jnp.max(s, axis=-1, keepdims=True)
            p_scr[slot, c * part:(c + 1) * part, :] = jnp.exp(s - mx).astype(BF16)
            m_scr[slot, c * part:(c + 1) * part, :] = jnp.broadcast_to(mx, (part, LANES))

    def stage_values(g, slot):
        _, _, _, kchunk = cell_params(g)
        koff = pl.multiple_of(kchunk * ATT_QB, ATT_QB)
        r = _dot(p_scr[slot], vs[pl.ds(koff, 2 * ATT_QB), :])
        num = jnp.where(head0, r[0:ATT_QB, 0:LANES], r[ATT_QB:, 0:LANES])
        den = jnp.where(head0, r[0:ATT_QB, LANES:], r[ATT_QB:, LANES:])
        mx = jnp.where(head0, m_scr[slot, 0:ATT_QB, :], m_scr[slot, ATT_QB:, :])
        rows = pl.ds(pl.multiple_of(g * ATT_QB, ATT_QB), ATT_QB)
        o_scr[rows, :] = num / den
        l_scr[rows, :] = mx + jnp.log(den)

    width = ATT_PIPE

    def group(c, do_values, do_softmax, do_logits):
        if do_values:
            for u in range(width):
                stage_values(c - width + u, u)
        if do_softmax:
            for u in range(width):
                stage_softmax(u)
        if do_logits:
            for u in range(width):
                stage_logits(c + width + u, u)

    group(-width, False, False, True)
    group(0, False, True, True)

    def trip(t, carry):
        group((t + 1) * width, True, True, True)
        return carry

    lax.fori_loop(0, total // width - 2, trip, 0)
    group(total - width, True, True, False)
    group(total, True, False, False)

    big = WINDOWS[-1][1]
    rows = ATT_TILE // big
    for r in range(big):
        os_, ls_ = [], []
        for dil, m, seg, kbase, qbase in layout:
            start = qbase + (r % dil) * m + r // dil
            os_.append(_strided(o_scr, start, rows, big // dil))
            ls_.append(_strided(l_scr, start, rows, big // dil))
        mx = jnp.maximum(jnp.maximum(ls_[0], ls_[1]), ls_[2])
        ws = [jnp.exp(l - mx) for l in ls_]
        num = ws[0] * os_[0] + ws[1] * os_[1] + ws[2] * os_[2]
        o_ref[pl.ds(r, rows, stride=big), :] = num / (ws[0] + ws[1] + ws[2])


def _attention(q, k, v, rel_bias, tp, seq_p, seq_s):
    t, attn_w = q.shape
    n_heads = attn_w // HEAD_DIM
    pair = LANES // HEAD_DIM
    assert seq_p % ATT_TILE == 0 and seq_s % ATT_TILE == 0 and ATT_TILE == 2 * ATT_HALO
    assert ATT_TILE // WINDOWS[-1][1] == ATT_QB and pair == 2
    layout, krows = _branch_layout()

    ii = np.arange(ATT_QB)[:, None]
    jj = np.arange(2 * ATT_QB)[None, :]
    rel = jj - ATT_SIDE - ii
    band = np.abs(rel) <= ATT_SIDE
    buckets = np.stack([_t5_bucket(rel * dil) for _, dil in WINDOWS])
    bucket_of = jnp.asarray(buckets)[None]
    bias = jnp.zeros((n_heads,) + buckets.shape, F32)
    for b in range(N_BUCKETS):
        bias = jnp.where(bucket_of == b, rel_bias[b].astype(F32)[:, None, None, None], bias)
    bias = jnp.where(jnp.asarray(band)[None, None], bias, NEG_INF)
    nb = len(WINDOWS)
    bias = bias.reshape(n_heads // pair, pair, nb, ATT_QB, 2 * ATT_QB).transpose(0, 2, 1, 3, 4)
    bias = bias.reshape(n_heads // pair, nb, pair * ATT_QB, 2 * ATT_QB)

    halo_blocks = t // ATT_HALO
    per_tile = ATT_TILE // ATT_HALO
    mid = WINDOWS[1][1]
    main = pl.BlockSpec((ATT_TILE, LANES), lambda i, j: (i, j))
    prev = pl.BlockSpec((ATT_HALO, LANES), lambda i, j: (jnp.maximum(i * per_tile - 1, 0), j))
    nxt = pl.BlockSpec((ATT_HALO, LANES), lambda i, j: (jnp.minimum((i + 1) * per_tile, halo_blocks - 1), j))
    n_q = nb * ATT_TILE
    return pl.pallas_call(
        functools.partial(_attn_kernel, tp // ATT_TILE, seq_p // ATT_TILE, seq_s // ATT_TILE),
        grid=(t // ATT_TILE, attn_w // LANES),
        in_specs=[main, main, prev, nxt, main, prev, nxt,
                  pl.BlockSpec((None, nb, pair * ATT_QB, 2 * ATT_QB), lambda i, j: (j, 0, 0, 0))],
        out_specs=main,
        out_shape=jax.ShapeDtypeStruct((t, attn_w), F32),
        scratch_shapes=[pltpu.VMEM((n_q, LANES), BF16),
                        pltpu.VMEM((krows // ATT_QB, LANES, ATT_QB), BF16),
                        pltpu.VMEM((krows, 2 * LANES), BF16),
                        pltpu.VMEM((n_q, LANES), F32), pltpu.VMEM((n_q, LANES), F32),
                        pltpu.VMEM((ATT_PIPE, pair * ATT_QB, 2 * ATT_QB), F32),
                        pltpu.VMEM((ATT_PIPE, pair * ATT_QB, 2 * ATT_QB), BF16),
                        pltpu.VMEM((ATT_PIPE, pair * ATT_QB, LANES), F32),
                        pltpu.VMEM(((ATT_TILE + 2 * ATT_HALO), LANES), F32),
                        pltpu.VMEM(((ATT_TILE + 2 * ATT_HALO), LANES), F32),
                        pltpu.VMEM((ATT_TILE, LANES), F32)],
        compiler_params=pltpu.CompilerParams(dimension_semantics=("arbitrary", "arbitrary"),
                                             vmem_limit_bytes=VMEM_LIMIT),
        name="attn",
    )(q, k, k, k, v, v, v, bias)


def _outproj_kernel(n1, sgu_w, n_exp, a_ref, attn_ref, xp_ref, xs_ref, onb_ref, wo_ref, norm2_ref, wrh_ref, wrl_ref,
                    br_ref, tri_ref, x1_ref, h2_ref, meta_ref, cnt_ref, carry):
    i = pl.program_id(0)

    @pl.when(i == 0)
    def _():
        carry[...] = jnp.zeros_like(carry)

    x = jnp.where(i < n1, xp_ref[...], xs_ref[...])
    bn = _rms(attn_ref[...], onb_ref[...]).astype(BF16)
    x1 = x + _dot(a_ref[...], wo_ref[0:sgu_w, :]) + _dot(bn, wo_ref[sgu_w:, :])
    x1_ref[...] = x1
    h2 = _rms(x1, norm2_ref[...])
    h2_ref[...] = h2.astype(BF16)
    hi, lo = _split_bf16(h2)
    logits = _dot(hi, wrh_ref[...]) + _dot(lo, wrh_ref[...]) + _dot(hi, wrl_ref[...]) + br_ref[...]

    lane = lax.broadcasted_iota(jnp.int32, logits.shape, 1)
    work = jnp.where(lane < n_exp, logits, -jnp.inf)
    chosen = jnp.zeros(logits.shape, jnp.bool_)
    experts, values = [], []
    for _ in range(TOP_K):
        top = jnp.max(work, axis=-1, keepdims=True)
        idx = jnp.min(jnp.where(work == top, lane, LANES), axis=-1, keepdims=True)
        hit = lane == idx
        chosen = jnp.logical_or(chosen, hit)
        work = jnp.where(hit, -jnp.inf, work)
        experts.append(idx)
        values.append(top)
    exps = [jnp.exp(v - values[0]) for v in values]
    den = exps[0] + exps[1] + exps[2] + exps[3]

    onehot = jnp.where(chosen, 1.0, 0.0)
    before = _dot(tri_ref[...], onehot.astype(BF16)) + carry[0:1, :]
    carry[0:1, :] = carry[0:1, :] + jnp.sum(onehot, axis=0, keepdims=True)
    cnt_ref[...] = carry[...]

    meta = jnp.zeros(logits.shape, F32)
    for kk in range(TOP_K):
        rank = jnp.sum(jnp.where(lane == experts[kk], before, 0.0), axis=-1, keepdims=True)
        meta = jnp.where(lane == kk, experts[kk].astype(F32), meta)
        meta = jnp.where(lane == TOP_K + kk, exps[kk] / den, meta)
        meta = jnp.where(lane == 2 * TOP_K + kk, rank, meta)
    meta_ref[...] = meta


def _outproj(a_n, attn, xp, xs, out_norm_b, w_o, norm2, w_router, b_router):
    t, sgu_w = a_n.shape
    attn_w = attn.shape[1]
    d_model = w_o.shape[1]
    n_exp = w_router.shape[1]
    assert n_exp <= LANES
    n1 = xp.shape[0] // TOK_TILE
    wr = jnp.pad(w_router, ((0, 0), (0, LANES - n_exp)))
    wr_hi = wr.astype(BF16)
    wr_lo = (wr - wr_hi.astype(F32)).astype(BF16)
    br = jnp.pad(b_router, (0, LANES - n_exp))[None]
    tri = np.tril(np.ones((TOK_TILE, TOK_TILE), np.float32), -1)

    const = lambda shape: pl.BlockSpec(shape, lambda i: (0,) * len(shape))
    tok = lambda w: pl.BlockSpec((TOK_TILE, w), lambda i: (i, 0))
    return pl.pallas_call(
        functools.partial(_outproj_kernel, n1, sgu_w, n_exp),
        grid=(t // TOK_TILE,),
        in_specs=[
            tok(sgu_w), tok(attn_w),
            pl.BlockSpec((TOK_TILE, d_model), lambda i: (jnp.minimum(i, n1 - 1), 0)),
            pl.BlockSpec((TOK_TILE, d_model), lambda i: (jnp.maximum(i - n1, 0), 0)),
            const((1, attn_w)), const(w_o.shape), const((1, d_model)), const((d_model, LANES)),
            const((d_model, LANES)), const((1, LANES)), const((TOK_TILE, TOK_TILE)),
        ],
        out_specs=[tok(d_model), tok(d_model), tok(LANES), const((8, LANES))],
        out_shape=[jax.ShapeDtypeStruct((t, d_model), F32), jax.ShapeDtypeStruct((t, d_model), BF16),
                   jax.ShapeDtypeStruct((t, LANES), F32), jax.ShapeDtypeStruct((8, LANES), F32)],
        scratch_shapes=[pltpu.VMEM((8, LANES), F32)],
        compiler_params=pltpu.CompilerParams(dimension_semantics=("arbitrary",), vmem_limit_bytes=VMEM_LIMIT),
        name="outproj",
    )(a_n, attn, xp, xs, out_norm_b[None], w_o.astype(BF16), norm2[None], wr_hi, wr_lo, br, jnp.asarray(tri, BF16))


def _moe_kernel(d_exp, blk_ref, exp_ref, lo_ref, hi_ref, x_ref, wgu_ref, bgu_ref, wd_ref, bd_ref, y_ref,
                wgu_s, wd_s, act_s):
    it = pl.program_id(0)
    prev = jnp.maximum(it - 1, 0)
    new_expert = jnp.logical_or(it == 0, exp_ref[it] != exp_ref[prev])
    new_block = jnp.logical_or(it == 0, blk_ref[it] != blk_ref[prev])
    lo, hi = lo_ref[it], hi_ref[it]
    active = hi > lo

    @pl.when(jnp.logical_and(new_expert, active))
    def _():
        wgu_s[...] = wgu_ref[...].astype(BF16)
        wd_s[...] = wd_ref[...].astype(BF16)

    @pl.when(active)
    def _():
        x = x_ref[...]
        step = 512
        for n in range(0, d_exp, step):
            gate = _dot(x, wgu_s[:, n:n + step]) + bgu_ref[:, n:n + step]
            up = _dot(x, wgu_s[:, d_exp + n:d_exp + n + step]) + bgu_ref[:, d_exp + n:d_exp + n + step]
            gate = jnp.minimum(gate, SWIGLU_LIMIT)
            up = jnp.clip(up, -SWIGLU_LIMIT, SWIGLU_LIMIT)
            glu = gate * jax.nn.sigmoid(SWIGLU_ALPHA * gate)
            act_s[:, n:n + step] = ((up + 1.0) * glu).astype(BF16)
        y = (_dot(act_s[...], wd_s[...]) + bd_ref[...]).astype(y_ref.dtype)
        row = lax.broadcasted_iota(jnp.int32, (MOE_ROWS, 1), 0)
        mine = jnp.logical_and(row >= lo, row < hi)

        @pl.when(new_block)
        def _():
            y_ref[...] = jnp.where(mine, y, jnp.zeros_like(y))

        @pl.when(jnp.logical_not(new_block))
        def _():
            y_ref[...] = jnp.where(mine, y, y_ref[...])


def _moe_experts(x_rows, items, w_gu, b_gu, w_down, b_down):
    n_rows, d_model = x_rows.shape
    n_exp, _, two_de = w_gu.shape
    d_exp = two_de // 2
    n_items = items[0].shape[0]
    grid_spec = pltpu.PrefetchScalarGridSpec(
        num_scalar_prefetch=4,
        grid=(n_items,),
        in_specs=[
            pl.BlockSpec((MOE_ROWS, d_model), lambda i, blk, ex, lo, hi: (blk[i], 0)),
            pl.BlockSpec((None, d_model, two_de), lambda i, blk, ex, lo, hi: (ex[i], 0, 0)),
            pl.BlockSpec((None, 1, two_de), lambda i, blk, ex, lo, hi: (ex[i], 0, 0)),
            pl.BlockSpec((None, d_exp, d_model), lambda i, blk, ex, lo, hi: (ex[i], 0, 0)),
            pl.BlockSpec((None, 1, d_model), lambda i, blk, ex, lo, hi: (ex[i], 0, 0)),
        ],
        out_specs=pl.BlockSpec((MOE_ROWS, d_model), lambda i, blk, ex, lo, hi: (blk[i], 0)),
        scratch_shapes=[pltpu.VMEM((d_model, two_de), BF16), pltpu.VMEM((d_exp, d_model), BF16),
                        pltpu.VMEM((MOE_ROWS, d_exp), BF16)],
    )
    return pl.pallas_call(
        functools.partial(_moe_kernel, d_exp),
        grid_spec=grid_spec,
        out_shape=jax.ShapeDtypeStruct((n_rows, d_model), BF16),
        compiler_params=pltpu.CompilerParams(dimension_semantics=("arbitrary",), vmem_limit_bytes=VMEM_LIMIT),
        name="moe",
    )(*items, x_rows, w_gu, b_gu[:, None, :], w_down, b_down[:, None, :])


def _route(meta, counts_f, n_exp):
    t = meta.shape[0]
    n_assign = t * TOP_K
    assert n_assign % MOE_ROWS == 0
    top_e = meta[:, 0:TOP_K].astype(jnp.int32)
    rank = meta[:, 2 * TOP_K:3 * TOP_K].astype(jnp.int32)
    counts = counts_f[0, :n_exp].astype(jnp.int32)
    end = jnp.cumsum(counts)
    start = end - counts
    experts = jnp.arange(n_exp, dtype=jnp.int32)
    start_of = jnp.sum(jnp.where(top_e[..., None] == experts, start, 0), axis=-1)
    pos = start_of + rank

    shift = (n_assign - 1).bit_length()
    assert (n_exp << shift) < 2 ** 31
    keys = (top_e.reshape(-1) << shift) + jnp.arange(n_assign, dtype=jnp.int32)
    sorted_tok = (jnp.sort(keys) & ((1 << shift) - 1)) // TOP_K

    n_blocks = n_assign // MOE_ROWS
    n_items = n_blocks + n_exp
    first_blk = start // MOE_ROWS
    n_it = jnp.where(counts > 0, (end - 1) // MOE_ROWS - first_blk + 1, 0)
    it_end = jnp.cumsum(n_it)
    it_start = it_end - n_it
    i = jnp.arange(n_items, dtype=jnp.int32)
    valid = i < it_end[-1]
    e_i = jnp.sum(jnp.minimum(i, it_end[-1] - 1)[:, None] >= it_end[None, :], axis=1).astype(jnp.int32)
    e_i = jnp.minimum(e_i, n_exp - 1)
    pick = lambda table: jnp.sum(jnp.where(e_i[:, None] == experts[None, :], table[None, :], 0), axis=1)
    blk_i = jnp.where(valid, pick(first_blk) + i - pick(it_start), n_blocks - 1)
    lo_i = jnp.where(valid, jnp.clip(pick(start) - blk_i * MOE_ROWS, 0, MOE_ROWS), 0)
    hi_i = jnp.where(valid, jnp.clip(pick(end) - blk_i * MOE_ROWS, 0, MOE_ROWS), 0)
    items = tuple(a.astype(jnp.int32) for a in (blk_i, e_i, lo_i, hi_i))
    return pos, sorted_tok, items


def _combine_kernel(x1_ref, yg_ref, meta_ref, o_ref):
    acc = x1_ref[...]
    for kk in range(TOP_K):
        gate = meta_ref[:, TOP_K + kk:TOP_K + kk + 1]
        acc = acc + gate * yg_ref[kk].astype(F32)
    o_ref[...] = acc


def _combine(x1, yg, meta, row0, rows):
    d_model = x1.shape[1]
    b0 = row0 // TOK_TILE
    tok = lambda w: pl.BlockSpec((TOK_TILE, w), lambda i: (i + b0, 0))
    return pl.pallas_call(
        _combine_kernel,
        grid=(rows // TOK_TILE,),
        in_specs=[tok(d_model), pl.BlockSpec((TOP_K, TOK_TILE, d_model), lambda i: (0, i + b0, 0)), tok(LANES)],
        out_specs=pl.BlockSpec((TOK_TILE, d_model), lambda i: (i, 0)),
        out_shape=jax.ShapeDtypeStruct((rows, d_model), F32),
        compiler_params=pltpu.CompilerParams(dimension_semantics=("arbitrary",), vmem_limit_bytes=VMEM_LIMIT),
        name="combine",
    )(x1, yg, meta)


def kernel(x_prompt, x_sample, norm1, w_in, q_gain, k_gain, rel_bias, sgu_norm, w_s, b_s, out_norm_a, out_norm_b,
           w_o, norm2, w_router, b_router, w_gu, b_gu, w_down, b_down):
    assert norm1.shape[0] == 1, "single-layer trunk"
    bp, sp, d_model = x_prompt.shape
    bs, ss, _ = x_sample.shape
    xp = x_prompt.reshape(bp * sp, d_model)
    xs = x_sample.reshape(bs * ss, d_model)
    tp, t = bp * sp, bp * sp + bs * ss

    q, k, v, a_n = _inproj(xp, xs, norm1[0], w_in[0], q_gain[0], k_gain[0], sgu_norm[0], w_s[0], b_s[0],
                           out_norm_a[0])
    attn = _attention(q, k, v, rel_bias, tp, sp, ss)
    x1, h2, meta, counts = _outproj(a_n, attn, xp, xs, out_norm_b[0], w_o[0], norm2[0], w_router[0], b_router[0])

    pos, sorted_tok, items = _route(meta, counts, w_router.shape[-1])
    x_rows = h2.at[sorted_tok].get(mode="promise_in_bounds")
    y_rows = _moe_experts(x_rows, items, w_gu[0], b_gu[0], w_down[0], b_down[0])
    yg = y_rows.at[pos.T.reshape(-1)].get(mode="promise_in_bounds").reshape(TOP_K, t, d_model)
    out_p = _combine(x1, yg, meta, 0, tp)
    out_s = _combine(x1, yg, meta, tp, t - tp)
    return (out_p.reshape(bp, sp, d_model), out_s.reshape(bs, ss, d_model))
```

```python
import functools
import math

import numpy as np
import jax
import jax.numpy as jnp
from jax import lax
from jax.experimental import pallas as pl
from jax.experimental.pallas import tpu as pltpu

F32 = jnp.float32
BF16 = jnp.bfloat16

EPS = 1e-6
NEG_INF = -1e30
HEAD_DIM = 64
SGU_CHUNK = 128
WINDOWS = ((128, 1), (512, 4), (2048, 16))
N_BUCKETS = 32
MAX_DISTANCE = 1024
TOP_K = 4
SWIGLU_LIMIT = 7.0
SWIGLU_ALPHA = 1.702

LANES = 128
TOK_TILE = 512
ATT_TILE = 2048
ATT_HALO = 1024
ATT_QB = 128
ATT_SIDE = 64
ATT_PIPE = 8
MOE_ROWS = 512
VMEM_LIMIT = 56 * 1024 * 1024


def _dot(a, b):
    return jnp.dot(a, b, preferred_element_type=F32)


def _dot_nt(a, b):
    return lax.dot_general(a, b, (((1,), (1,)), ((), ())), preferred_element_type=F32)


def _rms(x, g):
    ms = jnp.mean(x * x, axis=-1, keepdims=True)
    return x * lax.rsqrt(ms + EPS) * g


def _gelu(x):
    return 0.5 * x * (1.0 + lax.erf(x * (1.0 / math.sqrt(2.0))))


def _split_bf16(x):
    hi = x.astype(BF16)
    lo = (x - hi.astype(F32)).astype(BF16)
    return hi, lo


def _inproj_kernel(n1, attn_w, sgu_w, xp_ref, xs_ref, norm1_ref, win_ref, gsum_ref, gexp_ref,
                   qkg_ref, sgun_ref, wsp_ref, bsb_ref, ona_ref, q_ref, k_ref, v_ref, a_ref):
    i = pl.program_id(0)
    x = jnp.where(i < n1, xp_ref[...], xs_ref[...])
    h = _rms(x, norm1_ref[...]).astype(BF16)

    zqk = _dot(h, win_ref[:, 0:2 * attn_w])
    sq_hi, sq_lo = _split_bf16(zqk * zqk)
    ss = _dot(sq_hi, gsum_ref[...]) + _dot(sq_lo, gsum_ref[...])
    inv = lax.rsqrt(ss * (1.0 / HEAD_DIM) + EPS)
    inv_hi, inv_lo = _split_bf16(inv)
    invb = _dot(inv_hi, gexp_ref[...]) + _dot(inv_lo, gexp_ref[...])
    qk = zqk * invb * qkg_ref[...]
    q_ref[...] = qk[:, 0:attn_w]
    k_ref[...] = qk[:, attn_w:2 * attn_w]
    v_ref[...] = _dot(h, win_ref[:, 2 * attn_w:3 * attn_w])

    c0 = 3 * attn_w
    u = _gelu(_dot(h, win_ref[:, c0:c0 + sgu_w]))
    gv = _gelu(_dot(h, win_ref[:, c0 + sgu_w:c0 + 2 * sgu_w]))
    vsn = _rms(gv, sgun_ref[...]).astype(BF16)
    lane = lax.broadcasted_iota(jnp.int32, (SGU_CHUNK, LANES), 1)
    lo_half = lane < HEAD_DIM
    zero = jnp.zeros((SGU_CHUNK, LANES), BF16)
    for c in range(TOK_TILE // SGU_CHUNK):
        r0 = c * SGU_CHUNK
        parts = []
        for j in range(sgu_w // LANES):
            blk = vsn[r0:r0 + SGU_CHUNK, j * LANES:(j + 1) * LANES]
            rhs = jnp.concatenate([jnp.where(lo_half, blk, zero), jnp.where(lo_half, zero, blk)], axis=0)
            parts.append(_dot(wsp_ref[j], rhs))
        s = jnp.concatenate(parts, axis=1) + bsb_ref[...]
        a = u[r0:r0 + SGU_CHUNK, :] * s
        a_ref[r0:r0 + SGU_CHUNK, :] = _rms(a, ona_ref[...]).astype(BF16)


def _inproj(xp, xs, norm1, w_in, q_gain, k_gain, sgu_norm, w_s, b_s, out_norm_a):
    tp, d_model = xp.shape
    ts = xs.shape[0]
    t = tp + ts
    n_heads_w = q_gain.shape[0]
    sgu_w = sgu_norm.shape[0]
    attn_w = (w_in.shape[1] - 2 * sgu_w) // 3
    n_heads = attn_w // n_heads_w
    n_groups = w_s.shape[0]
    assert n_heads_w == HEAD_DIM and sgu_w // n_groups == HEAD_DIM and w_s.shape[1] == SGU_CHUNK
    assert tp % TOK_TILE == 0 and ts % TOK_TILE == 0 and 2 * n_heads <= LANES
    n1 = tp // TOK_TILE

    heads = np.arange(2 * attn_w) // HEAD_DIM
    gsum = (heads[:, None] == np.arange(LANES)[None, :]).astype(np.float32)
    gexp = gsum.T
    qkg = jnp.concatenate([jnp.tile(q_gain, n_heads) * (HEAD_DIM ** -0.5), jnp.tile(k_gain, n_heads)])
    wsp = jnp.concatenate([w_s[0::2], w_s[1::2]], axis=2).astype(BF16)
    bsb = jnp.repeat(b_s.T, HEAD_DIM, axis=1)

    const = lambda shape: pl.BlockSpec(shape, lambda i: (0,) * len(shape))
    tok = lambda w: pl.BlockSpec((TOK_TILE, w), lambda i: (i, 0))
    return pl.pallas_call(
        functools.partial(_inproj_kernel, n1, attn_w, sgu_w),
        grid=(t // TOK_TILE,),
        in_specs=[
            pl.BlockSpec((TOK_TILE, d_model), lambda i: (jnp.minimum(i, n1 - 1), 0)),
            pl.BlockSpec((TOK_TILE, d_model), lambda i: (jnp.maximum(i - n1, 0), 0)),
            const((1, d_model)), const(w_in.shape), const((2 * attn_w, LANES)), const((LANES, 2 * attn_w)),
            const((1, 2 * attn_w)), const((1, sgu_w)), const(wsp.shape), const((SGU_CHUNK, sgu_w)),
            const((1, sgu_w)),
        ],
        out_specs=[tok(attn_w), tok(attn_w), tok(attn_w), tok(sgu_w)],
        out_shape=[jax.ShapeDtypeStruct((t, attn_w), F32)] * 3 + [jax.ShapeDtypeStruct((t, sgu_w), BF16)],
        compiler_params=pltpu.CompilerParams(dimension_semantics=("arbitrary",), vmem_limit_bytes=VMEM_LIMIT),
        name="inproj",
    )(xp, xs, norm1[None], w_in.astype(BF16), jnp.asarray(gsum, BF16), jnp.asarray(gexp, BF16),
      qkg[None], sgu_norm[None], wsp, bsb, out_norm_a[None])


def _t5_bucket(rel):
    nb = N_BUCKETS // 2
    bucket = (rel > 0).astype(np.int32) * nb
    n = np.abs(rel)
    max_exact = nb // 2
    large = max_exact + (np.log(np.maximum(n, 1) / max_exact)
                         / np.log(MAX_DISTANCE / max_exact) * (nb - max_exact)).astype(np.int32)
    large = np.minimum(large, nb - 1)
    return (bucket + np.where(n < max_exact, n, large)).astype(np.int32)


def _branch_layout():
    out, kbase = [], 0
    for bi, (window, dil) in enumerate(WINDOWS):
        assert window // (2 * dil) == ATT_SIDE
        m = ATT_TILE // dil
        seg = m + 2 * ATT_SIDE
        out.append((dil, m, seg, kbase, bi * ATT_TILE))
        kbase += dil * seg
    return out, kbase


def _strided(ref, start, size, stride):
    if stride == 1:
        return ref[start:start + size, :]
    return ref[pl.ds(start, size, stride=stride), :]


def _attn_kernel(tiles_p, per_seq_p, per_seq_s, q_ref, km_ref, kp_ref, kn_ref, vm_ref, vp_ref, vn_ref,
                 bias_ref, o_ref, qs, kts, vs, o_scr, l_scr, s_scr, p_scr, m_scr, k4, v4, q4):
    i = pl.program_id(0)
    w = jnp.where(i < tiles_p, i % per_seq_p, (i - tiles_p) % per_seq_s)
    last = jnp.where(i < tiles_p, per_seq_p - 1, per_seq_s - 1)
    left_ok = w > 0
    right_ok = w < last
    layout, _ = _branch_layout()
    mid = WINDOWS[1][1]
    assert WINDOWS[2][1] == mid * mid
    halo4, main4 = ATT_HALO // mid, ATT_TILE // mid
    win4 = 2 * halo4 + main4

    @pl.when(jnp.logical_and(i == 0, pl.program_id(1) == 0))
    def _():
        vs[:, LANES:] = jnp.ones((vs.shape[0], LANES), BF16)

    for r in range(mid):
        for dst, main, prev, nxt in ((k4, km_ref, kp_ref, kn_ref), (v4, vm_ref, vp_ref, vn_ref)):
            dst[r * win4:r * win4 + halo4, :] = _strided(prev, r, halo4, mid)
            dst[r * win4 + halo4:r * win4 + halo4 + main4, :] = _strided(main, r, main4, mid)
            dst[r * win4 + halo4 + main4:(r + 1) * win4, :] = _strided(nxt, r, halo4, mid)
        q4[r * main4:(r + 1) * main4, :] = _strided(q_ref, r, main4, mid)

    def key_rows(src_k, src_v, start, stride, kbase_rows):
        kc = _strided(src_k, start, ATT_QB, stride)
        kts[kbase_rows // ATT_QB] = kc.T.astype(BF16)
        vs[kbase_rows:kbase_rows + ATT_QB, 0:LANES] = _strided(src_v, start, ATT_QB, stride).astype(BF16)

    for bi, (dil, m, seg, kbase, qbase) in enumerate(layout):
        nchunk = seg // ATT_QB
        for r in range(dil):
            o = kbase + r * seg
            if dil == 1:
                for c in range(nchunk):
                    lo = c * ATT_QB - ATT_SIDE
                    if c == 0:
                        kc = jnp.concatenate([kp_ref[ATT_HALO - ATT_SIDE:, :], km_ref[0:ATT_QB - ATT_SIDE, :]], axis=0)
                        vc = jnp.concatenate([vp_ref[ATT_HALO - ATT_SIDE:, :], vm_ref[0:ATT_QB - ATT_SIDE, :]], axis=0)
                    elif c == nchunk - 1:
                        kc = jnp.concatenate([km_ref[lo:, :], kn_ref[0:ATT_SIDE, :]], axis=0)
                        vc = jnp.concatenate([vm_ref[lo:, :], vn_ref[0:ATT_SIDE, :]], axis=0)
                    else:
                        kc = km_ref[lo:lo + ATT_QB, :]
                        vc = vm_ref[lo:lo + ATT_QB, :]
                    kts[(o + c * ATT_QB) // ATT_QB] = kc.T.astype(BF16)
                    vs[o + c * ATT_QB:o + (c + 1) * ATT_QB, 0:LANES] = vc.astype(BF16)
                qs[qbase:qbase + m, :] = q_ref[...].astype(BF16)
            else:
                r4, s = r % mid, r // mid
                step = dil // mid
                first = r4 * win4 + (halo4 - ATT_SIDE * step) + s
                for c in range(nchunk):
                    key_rows(k4, v4, first + c * ATT_QB * step, step, o + c * ATT_QB)
                qs[qbase + r * m:qbase + (r + 1) * m, :] = _strided(q4, r4 * main4 + s, m, step).astype(BF16)

    lane = lax.broadcasted_iota(jnp.int32, (ATT_QB, LANES), 1)
    head0 = lane < HEAD_DIM
    col = lax.broadcasted_iota(jnp.int32, (1, 2 * ATT_QB), 1)
    cells = ATT_TILE // ATT_QB
    total = len(layout) * cells
    blk_bits = [(m // ATT_QB).bit_length() - 1 for _, m, _, _, _ in layout]
    seg_chunks = [seg // ATT_QB for _, _, seg, _, _ in layout]
    base_chunks = [kbase // ATT_QB for _, _, _, kbase, _ in layout]

    def cell_params(g):
        bi = g // cells
        idx = g % cells
        if isinstance(g, int):
            pick = lambda vals: vals[bi]
        else:
            pick = lambda vals: jnp.where(bi == 0, vals[0], jnp.where(bi == 1, vals[1], vals[2]))
        bits = pick(blk_bits)
        last_blk = (1 << bits) - 1
        blk = idx & last_blk
        return bi, blk, last_blk, pick(base_chunks) + (idx >> bits) * pick(seg_chunks) + blk

    def stage_logits(g, slot):
        bi, blk, last_blk, kchunk = cell_params(g)
        left_bad = jnp.logical_and(blk == 0, jnp.logical_not(left_ok))
        right_bad = jnp.logical_and(blk == last_blk, jnp.logical_not(right_ok))
        edge = (jnp.where(jnp.logical_and(col < ATT_SIDE, left_bad), NEG_INF, 0.0)
                + jnp.where(jnp.logical_and(col >= 2 * ATT_QB - ATT_SIDE, right_bad), NEG_INF, 0.0))
        qc = qs[pl.ds(pl.multiple_of(g * ATT_QB, ATT_QB), ATT_QB), :]
        zero = jnp.zeros_like(qc)
        lhs = jnp.concatenate([jnp.where(head0, qc, zero), jnp.where(head0, zero, qc)], axis=0)
        kt = jnp.concatenate([kts[kchunk], kts[kchunk + 1]], axis=1)
        s_scr[slot] = _dot(lhs, kt) + bias_ref[bi] + edge

    def stage_softmax(slot):
        part = ATT_QB // 2
        for c in range(2 * ATT_QB // part):
            s = s_scr[slot, c * part:(c + 1) * part, :]
            mx = jnp.max(s, axis=-1, keepdims=True)
            p_scr[slot, c * part:(c + 1) * part, :] = jnp.exp(s - mx).astype(BF16)
            m_scr[slot, c * part:(c + 1) * part, :] = jnp.broadcast_to(mx, (part, LANES))

    def stage_values(g, slot):
        _, _, _, kchunk = cell_params(g)
        koff = pl.multiple_of(kchunk * ATT_QB, ATT_QB)
        r = _dot(p_scr[slot], vs[pl.ds(koff, 2 * ATT_QB), :])
        num = jnp.where(head0, r[0:ATT_QB, 0:LANES], r[ATT_QB:, 0:LANES])
        den = jnp.where(head0, r[0:ATT_QB, LANES:], r[ATT_QB:, LANES:])
        mx = jnp.where(head0, m_scr[slot, 0:ATT_QB, :], m_scr[slot, ATT_QB:, :])
        rows = pl.ds(pl.multiple_of(g * ATT_QB, ATT_QB), ATT_QB)
        o_scr[rows, :] = num / den
        l_scr[rows, :] = mx + jnp.log(den)

    width = ATT_PIPE

    def group(c, do_values, do_softmax, do_logits):
        if do_values:
            for u in range(width):
                stage_values(c - width + u, u)
        if do_softmax:
            for u in range(width):
                stage_softmax(u)
        if do_logits:
            for u in range(width):
                stage_logits(c + width + u, u)

    group(-width, False, False, True)
    group(0, False, True, True)

    def trip(t, carry):
        group((t + 1) * width, True, True, True)
        return carry

    lax.fori_loop(0, total // width - 2, trip, 0)
    group(total - width, True, True, False)
    group(total, True, False, False)

    big = WINDOWS[-1][1]
    rows = ATT_TILE // big
    for r in range(big):
        os_, ls_ = [], []
        for dil, m, seg, kbase, qbase in layout:
            start = qbase + (r % dil) * m + r // dil
            os_.append(_strided(o_scr, start, rows, big // dil))
            ls_.append(_strided(l_scr, start, rows, big // dil))
        mx = jnp.maximum(jnp.maximum(ls_[0], ls_[1]), ls_[2])
        ws = [jnp.exp(l - mx) for l in ls_]
        num = ws[0] * os_[0] + ws[1] * os_[1] + ws[2] * os_[2]
        o_ref[pl.ds(r, rows, stride=big), :] = num / (ws[0] + ws[1] + ws[2])


def _attention(q, k, v, rel_bias, tp, seq_p, seq_s):
    t, attn_w = q.shape
    n_heads = attn_w // HEAD_DIM
    pair = LANES // HEAD_DIM
    assert seq_p % ATT_TILE == 0 and seq_s % ATT_TILE == 0 and ATT_TILE == 2 * ATT_HALO
    assert ATT_TILE // WINDOWS[-1][1] == ATT_QB and pair == 2
    layout, krows = _branch_layout()

    ii = np.arange(ATT_QB)[:, None]
    jj = np.arange(2 * ATT_QB)[None, :]
    rel = jj - ATT_SIDE - ii
    band = np.abs(rel) <= ATT_SIDE
    buckets = np.stack([_t5_bucket(rel * dil) for _, dil in WINDOWS])
    bucket_of = jnp.asarray(buckets)[None]
    bias = jnp.zeros((n_heads,) + buckets.shape, F32)
    for b in range(N_BUCKETS):
        bias = jnp.where(bucket_of == b, rel_bias[b].astype(F32)[:, None, None, None], bias)
    bias = jnp.where(jnp.asarray(band)[None, None], bias, NEG_INF)
    nb = len(WINDOWS)
    bias = bias.reshape(n_heads // pair, pair, nb, ATT_QB, 2 * ATT_QB).transpose(0, 2, 1, 3, 4)
    bias = bias.reshape(n_heads // pair, nb, pair * ATT_QB, 2 * ATT_QB)

    halo_blocks = t // ATT_HALO
    per_tile = ATT_TILE // ATT_HALO
    mid = WINDOWS[1][1]
    main = pl.BlockSpec((ATT_TILE, LANES), lambda i, j: (i, j))
    prev = pl.BlockSpec((ATT_HALO, LANES), lambda i, j: (jnp.maximum(i * per_tile - 1, 0), j))
    nxt = pl.BlockSpec((ATT_HALO, LANES), lambda i, j: (jnp.minimum((i + 1) * per_tile, halo_blocks - 1), j))
    n_q = nb * ATT_TILE
    return pl.pallas_call(
        functools.partial(_attn_kernel, tp // ATT_TILE, seq_p // ATT_TILE, seq_s // ATT_TILE),
        grid=(t // ATT_TILE, attn_w // LANES),
        in_specs=[main, main, prev, nxt, main, prev, nxt,
                  pl.BlockSpec((None, nb, pair * ATT_QB, 2 * ATT_QB), lambda i, j: (j, 0, 0, 0))],
        out_specs=main,
        out_shape=jax.ShapeDtypeStruct((t, attn_w), F32),
        scratch_shapes=[pltpu.VMEM((n_q, LANES), BF16),
                        pltpu.VMEM((krows // ATT_QB, LANES, ATT_QB), BF16),
                        pltpu.VMEM((krows, 2 * LANES), BF16),
                        pltpu.VMEM((n_q, LANES), F32), pltpu.VMEM((n_q, LANES), F32),
                        pltpu.VMEM((ATT_PIPE, pair * ATT_QB, 2 * ATT_QB), F32),
                        pltpu.VMEM((ATT_PIPE, pair * ATT_QB, 2 * ATT_QB), BF16),
                        pltpu.VMEM((ATT_PIPE, pair * ATT_QB, LANES), F32),
                        pltpu.VMEM(((ATT_TILE + 2 * ATT_HALO), LANES), F32),
                        pltpu.VMEM(((ATT_TILE + 2 * ATT_HALO), LANES), F32),
                        pltpu.VMEM((ATT_TILE, LANES), F32)],
        compiler_params=pltpu.CompilerParams(dimension_semantics=("arbitrary", "arbitrary"),
                                             vmem_limit_bytes=VMEM_LIMIT),
        name="attn",
    )(q, k, k, k, v, v, v, bias)


def _outproj_kernel(sgu_w, n_exp, a_ref, attn_ref, x_ref, onb_ref, wo_ref, norm2_ref, wrh_ref, wrl_ref,
                    br_ref, tri_ref, x1_ref, h2_ref, meta_ref, cnt_ref, carry):
    i = pl.program_id(0)

    @pl.when(i == 0)
    def _():
        carry[...] = jnp.zeros_like(carry)

    bn = _rms(attn_ref[...], onb_ref[...]).astype(BF16)
    x1 = x_ref[...] + _dot(a_ref[...], wo_ref[0:sgu_w, :]) + _dot(bn, wo_ref[sgu_w:, :])
    x1_ref[...] = x1
    h2 = _rms(x1, norm2_ref[...])
    h2_ref[...] = h2.astype(BF16)
    hi, lo = _split_bf16(h2)
    logits = _dot(hi, wrh_ref[...]) + _dot(lo, wrh_ref[...]) + _dot(hi, wrl_ref[...]) + br_ref[...]

    lane = lax.broadcasted_iota(jnp.int32, logits.shape, 1)
    work = jnp.where(lane < n_exp, logits, -jnp.inf)
    chosen = jnp.zeros(logits.shape, jnp.bool_)
    experts, values = [], []
    for _ in range(TOP_K):
        top = jnp.max(work, axis=-1, keepdims=True)
        idx = jnp.min(jnp.where(work == top, lane, LANES), axis=-1, keepdims=True)
        hit = lane == idx
        chosen = jnp.logical_or(chosen, hit)
        work = jnp.where(hit, -jnp.inf, work)
        experts.append(idx)
        values.append(top)
    exps = [jnp.exp(v - values[0]) for v in values]
    den = exps[0] + exps[1] + exps[2] + exps[3]

    onehot = jnp.where(chosen, 1.0, 0.0)
    before = _dot(tri_ref[...], onehot.astype(BF16)) + carry[0:1, :]
    carry[0:1, :] = carry[0:1, :] + jnp.sum(onehot, axis=0, keepdims=True)
    cnt_ref[...] = carry[...]

    meta = jnp.zeros(logits.shape, F32)
    for kk in range(TOP_K):
        rank = jnp.sum(jnp.where(lane == experts[kk], before, 0.0), axis=-1, keepdims=True)
        meta = jnp.where(lane == kk, experts[kk].astype(F32), meta)
        meta = jnp.where(lane == TOP_K + kk, exps[kk] / den, meta)
        meta = jnp.where(lane == 2 * TOP_K + kk, rank, meta)
    meta_ref[...] = meta


def _outproj(a_n, attn, x, row0, out_norm_b, w_o, norm2, w_router, b_router):
    sgu_w = a_n.shape[1]
    attn_w = attn.shape[1]
    t, d_model = x.shape
    n_exp = w_router.shape[1]
    assert n_exp <= LANES and row0 % TOK_TILE == 0 and t % TOK_TILE == 0
    b0 = row0 // TOK_TILE
    wr = jnp.pad(w_router, ((0, 0), (0, LANES - n_exp)))
    wr_hi = wr.astype(BF16)
    wr_lo = (wr - wr_hi.astype(F32)).astype(BF16)
    br = jnp.pad(b_router, (0, LANES - n_exp))[None]
    tri = np.tril(np.ones((TOK_TILE, TOK_TILE), np.float32), -1)

    const = lambda shape: pl.BlockSpec(shape, lambda i: (0,) * len(shape))
    tok = lambda w: pl.BlockSpec((TOK_TILE, w), lambda i: (i, 0))
    src = lambda w: pl.BlockSpec((TOK_TILE, w), lambda i: (i + b0, 0))
    return pl.pallas_call(
        functools.partial(_outproj_kernel, sgu_w, n_exp),
        grid=(t // TOK_TILE,),
        in_specs=[
            src(sgu_w), src(attn_w), tok(d_model),
            const((1, attn_w)), const(w_o.shape), const((1, d_model)), const((d_model, LANES)),
            const((d_model, LANES)), const((1, LANES)), const((TOK_TILE, TOK_TILE)),
        ],
        out_specs=[tok(d_model), tok(d_model), tok(LANES), const((8, LANES))],
        out_shape=[jax.ShapeDtypeStruct((t, d_model), F32), jax.ShapeDtypeStruct((t, d_model), BF16),
                   jax.ShapeDtypeStruct((t, LANES), F32), jax.ShapeDtypeStruct((8, LANES), F32)],
        scratch_shapes=[pltpu.VMEM((8, LANES), F32)],
        compiler_params=pltpu.CompilerParams(dimension_semantics=("arbitrary",), vmem_limit_bytes=VMEM_LIMIT),
        name="outproj",
    )(a_n, attn, x, out_norm_b[None], w_o.astype(BF16), norm2[None], wr_hi, wr_lo, br, jnp.asarray(tri, BF16))


def _moe_kernel(d_exp, blk_ref, exp_ref, lo_ref, hi_ref, x_ref, wgu_ref, bgu_ref, wd_ref, bd_ref, y_ref,
                wgu_s, wd_s, act_s):
    it = pl.program_id(0)
    prev = jnp.maximum(it - 1, 0)
    new_expert = jnp.logical_or(it == 0, exp_ref[it] != exp_ref[prev])
    new_block = jnp.logical_or(it == 0, blk_ref[it] != blk_ref[prev])
    lo, hi = lo_ref[it], hi_ref[it]
    active = hi > lo

    @pl.when(jnp.logical_and(new_expert, active))
    def _():
        wgu_s[...] = wgu_ref[...].astype(BF16)
        wd_s[...] = wd_ref[...].astype(BF16)

    @pl.when(active)
    def _():
        x = x_ref[...]
        step = 512
        for n in range(0, d_exp, step):
            gate = _dot(x, wgu_s[:, n:n + step]) + bgu_ref[:, n:n + step]
            up = _dot(x, wgu_s[:, d_exp + n:d_exp + n + step]) + bgu_ref[:, d_exp + n:d_exp + n + step]
            gate = jnp.minimum(gate, SWIGLU_LIMIT)
            up = jnp.clip(up, -SWIGLU_LIMIT, SWIGLU_LIMIT)
            glu = gate * jax.nn.sigmoid(SWIGLU_ALPHA * gate)
            act_s[:, n:n + step] = ((up + 1.0) * glu).astype(BF16)
        y = (_dot(act_s[...], wd_s[...]) + bd_ref[...]).astype(y_ref.dtype)
        row = lax.broadcasted_iota(jnp.int32, (MOE_ROWS, 1), 0)
        mine = jnp.logical_and(row >= lo, row < hi)

        @pl.when(new_block)
        def _():
            y_ref[...] = jnp.where(mine, y, jnp.zeros_like(y))

        @pl.when(jnp.logical_not(new_block))
        def _():
            y_ref[...] = jnp.where(mine, y, y_ref[...])


def _moe_experts(x_rows, items, w_gu, b_gu, w_down, b_down):
    n_rows, d_model = x_rows.shape
    n_exp, _, two_de = w_gu.shape
    d_exp = two_de // 2
    n_items = items[0].shape[0]
    grid_spec = pltpu.PrefetchScalarGridSpec(
        num_scalar_prefetch=4,
        grid=(n_items,),
        in_specs=[
            pl.BlockSpec((MOE_ROWS, d_model), lambda i, blk, ex, lo, hi: (blk[i], 0)),
            pl.BlockSpec((None, d_model, two_de), lambda i, blk, ex, lo, hi: (ex[i], 0, 0)),
            pl.BlockSpec((None, 1, two_de), lambda i, blk, ex, lo, hi: (ex[i], 0, 0)),
            pl.BlockSpec((None, d_exp, d_model), lambda i, blk, ex, lo, hi: (ex[i], 0, 0)),
            pl.BlockSpec((None, 1, d_model), lambda i, blk, ex, lo, hi: (ex[i], 0, 0)),
        ],
        out_specs=pl.BlockSpec((MOE_ROWS, d_model), lambda i, blk, ex, lo, hi: (blk[i], 0)),
        scratch_shapes=[pltpu.VMEM((d_model, two_de), BF16), pltpu.VMEM((d_exp, d_model), BF16),
                        pltpu.VMEM((MOE_ROWS, d_exp), BF16)],
    )
    return pl.pallas_call(
        functools.partial(_moe_kernel, d_exp),
        grid_spec=grid_spec,
        out_shape=jax.ShapeDtypeStruct((n_rows, d_model), BF16),
        compiler_params=pltpu.CompilerParams(dimension_semantics=("arbitrary",), vmem_limit_bytes=VMEM_LIMIT),
        name="moe",
    )(*items, x_rows, w_gu, b_gu[:, None, :], w_down, b_down[:, None, :])


def _route(meta, counts_f, n_exp):
    t = meta.shape[0]
    n_assign = t * TOP_K
    assert n_assign % MOE_ROWS == 0
    top_e = meta[:, 0:TOP_K].astype(jnp.int32)
    rank = meta[:, 2 * TOP_K:3 * TOP_K].astype(jnp.int32)
    counts = counts_f[0, :n_exp].astype(jnp.int32)
    end = jnp.cumsum(counts)
    start = end - counts
    experts = jnp.arange(n_exp, dtype=jnp.int32)
    start_of = jnp.sum(jnp.where(top_e[..., None] == experts, start, 0), axis=-1)
    pos = start_of + rank

    shift = (n_assign - 1).bit_length()
    assert (n_exp << shift) < 2 ** 31
    keys = (top_e.reshape(-1) << shift) + jnp.arange(n_assign, dtype=jnp.int32)
    sorted_tok = (jnp.sort(keys) & ((1 << shift) - 1)) // TOP_K

    n_blocks = n_assign // MOE_ROWS
    n_items = n_blocks + n_exp
    first_blk = start // MOE_ROWS
    n_it = jnp.where(counts > 0, (end - 1) // MOE_ROWS - first_blk + 1, 0)
    it_end = jnp.cumsum(n_it)
    it_start = it_end - n_it
    i = jnp.arange(n_items, dtype=jnp.int32)
    valid = i < it_end[-1]
    e_i = jnp.sum(jnp.minimum(i, it_end[-1] - 1)[:, None] >= it_end[None, :], axis=1).astype(jnp.int32)
    e_i = jnp.minimum(e_i, n_exp - 1)
    pick = lambda table: jnp.sum(jnp.where(e_i[:, None] == experts[None, :], table[None, :], 0), axis=1)
    blk_i = jnp.where(valid, pick(first_blk) + i - pick(it_start), n_blocks - 1)
    lo_i = jnp.where(valid, jnp.clip(pick(start) - blk_i * MOE_ROWS, 0, MOE_ROWS), 0)
    hi_i = jnp.where(valid, jnp.clip(pick(end) - blk_i * MOE_ROWS, 0, MOE_ROWS), 0)
    items = tuple(a.astype(jnp.int32) for a in (blk_i, e_i, lo_i, hi_i))
    return pos, sorted_tok, items


def _combine_kernel(x1_ref, yg_ref, meta_ref, o_ref):
    acc = x1_ref[...]
    for kk in range(TOP_K):
        gate = meta_ref[:, TOP_K + kk:TOP_K + kk + 1]
        acc = acc + gate * yg_ref[kk].astype(F32)
    o_ref[...] = acc


def _combine(x1, yg, meta):
    rows, d_model = x1.shape
    tok = lambda w: pl.BlockSpec((TOK_TILE, w), lambda i: (i, 0))
    return pl.pallas_call(
        _combine_kernel,
        grid=(rows // TOK_TILE,),
        in_specs=[tok(d_model), pl.BlockSpec((TOP_K, TOK_TILE, d_model), lambda i: (0, i, 0)), tok(LANES)],
        out_specs=tok(d_model),
        out_shape=jax.ShapeDtypeStruct((rows, d_model), F32),
        compiler_params=pltpu.CompilerParams(dimension_semantics=("arbitrary",), vmem_limit_bytes=VMEM_LIMIT),
        name="combine",
    )(x1, yg, meta)


def kernel(x_prompt, x_sample, norm1, w_in, q_gain, k_gain, rel_bias, sgu_norm, w_s, b_s, out_norm_a, out_norm_b,
           w_o, norm2, w_router, b_router, w_gu, b_gu, w_down, b_down):
    assert norm1.shape[0] == 1, "single-layer trunk"
    bp, sp, d_model = x_prompt.shape
    bs, ss, _ = x_sample.shape
    xp = x_prompt.reshape(bp * sp, d_model)
    xs = x_sample.reshape(bs * ss, d_model)
    tp, t = bp * sp, bp * sp + bs * ss

    q, k, v, a_n = _inproj(xp, xs, norm1[0], w_in[0], q_gain[0], k_gain[0], sgu_norm[0], w_s[0], b_s[0],
                           out_norm_a[0])
    attn = _attention(q, k, v, rel_bias, tp, sp, ss)

    outs = []
    for x, row0 in ((xp, 0), (xs, tp)):
        rows = x.shape[0]
        x1, h2, meta, counts = _outproj(a_n, attn, x, row0, out_norm_b[0], w_o[0], norm2[0], w_router[0],
                                        b_router[0])
        pos, sorted_tok, items = _route(meta, counts, w_router.shape[-1])
        x_rows = h2.at[sorted_tok].get(mode="promise_in_bounds")
        y_rows = _moe_experts(x_rows, items, w_gu[0], b_gu[0], w_down[0], b_down[0])
        yg = y_rows.at[pos.T.reshape(-1)].get(mode="promise_in_bounds").reshape(TOP_K, rows, d_model)
        outs.append(_combine(x1, yg, meta))
    return (outs[0].reshape(bp, sp, d_model), outs[1].reshape(bs, ss, d_model))
```

```python
import functools
import math

import numpy as np
import jax
import jax.numpy as jnp
from jax import lax
from jax.experimental import pallas as pl
from jax.experimental.pallas import tpu as pltpu

F32 = jnp.float32
BF16 = jnp.bfloat16

EPS = 1e-6
NEG_INF = -1e30
HEAD_DIM = 64
SGU_CHUNK = 128
WINDOWS = ((128, 1), (512, 4), (2048, 16))
N_BUCKETS = 32
MAX_DISTANCE = 1024
TOP_K = 4
SWIGLU_LIMIT = 7.0
SWIGLU_ALPHA = 1.702

LANES = 128
TOK_TILE = 512
ATT_TILE = 2048
ATT_HALO = 1024
ATT_QB = 128
ATT_SIDE = 64
ATT_PIPE = 8
MOE_ROWS = 512
VMEM_LIMIT = 56 * 1024 * 1024


def _dot(a, b):
    return jnp.dot(a, b, preferred_element_type=F32)


def _dot_nt(a, b):
    return lax.dot_general(a, b, (((1,), (1,)), ((), ())), preferred_element_type=F32)


def _rms(x, g):
    ms = jnp.mean(x * x, axis=-1, keepdims=True)
    return x * lax.rsqrt(ms + EPS) * g


def _gelu(x):
    return 0.5 * x * (1.0 + lax.erf(x * (1.0 / math.sqrt(2.0))))


def _split_bf16(x):
    hi = x.astype(BF16)
    lo = (x - hi.astype(F32)).astype(BF16)
    return hi, lo


def _inproj_kernel(n1, attn_w, sgu_w, xp_ref, xs_ref, norm1_ref, win_ref, gsum_ref, gexp_ref,
                   qkg_ref, sgun_ref, wsp_ref, bsb_ref, ona_ref, q_ref, k_ref, v_ref, a_ref):
    i = pl.program_id(0)
    x = jnp.where(i < n1, xp_ref[...], xs_ref[...])
    h = _rms(x, norm1_ref[...]).astype(BF16)

    zqk = _dot(h, win_ref[:, 0:2 * attn_w])
    sq_hi, sq_lo = _split_bf16(zqk * zqk)
    ss = _dot(sq_hi, gsum_ref[...]) + _dot(sq_lo, gsum_ref[...])
    inv = lax.rsqrt(ss * (1.0 / HEAD_DIM) + EPS)
    inv_hi, inv_lo = _split_bf16(inv)
    invb = _dot(inv_hi, gexp_ref[...]) + _dot(inv_lo, gexp_ref[...])
    qk = zqk * invb * qkg_ref[...]
    q_ref[...] = qk[:, 0:attn_w]
    k_ref[...] = qk[:, attn_w:2 * attn_w]
    v_ref[...] = _dot(h, win_ref[:, 2 * attn_w:3 * attn_w])

    c0 = 3 * attn_w
    u = _gelu(_dot(h, win_ref[:, c0:c0 + sgu_w]))
    gv = _gelu(_dot(h, win_ref[:, c0 + sgu_w:c0 + 2 * sgu_w]))
    vsn = _rms(gv, sgun_ref[...]).astype(BF16)
    lane = lax.broadcasted_iota(jnp.int32, (SGU_CHUNK, LANES), 1)
    lo_half = lane < HEAD_DIM
    zero = jnp.zeros((SGU_CHUNK, LANES), BF16)
    for c in range(TOK_TILE // SGU_CHUNK):
        r0 = c * SGU_CHUNK
        parts = []
        for j in range(sgu_w // LANES):
            blk = vsn[r0:r0 + SGU_CHUNK, j * LANES:(j + 1) * LANES]
            rhs = jnp.concatenate([jnp.where(lo_half, blk, zero), jnp.where(lo_half, zero, blk)], axis=0)
            parts.append(_dot(wsp_ref[j], rhs))
        s = jnp.concatenate(parts, axis=1) + bsb_ref[...]
        a = u[r0:r0 + SGU_CHUNK, :] * s
        a_ref[r0:r0 + SGU_CHUNK, :] = _rms(a, ona_ref[...]).astype(BF16)


def _inproj(xp, xs, norm1, w_in, q_gain, k_gain, sgu_norm, w_s, b_s, out_norm_a):
    tp, d_model = xp.shape
    ts = xs.shape[0]
    t = tp + ts
    n_heads_w = q_gain.shape[0]
    sgu_w = sgu_norm.shape[0]
    attn_w = (w_in.shape[1] - 2 * sgu_w) // 3
    n_heads = attn_w // n_heads_w
    n_groups = w_s.shape[0]
    assert n_heads_w == HEAD_DIM and sgu_w // n_groups == HEAD_DIM and w_s.shape[1] == SGU_CHUNK
    assert tp % TOK_TILE == 0 and ts % TOK_TILE == 0 and 2 * n_heads <= LANES
    n1 = tp // TOK_TILE

    heads = np.arange(2 * attn_w) // HEAD_DIM
    gsum = (heads[:, None] == np.arange(LANES)[None, :]).astype(np.float32)
    gexp = gsum.T
    qkg = jnp.concatenate([jnp.tile(q_gain, n_heads) * (HEAD_DIM ** -0.5), jnp.tile(k_gain, n_heads)])
    wsp = jnp.concatenate([w_s[0::2], w_s[1::2]], axis=2).astype(BF16)
    bsb = jnp.repeat(b_s.T, HEAD_DIM, axis=1)

    const = lambda shape: pl.BlockSpec(shape, lambda i: (0,) * len(shape))
    tok = lambda w: pl.BlockSpec((TOK_TILE, w), lambda i: (i, 0))
    return pl.pallas_call(
        functools.partial(_inproj_kernel, n1, attn_w, sgu_w),
        grid=(t // TOK_TILE,),
        in_specs=[
            pl.BlockSpec((TOK_TILE, d_model), lambda i: (jnp.minimum(i, n1 - 1), 0)),
            pl.BlockSpec((TOK_TILE, d_model), lambda i: (jnp.maximum(i - n1, 0), 0)),
            const((1, d_model)), const(w_in.shape), const((2 * attn_w, LANES)), const((LANES, 2 * attn_w)),
            const((1, 2 * attn_w)), const((1, sgu_w)), const(wsp.shape), const((SGU_CHUNK, sgu_w)),
            const((1, sgu_w)),
        ],
        out_specs=[tok(attn_w), tok(attn_w), tok(attn_w), tok(sgu_w)],
        out_shape=[jax.ShapeDtypeStruct((t, attn_w), F32)] * 3 + [jax.ShapeDtypeStruct((t, sgu_w), BF16)],
        compiler_params=pltpu.CompilerParams(dimension_semantics=("arbitrary",), vmem_limit_bytes=VMEM_LIMIT),
        name="inproj",
    )(xp, xs, norm1[None], w_in.astype(BF16), jnp.asarray(gsum, BF16), jnp.asarray(gexp, BF16),
      qkg[None], sgu_norm[None], wsp, bsb, out_norm_a[None])


def _t5_bucket(rel):
    nb = N_BUCKETS // 2
    bucket = (rel > 0).astype(np.int32) * nb
    n = np.abs(rel)
    max_exact = nb // 2
    large = max_exact + (np.log(np.maximum(n, 1) / max_exact)
                         / np.log(MAX_DISTANCE / max_exact) * (nb - max_exact)).astype(np.int32)
    large = np.minimum(large, nb - 1)
    return (bucket + np.where(n < max_exact, n, large)).astype(np.int32)


def _branch_layout():
    out, kbase = [], 0
    for bi, (window, dil) in enumerate(WINDOWS):
        assert window // (2 * dil) == ATT_SIDE
        m = ATT_TILE // dil
        seg = m + 2 * ATT_SIDE
        out.append((dil, m, seg, kbase, bi * ATT_TILE))
        kbase += dil * seg
    return out, kbase


def _strided(ref, start, size, stride):
    if stride == 1:
        return ref[start:start + size, :]
    return ref[pl.ds(start, size, stride=stride), :]


def _attn_kernel(tiles_p, per_seq_p, per_seq_s, q_ref, km_ref, kp_ref, kn_ref, vm_ref, vp_ref, vn_ref,
                 bias_ref, o_ref, qs, kts, vs, o_scr, l_scr, s_scr, p_scr, m_scr, k4, v4, q4):
    i = pl.program_id(0)
    w = jnp.where(i < tiles_p, i % per_seq_p, (i - tiles_p) % per_seq_s)
    last = jnp.where(i < tiles_p, per_seq_p - 1, per_seq_s - 1)
    left_ok = w > 0
    right_ok = w < last
    layout, _ = _branch_layout()
    mid = WINDOWS[1][1]
    assert WINDOWS[2][1] == mid * mid
    halo4, main4 = ATT_HALO // mid, ATT_TILE // mid
    win4 = 2 * halo4 + main4

    @pl.when(jnp.logical_and(i == 0, pl.program_id(1) == 0))
    def _():
        vs[:, LANES:] = jnp.ones((vs.shape[0], LANES), BF16)

    for r in range(mid):
        for dst, main, prev, nxt in ((k4, km_ref, kp_ref, kn_ref), (v4, vm_ref, vp_ref, vn_ref)):
            dst[r * win4:r * win4 + halo4, :] = _strided(prev, r, halo4, mid)
            dst[r * win4 + halo4:r * win4 + halo4 + main4, :] = _strided(main, r, main4, mid)
            dst[r * win4 + halo4 + main4:(r + 1) * win4, :] = _strided(nxt, r, halo4, mid)
        q4[r * main4:(r + 1) * main4, :] = _strided(q_ref, r, main4, mid)

    def key_rows(src_k, src_v, start, stride, kbase_rows):
        kc = _strided(src_k, start, ATT_QB, stride)
        kts[kbase_rows // ATT_QB] = kc.T.astype(BF16)
        vs[kbase_rows:kbase_rows + ATT_QB, 0:LANES] = _strided(src_v, start, ATT_QB, stride).astype(BF16)

    for bi, (dil, m, seg, kbase, qbase) in enumerate(layout):
        nchunk = seg // ATT_QB
        for r in range(dil):
            o = kbase + r * seg
            if dil == 1:
                for c in range(nchunk):
                    lo = c * ATT_QB - ATT_SIDE
                    if c == 0:
                        kc = jnp.concatenate([kp_ref[ATT_HALO - ATT_SIDE:, :], km_ref[0:ATT_QB - ATT_SIDE, :]], axis=0)
                        vc = jnp.concatenate([vp_ref[ATT_HALO - ATT_SIDE:, :], vm_ref[0:ATT_QB - ATT_SIDE, :]], axis=0)
                    elif c == nchunk - 1:
                        kc = jnp.concatenate([km_ref[lo:, :], kn_ref[0:ATT_SIDE, :]], axis=0)
                        vc = jnp.concatenate([vm_ref[lo:, :], vn_ref[0:ATT_SIDE, :]], axis=0)
                    else:
                        kc = km_ref[lo:lo + ATT_QB, :]
                        vc = vm_ref[lo:lo + ATT_QB, :]
                    kts[(o + c * ATT_QB) // ATT_QB] = kc.T.astype(BF16)
                    vs[o + c * ATT_QB:o + (c + 1) * ATT_QB, 0:LANES] = vc.astype(BF16)
                qs[qbase:qbase + m, :] = q_ref[...].astype(BF16)
            else:
                r4, s = r % mid, r // mid
                step = dil // mid
                first = r4 * win4 + (halo4 - ATT_SIDE * step) + s
                for c in range(nchunk):
                    key_rows(k4, v4, first + c * ATT_QB * step, step, o + c * ATT_QB)
                qs[qbase + r * m:qbase + (r + 1) * m, :] = _strided(q4, r4 * main4 + s, m, step).astype(BF16)

    lane = lax.broadcasted_iota(jnp.int32, (ATT_QB, LANES), 1)
    head0 = lane < HEAD_DIM
    col = lax.broadcasted_iota(jnp.int32, (1, 2 * ATT_QB), 1)
    cells = ATT_TILE // ATT_QB
    total = len(layout) * cells
    blk_bits = [(m // ATT_QB).bit_length() - 1 for _, m, _, _, _ in layout]
    seg_chunks = [seg // ATT_QB for _, _, seg, _, _ in layout]
    base_chunks = [kbase // ATT_QB for _, _, _, kbase, _ in layout]

    def cell_params(g):
        bi = g // cells
        idx = g % cells
        if isinstance(g, int):
            pick = lambda vals: vals[bi]
        else:
            pick = lambda vals: jnp.where(bi == 0, vals[0], jnp.where(bi == 1, vals[1], vals[2]))
        bits = pick(blk_bits)
        last_blk = (1 << bits) - 1
        blk = idx & last_blk
        return bi, blk, last_blk, pick(base_chunks) + (idx >> bits) * pick(seg_chunks) + blk

    def stage_logits(g, slot):
        bi, blk, last_blk, kchunk = cell_params(g)
        left_bad = jnp.logical_and(blk == 0, jnp.logical_not(left_ok))
        right_bad = jnp.logical_and(blk == last_blk, jnp.logical_not(right_ok))
        edge = (jnp.where(jnp.logical_and(col < ATT_SIDE, left_bad), NEG_INF, 0.0)
                + jnp.where(jnp.logical_and(col >= 2 * ATT_QB - ATT_SIDE, right_bad), NEG_INF, 0.0))
        qc = qs[pl.ds(pl.multiple_of(g * ATT_QB, ATT_QB), ATT_QB), :]
        zero = jnp.zeros_like(qc)
        lhs = jnp.concatenate([jnp.where(head0, qc, zero), jnp.where(head0, zero, qc)], axis=0)
        kt = jnp.concatenate([kts[kchunk], kts[kchunk + 1]], axis=1)
        s_scr[slot] = _dot(lhs, kt) + bias_ref[bi] + edge

    def stage_softmax(slot):
        part = ATT_QB // 2
        for c in range(2 * ATT_QB // part):
            s = s_scr[slot, c * part:(c + 1) * part, :]
            mx = jnp.max(s, axis=-1, keepdims=True)
            p_scr[slot, c * part:(c + 1) * part, :] = jnp.exp(s - mx).astype(BF16)
            m_scr[slot, c * part:(c + 1) * part, :] = jnp.broadcast_to(mx, (part, LANES))

    def stage_values(g, slot):
        _, _, _, kchunk = cell_params(g)
        koff = pl.multiple_of(kchunk * ATT_QB, ATT_QB)
        r = _dot(p_scr[slot], vs[pl.ds(koff, 2 * ATT_QB), :])
        num = jnp.where(head0, r[0:ATT_QB, 0:LANES], r[ATT_QB:, 0:LANES])
        den = jnp.where(head0, r[0:ATT_QB, LANES:], r[ATT_QB:, LANES:])
        mx = jnp.where(head0, m_scr[slot, 0:ATT_QB, :], m_scr[slot, ATT_QB:, :])
        rows = pl.ds(pl.multiple_of(g * ATT_QB, ATT_QB), ATT_QB)
        o_scr[rows, :] = num / den
        l_scr[rows, :] = mx + jnp.log(den)

    width = ATT_PIPE

    def group(c, do_values, do_softmax, do_logits):
        if do_values:
            for u in range(width):
                stage_values(c - width + u, u)
        if do_softmax:
            for u in range(width):
                stage_softmax(u)
        if do_logits:
            for u in range(width):
                stage_logits(c + width + u, u)

    group(-width, False, False, True)
    group(0, False, True, True)

    def trip(t, carry):
        group((t + 1) * width, True, True, True)
        return carry

    lax.fori_loop(0, total // width - 2, trip, 0)
    group(total - width, True, True, False)
    group(total, True, False, False)

    big = WINDOWS[-1][1]
    rows = ATT_TILE // big
    for r in range(big):
        os_, ls_ = [], []
        for dil, m, seg, kbase, qbase in layout:
            start = qbase + (r % dil) * m + r // dil
            os_.append(_strided(o_scr, start, rows, big // dil))
            ls_.append(_strided(l_scr, start, rows, big // dil))
        mx = jnp.maximum(jnp.maximum(ls_[0], ls_[1]), ls_[2])
        ws = [jnp.exp(l - mx) for l in ls_]
        num = ws[0] * os_[0] + ws[1] * os_[1] + ws[2] * os_[2]
        o_ref[pl.ds(r, rows, stride=big), :] = num / (ws[0] + ws[1] + ws[2])


def _attention(q, k, v, rel_bias, tp, seq_p, seq_s):
    t, attn_w = q.shape
    n_heads = attn_w // HEAD_DIM
    pair = LANES // HEAD_DIM
    assert seq_p % ATT_TILE == 0 and seq_s % ATT_TILE == 0 and ATT_TILE == 2 * ATT_HALO
    assert ATT_TILE // WINDOWS[-1][1] == ATT_QB and pair == 2
    layout, krows = _branch_layout()

    ii = np.arange(ATT_QB)[:, None]
    jj = np.arange(2 * ATT_QB)[None, :]
    rel = jj - ATT_SIDE - ii
    band = np.abs(rel) <= ATT_SIDE
    buckets = np.stack([_t5_bucket(rel * dil) for _, dil in WINDOWS])
    bucket_of = jnp.asarray(buckets)[None]
    bias = jnp.zeros((n_heads,) + buckets.shape, F32)
    for b in range(N_BUCKETS):
        bias = jnp.where(bucket_of == b, rel_bias[b].astype(F32)[:, None, None, None], bias)
    bias = jnp.where(jnp.asarray(band)[None, None], bias, NEG_INF)
    nb = len(WINDOWS)
    bias = bias.reshape(n_heads // pair, pair, nb, ATT_QB, 2 * ATT_QB).transpose(0, 2, 1, 3, 4)
    bias = bias.reshape(n_heads // pair, nb, pair * ATT_QB, 2 * ATT_QB)

    halo_blocks = t // ATT_HALO
    per_tile = ATT_TILE // ATT_HALO
    mid = WINDOWS[1][1]
    main = pl.BlockSpec((ATT_TILE, LANES), lambda i, j: (i, j))
    prev = pl.BlockSpec((ATT_HALO, LANES), lambda i, j: (jnp.maximum(i * per_tile - 1, 0), j))
    nxt = pl.BlockSpec((ATT_HALO, LANES), lambda i, j: (jnp.minimum((i + 1) * per_tile, halo_blocks - 1), j))
    n_q = nb * ATT_TILE
    return pl.pallas_call(
        functools.partial(_attn_kernel, tp // ATT_TILE, seq_p // ATT_TILE, seq_s // ATT_TILE),
        grid=(t // ATT_TILE, attn_w // LANES),
        in_specs=[main, main, prev, nxt, main, prev, nxt,
                  pl.BlockSpec((None, nb, pair * ATT_QB, 2 * ATT_QB), lambda i, j: (j, 0, 0, 0))],
        out_specs=main,
        out_shape=jax.ShapeDtypeStruct((t, attn_w), F32),
        scratch_shapes=[pltpu.VMEM((n_q, LANES), BF16),
                        pltpu.VMEM((krows // ATT_QB, LANES, ATT_QB), BF16),
                        pltpu.VMEM((krows, 2 * LANES), BF16),
                        pltpu.VMEM((n_q, LANES), F32), pltpu.VMEM((n_q, LANES), F32),
                        pltpu.VMEM((ATT_PIPE, pair * ATT_QB, 2 * ATT_QB), F32),
                        pltpu.VMEM((ATT_PIPE, pair * ATT_QB, 2 * ATT_QB), BF16),
                        pltpu.VMEM((ATT_PIPE, pair * ATT_QB, LANES), F32),
                        pltpu.VMEM(((ATT_TILE + 2 * ATT_HALO), LANES), F32),
                        pltpu.VMEM(((ATT_TILE + 2 * ATT_HALO), LANES), F32),
                        pltpu.VMEM((ATT_TILE, LANES), F32)],
        compiler_params=pltpu.CompilerParams(dimension_semantics=("arbitrary", "arbitrary"),
                                             vmem_limit_bytes=VMEM_LIMIT),
        name="attn",
    )(q, k, k, k, v, v, v, bias)


def _outproj_kernel(sgu_w, n_exp, a_ref, attn_ref, x_ref, onb_ref, wo_ref, norm2_ref, wrh_ref, wrl_ref,
                    br_ref, tri_ref, x1_ref, h2_ref, meta_ref, cnt_ref, route_ref, carry):
    i = pl.program_id(0)

    @pl.when(i == 0)
    def _():
        carry[...] = jnp.zeros_like(carry)

    bn = _rms(attn_ref[...], onb_ref[...]).astype(BF16)
    x1 = x_ref[...] + _dot(a_ref[...], wo_ref[0:sgu_w, :]) + _dot(bn, wo_ref[sgu_w:, :])
    x1_ref[...] = x1
    h2 = _rms(x1, norm2_ref[...])
    h2_ref[...] = h2.astype(BF16)
    hi, lo = _split_bf16(h2)
    logits = _dot(hi, wrh_ref[...]) + _dot(lo, wrh_ref[...]) + _dot(hi, wrl_ref[...]) + br_ref[...]

    lane = lax.broadcasted_iota(jnp.int32, logits.shape, 1)
    work = jnp.where(lane < n_exp, logits, -jnp.inf)
    chosen = jnp.zeros(logits.shape, jnp.bool_)
    experts, values = [], []
    for _ in range(TOP_K):
        top = jnp.max(work, axis=-1, keepdims=True)
        idx = jnp.min(jnp.where(work == top, lane, LANES), axis=-1, keepdims=True)
        hit = lane == idx
        chosen = jnp.logical_or(chosen, hit)
        work = jnp.where(hit, -jnp.inf, work)
        experts.append(idx)
        values.append(top)
    exps = [jnp.exp(v - values[0]) for v in values]
    den = exps[0] + exps[1] + exps[2] + exps[3]

    onehot = jnp.where(chosen, 1.0, 0.0)
    before = _dot(tri_ref[...], onehot.astype(BF16)) + carry[0:1, :]
    carry[0:1, :] = carry[0:1, :] + jnp.sum(onehot, axis=0, keepdims=True)
    cnt_ref[...] = carry[...]

    meta = jnp.zeros(logits.shape, F32)
    for kk in range(TOP_K):
        rank = jnp.sum(jnp.where(lane == experts[kk], before, 0.0), axis=-1, keepdims=True)
        meta = jnp.where(lane == kk, experts[kk].astype(F32), meta)
        meta = jnp.where(lane == TOP_K + kk, exps[kk] / den, meta)
        meta = jnp.where(lane == 2 * TOP_K + kk, rank, meta)
    meta_ref[...] = meta
    route_ref[...] = meta.T[0:route_ref.shape[0], :]


def _outproj(a_n, attn, x, row0, out_norm_b, w_o, norm2, w_router, b_router):
    sgu_w = a_n.shape[1]
    attn_w = attn.shape[1]
    t, d_model = x.shape
    n_exp = w_router.shape[1]
    assert n_exp <= LANES and row0 % TOK_TILE == 0 and t % TOK_TILE == 0
    b0 = row0 // TOK_TILE
    wr = jnp.pad(w_router, ((0, 0), (0, LANES - n_exp)))
    wr_hi = wr.astype(BF16)
    wr_lo = (wr - wr_hi.astype(F32)).astype(BF16)
    br = jnp.pad(b_router, (0, LANES - n_exp))[None]
    tri = np.tril(np.ones((TOK_TILE, TOK_TILE), np.float32), -1)

    const = lambda shape: pl.BlockSpec(shape, lambda i: (0,) * len(shape))
    tok = lambda w: pl.BlockSpec((TOK_TILE, w), lambda i: (i, 0))
    src = lambda w: pl.BlockSpec((TOK_TILE, w), lambda i: (i + b0, 0))
    return pl.pallas_call(
        functools.partial(_outproj_kernel, sgu_w, n_exp),
        grid=(t // TOK_TILE,),
        in_specs=[
            src(sgu_w), src(attn_w), tok(d_model),
            const((1, attn_w)), const(w_o.shape), const((1, d_model)), const((d_model, LANES)),
            const((d_model, LANES)), const((1, LANES)), const((TOK_TILE, TOK_TILE)),
        ],
        out_specs=[tok(d_model), src(d_model), tok(LANES), const((8, LANES)),
                   pl.BlockSpec((4 * TOP_K, TOK_TILE), lambda i: (0, i))],
        out_shape=[jax.ShapeDtypeStruct((t, d_model), F32), jax.ShapeDtypeStruct((a_n.shape[0], d_model), BF16),
                   jax.ShapeDtypeStruct((t, LANES), F32), jax.ShapeDtypeStruct((8, LANES), F32),
                   jax.ShapeDtypeStruct((4 * TOP_K, t), F32)],
        scratch_shapes=[pltpu.VMEM((8, LANES), F32)],
        compiler_params=pltpu.CompilerParams(dimension_semantics=("arbitrary",), vmem_limit_bytes=VMEM_LIMIT),
        name="outproj",
    )(a_n, attn, x, out_norm_b[None], w_o.astype(BF16), norm2[None], wr_hi, wr_lo, br, jnp.asarray(tri, BF16))


def _moe_kernel(d_exp, blk_ref, exp_ref, lo_ref, hi_ref, x_ref, wgu_ref, bgu_ref, wd_ref, bd_ref, y_ref,
                wgu_s, wd_s, act_s):
    it = pl.program_id(0)
    prev = jnp.maximum(it - 1, 0)
    new_expert = jnp.logical_or(it == 0, exp_ref[it] != exp_ref[prev])
    new_block = jnp.logical_or(it == 0, blk_ref[it] != blk_ref[prev])
    lo, hi = lo_ref[it], hi_ref[it]
    active = hi > lo

    @pl.when(jnp.logical_and(new_expert, active))
    def _():
        wgu_s[...] = wgu_ref[...].astype(BF16)
        wd_s[...] = wd_ref[...].astype(BF16)

    @pl.when(active)
    def _():
        x = x_ref[...]
        step = 512
        for n in range(0, d_exp, step):
            gate = _dot(x, wgu_s[:, n:n + step]) + bgu_ref[:, n:n + step]
            up = _dot(x, wgu_s[:, d_exp + n:d_exp + n + step]) + bgu_ref[:, d_exp + n:d_exp + n + step]
            gate = jnp.minimum(gate, SWIGLU_LIMIT)
            up = jnp.clip(up, -SWIGLU_LIMIT, SWIGLU_LIMIT)
            glu = gate * jax.nn.sigmoid(SWIGLU_ALPHA * gate)
            act_s[:, n:n + step] = ((up + 1.0) * glu).astype(BF16)
        y = (_dot(act_s[...], wd_s[...]) + bd_ref[...]).astype(y_ref.dtype)
        row = lax.broadcasted_iota(jnp.int32, (MOE_ROWS, 1), 0)
        mine = jnp.logical_and(row >= lo, row < hi)

        @pl.when(new_block)
        def _():
            y_ref[...] = jnp.where(mine, y, jnp.zeros_like(y))

        @pl.when(jnp.logical_not(new_block))
        def _():
            y_ref[...] = jnp.where(mine, y, y_ref[...])


def _moe_experts(x_rows, items, w_gu, b_gu, w_down, b_down):
    n_rows, d_model = x_rows.shape
    n_exp, _, two_de = w_gu.shape
    d_exp = two_de // 2
    n_items = items[0].shape[0]
    grid_spec = pltpu.PrefetchScalarGridSpec(
        num_scalar_prefetch=4,
        grid=(n_items,),
        in_specs=[
            pl.BlockSpec((MOE_ROWS, d_model), lambda i, blk, ex, lo, hi: (blk[i], 0)),
            pl.BlockSpec((None, d_model, two_de), lambda i, blk, ex, lo, hi: (ex[i], 0, 0)),
            pl.BlockSpec((None, 1, two_de), lambda i, blk, ex, lo, hi: (ex[i], 0, 0)),
            pl.BlockSpec((None, d_exp, d_model), lambda i, blk, ex, lo, hi: (ex[i], 0, 0)),
            pl.BlockSpec((None, 1, d_model), lambda i, blk, ex, lo, hi: (ex[i], 0, 0)),
        ],
        out_specs=pl.BlockSpec((MOE_ROWS, d_model), lambda i, blk, ex, lo, hi: (blk[i], 0)),
        scratch_shapes=[pltpu.VMEM((d_model, two_de), BF16), pltpu.VMEM((d_exp, d_model), BF16),
                        pltpu.VMEM((MOE_ROWS, d_exp), BF16)],
    )
    return pl.pallas_call(
        functools.partial(_moe_kernel, d_exp),
        grid_spec=grid_spec,
        out_shape=jax.ShapeDtypeStruct((n_rows, d_model), BF16),
        compiler_params=pltpu.CompilerParams(dimension_semantics=("arbitrary",), vmem_limit_bytes=VMEM_LIMIT),
        name="moe",
    )(*items, x_rows, w_gu, b_gu[:, None, :], w_down, b_down[:, None, :])


def _route(meta, counts_f, n_exp):
    t = meta.shape[1]
    n_assign = t * TOP_K
    assert n_assign % MOE_ROWS == 0
    top_e = meta[0:TOP_K].astype(jnp.int32)
    rank = meta[2 * TOP_K:3 * TOP_K].astype(jnp.int32)
    counts = counts_f[0, :n_exp].astype(jnp.int32)
    end = jnp.cumsum(counts)
    start = end - counts
    experts = jnp.arange(n_exp, dtype=jnp.int32)
    start_of = jnp.zeros_like(top_e)
    for e in range(n_exp):
        start_of = jnp.where(top_e == e, start[e], start_of)
    pos = (start_of + rank).reshape(-1)

    shift = (n_assign - 1).bit_length()
    assert (n_exp << shift) < 2 ** 31
    assign = jnp.arange(t, dtype=jnp.int32)[None, :] * TOP_K + jnp.arange(TOP_K, dtype=jnp.int32)[:, None]
    keys = ((top_e << shift) + assign).reshape(-1)
    sorted_tok = (jnp.sort(keys) & ((1 << shift) - 1)) // TOP_K

    n_blocks = n_assign // MOE_ROWS
    n_items = n_blocks + n_exp
    first_blk = start // MOE_ROWS
    n_it = jnp.where(counts > 0, (end - 1) // MOE_ROWS - first_blk + 1, 0)
    it_end = jnp.cumsum(n_it)
    it_start = it_end - n_it
    i = jnp.arange(n_items, dtype=jnp.int32)
    valid = i < it_end[-1]
    e_i = jnp.sum(jnp.minimum(i, it_end[-1] - 1)[:, None] >= it_end[None, :], axis=1).astype(jnp.int32)
    e_i = jnp.minimum(e_i, n_exp - 1)
    pick = lambda table: jnp.sum(jnp.where(e_i[:, None] == experts[None, :], table[None, :], 0), axis=1)
    blk_i = jnp.where(valid, pick(first_blk) + i - pick(it_start), n_blocks - 1)
    lo_i = jnp.where(valid, jnp.clip(pick(start) - blk_i * MOE_ROWS, 0, MOE_ROWS), 0)
    hi_i = jnp.where(valid, jnp.clip(pick(end) - blk_i * MOE_ROWS, 0, MOE_ROWS), 0)
    items = tuple(a.astype(jnp.int32) for a in (blk_i, e_i, lo_i, hi_i))
    return pos, sorted_tok, items


def _combine_kernel(x1_ref, yg_ref, meta_ref, o_ref):
    acc = x1_ref[...]
    for kk in range(TOP_K):
        gate = meta_ref[:, TOP_K + kk:TOP_K + kk + 1]
        acc = acc + gate * yg_ref[kk].astype(F32)
    o_ref[...] = acc


def _combine(x1, yg, meta):
    rows, d_model = x1.shape
    tok = lambda w: pl.BlockSpec((TOK_TILE, w), lambda i: (i, 0))
    return pl.pallas_call(
        _combine_kernel,
        grid=(rows // TOK_TILE,),
        in_specs=[tok(d_model), pl.BlockSpec((TOP_K, TOK_TILE, d_model), lambda i: (0, i, 0)), tok(LANES)],
        out_specs=tok(d_model),
        out_shape=jax.ShapeDtypeStruct((rows, d_model), F32),
        compiler_params=pltpu.CompilerParams(dimension_semantics=("arbitrary",), vmem_limit_bytes=VMEM_LIMIT),
        name="combine",
    )(x1, yg, meta)


def kernel(x_prompt, x_sample, norm1, w_in, q_gain, k_gain, rel_bias, sgu_norm, w_s, b_s, out_norm_a, out_norm_b,
           w_o, norm2, w_router, b_router, w_gu, b_gu, w_down, b_down):
    assert norm1.shape[0] == 1, "single-layer trunk"
    bp, sp, d_model = x_prompt.shape
    bs, ss, _ = x_sample.shape
    xp = x_prompt.reshape(bp * sp, d_model)
    xs = x_sample.reshape(bs * ss, d_model)
    tp, t = bp * sp, bp * sp + bs * ss

    q, k, v, a_n = _inproj(xp, xs, norm1[0], w_in[0], q_gain[0], k_gain[0], sgu_norm[0], w_s[0], b_s[0],
                           out_norm_a[0])
    attn = _attention(q, k, v, rel_bias, tp, sp, ss)

    outs = []
    for x, row0 in ((xp, 0), (xs, tp)):
        rows = x.shape[0]
        x1, h2, meta, counts, route = _outproj(a_n, attn, x, row0, out_norm_b[0], w_o[0], norm2[0], w_router[0],
                                               b_router[0])
        pos, sorted_tok, items = _route(route, counts, w_router.shape[-1])
        x_rows = h2.at[sorted_tok + row0].get(mode="promise_in_bounds")
        y_rows = _moe_experts(x_rows, items, w_gu[0], b_gu[0], w_down[0], b_down[0])
        yg = y_rows.at[pos].get(mode="promise_in_bounds").reshape(TOP_K, rows, d_model)
        outs.append(_combine(x1, yg, meta))
    return (outs[0].reshape(bp, sp, d_model), outs[1].reshape(bs, ss, d_model))
```

```python
import functools
import math

import numpy as np
import jax
import jax.numpy as jnp
from jax import lax
from jax.experimental import pallas as pl
from jax.experimental.pallas import tpu as pltpu

F32 = jnp.float32
BF16 = jnp.bfloat16

EPS = 1e-6
NEG_INF = -1e30
HEAD_DIM = 64
SGU_CHUNK = 128
WINDOWS = ((128, 1), (512, 4), (2048, 16))
N_BUCKETS = 32
MAX_DISTANCE = 1024
TOP_K = 4
SWIGLU_LIMIT = 7.0
SWIGLU_ALPHA = 1.702

LANES = 128
TOK_TILE = 512
ATT_TILE = 2048
ATT_HALO = 1024
ATT_QB = 128
ATT_SIDE = 64
ATT_PIPE = 8
MOE_ROWS = 512
VMEM_LIMIT = 56 * 1024 * 1024


def _dot(a, b):
    return jnp.dot(a, b, preferred_element_type=F32)


def _dot_nt(a, b):
    return lax.dot_general(a, b, (((1,), (1,)), ((), ())), preferred_element_type=F32)


def _rms(x, g):
    ms = jnp.mean(x * x, axis=-1, keepdims=True)
    return x * lax.rsqrt(ms + EPS) * g


def _gelu(x):
    return 0.5 * x * (1.0 + lax.erf(x * (1.0 / math.sqrt(2.0))))


def _split_bf16(x):
    hi = x.astype(BF16)
    lo = (x - hi.astype(F32)).astype(BF16)
    return hi, lo


def _inproj_kernel(n1, attn_w, sgu_w, xp_ref, xs_ref, norm1_ref, win_ref, gsum_ref, gexp_ref,
                   qkg_ref, sgun_ref, wsp_ref, bsb_ref, ona_ref, q_ref, k_ref, v_ref, a_ref):
    i = pl.program_id(0)
    x = jnp.where(i < n1, xp_ref[...], xs_ref[...])
    h = _rms(x, norm1_ref[...]).astype(BF16)

    zqk = _dot(h, win_ref[:, 0:2 * attn_w])
    sq_hi, sq_lo = _split_bf16(zqk * zqk)
    ss = _dot(sq_hi, gsum_ref[...]) + _dot(sq_lo, gsum_ref[...])
    inv = lax.rsqrt(ss * (1.0 / HEAD_DIM) + EPS)
    inv_hi, inv_lo = _split_bf16(inv)
    invb = _dot(inv_hi, gexp_ref[...]) + _dot(inv_lo, gexp_ref[...])
    qk = zqk * invb * qkg_ref[...]
    q_ref[...] = qk[:, 0:attn_w]
    k_ref[...] = qk[:, attn_w:2 * attn_w]
    v_ref[...] = _dot(h, win_ref[:, 2 * attn_w:3 * attn_w])

    c0 = 3 * attn_w
    u = _gelu(_dot(h, win_ref[:, c0:c0 + sgu_w]))
    gv = _gelu(_dot(h, win_ref[:, c0 + sgu_w:c0 + 2 * sgu_w]))
    vsn = _rms(gv, sgun_ref[...]).astype(BF16)
    lane = lax.broadcasted_iota(jnp.int32, (SGU_CHUNK, LANES), 1)
    lo_half = lane < HEAD_DIM
    zero = jnp.zeros((SGU_CHUNK, LANES), BF16)
    for c in range(TOK_TILE // SGU_CHUNK):
        r0 = c * SGU_CHUNK
        parts = []
        for j in range(sgu_w // LANES):
            blk = vsn[r0:r0 + SGU_CHUNK, j * LANES:(j + 1) * LANES]
            rhs = jnp.concatenate([jnp.where(lo_half, blk, zero), jnp.where(lo_half, zero, blk)], axis=0)
            parts.append(_dot(wsp_ref[j], rhs))
        s = jnp.concatenate(parts, axis=1) + bsb_ref[...]
        a = u[r0:r0 + SGU_CHUNK, :] * s
        a_ref[r0:r0 + SGU_CHUNK, :] = _rms(a, ona_ref[...]).astype(BF16)


def _inproj(xp, xs, norm1, w_in, q_gain, k_gain, sgu_norm, w_s, b_s, out_norm_a):
    tp, d_model = xp.shape
    ts = xs.shape[0]
    t = tp + ts
    n_heads_w = q_gain.shape[0]
    sgu_w = sgu_norm.shape[0]
    attn_w = (w_in.shape[1] - 2 * sgu_w) // 3
    n_heads = attn_w // n_heads_w
    n_groups = w_s.shape[0]
    assert n_heads_w == HEAD_DIM and sgu_w // n_groups == HEAD_DIM and w_s.shape[1] == SGU_CHUNK
    assert tp % TOK_TILE == 0 and ts % TOK_TILE == 0 and 2 * n_heads <= LANES
    n1 = tp // TOK_TILE

    heads = np.arange(2 * attn_w) // HEAD_DIM
    gsum = (heads[:, None] == np.arange(LANES)[None, :]).astype(np.float32)
    gexp = gsum.T
    qkg = jnp.concatenate([jnp.tile(q_gain, n_heads) * (HEAD_DIM ** -0.5), jnp.tile(k_gain, n_heads)])
    wsp = jnp.concatenate([w_s[0::2], w_s[1::2]], axis=2).astype(BF16)
    bsb = jnp.repeat(b_s.T, HEAD_DIM, axis=1)

    const = lambda shape: pl.BlockSpec(shape, lambda i: (0,) * len(shape))
    tok = lambda w: pl.BlockSpec((TOK_TILE, w), lambda i: (i, 0))
    return pl.pallas_call(
        functools.partial(_inproj_kernel, n1, attn_w, sgu_w),
        grid=(t // TOK_TILE,),
        in_specs=[
            pl.BlockSpec((TOK_TILE, d_model), lambda i: (jnp.minimum(i, n1 - 1), 0)),
            pl.BlockSpec((TOK_TILE, d_model), lambda i: (jnp.maximum(i - n1, 0), 0)),
            const((1, d_model)), const(w_in.shape), const((2 * attn_w, LANES)), const((LANES, 2 * attn_w)),
            const((1, 2 * attn_w)), const((1, sgu_w)), const(wsp.shape), const((SGU_CHUNK, sgu_w)),
            const((1, sgu_w)),
        ],
        out_specs=[tok(attn_w), tok(attn_w), tok(attn_w), tok(sgu_w)],
        out_shape=[jax.ShapeDtypeStruct((t, attn_w), F32)] * 3 + [jax.ShapeDtypeStruct((t, sgu_w), BF16)],
        compiler_params=pltpu.CompilerParams(dimension_semantics=("arbitrary",), vmem_limit_bytes=VMEM_LIMIT),
        name="inproj",
    )(xp, xs, norm1[None], w_in.astype(BF16), jnp.asarray(gsum, BF16), jnp.asarray(gexp, BF16),
      qkg[None], sgu_norm[None], wsp, bsb, out_norm_a[None])


def _t5_bucket(rel):
    nb = N_BUCKETS // 2
    bucket = (rel > 0).astype(np.int32) * nb
    n = np.abs(rel)
    max_exact = nb // 2
    large = max_exact + (np.log(np.maximum(n, 1) / max_exact)
                         / np.log(MAX_DISTANCE / max_exact) * (nb - max_exact)).astype(np.int32)
    large = np.minimum(large, nb - 1)
    return (bucket + np.where(n < max_exact, n, large)).astype(np.int32)


def _branch_layout():
    out, kbase = [], 0
    for bi, (window, dil) in enumerate(WINDOWS):
        assert window // (2 * dil) == ATT_SIDE
        m = ATT_TILE // dil
        seg = m + 2 * ATT_SIDE
        out.append((dil, m, seg, kbase, bi * ATT_TILE))
        kbase += dil * seg
    return out, kbase


def _strided(ref, start, size, stride):
    if stride == 1:
        return ref[start:start + size, :]
    return ref[pl.ds(start, size, stride=stride), :]


def _attn_kernel(tiles_p, per_seq_p, per_seq_s, q_ref, km_ref, kp_ref, kn_ref, vm_ref, vp_ref, vn_ref,
                 bias_ref, o_ref, qs, kts, vs, o_scr, d_scr, l_scr, s_scr, p_scr, m_scr, k4, v4, q4):
    i = pl.program_id(0)
    w = jnp.where(i < tiles_p, i % per_seq_p, (i - tiles_p) % per_seq_s)
    last = jnp.where(i < tiles_p, per_seq_p - 1, per_seq_s - 1)
    left_ok = w > 0
    right_ok = w < last
    layout, _ = _branch_layout()
    mid = WINDOWS[1][1]
    assert WINDOWS[2][1] == mid * mid
    halo4, main4 = ATT_HALO // mid, ATT_TILE // mid
    win4 = 2 * halo4 + main4

    @pl.when(jnp.logical_and(i == 0, pl.program_id(1) == 0))
    def _():
        vs[:, LANES:] = jnp.ones((vs.shape[0], LANES), BF16)

    for r in range(mid):
        for dst, main, prev, nxt in ((k4, km_ref, kp_ref, kn_ref), (v4, vm_ref, vp_ref, vn_ref)):
            dst[r * win4:r * win4 + halo4, :] = _strided(prev, r, halo4, mid)
            dst[r * win4 + halo4:r * win4 + halo4 + main4, :] = _strided(main, r, main4, mid)
            dst[r * win4 + halo4 + main4:(r + 1) * win4, :] = _strided(nxt, r, halo4, mid)
        q4[r * main4:(r + 1) * main4, :] = _strided(q_ref, r, main4, mid)

    def key_rows(src_k, src_v, start, stride, kbase_rows):
        kc = _strided(src_k, start, ATT_QB, stride)
        kts[kbase_rows // ATT_QB] = kc.T.astype(BF16)
        vs[kbase_rows:kbase_rows + ATT_QB, 0:LANES] = _strided(src_v, start, ATT_QB, stride).astype(BF16)

    for bi, (dil, m, seg, kbase, qbase) in enumerate(layout):
        nchunk = seg // ATT_QB
        for r in range(dil):
            o = kbase + r * seg
            if dil == 1:
                for c in range(nchunk):
                    lo = c * ATT_QB - ATT_SIDE
                    if c == 0:
                        kc = jnp.concatenate([kp_ref[ATT_HALO - ATT_SIDE:, :], km_ref[0:ATT_QB - ATT_SIDE, :]], axis=0)
                        vc = jnp.concatenate([vp_ref[ATT_HALO - ATT_SIDE:, :], vm_ref[0:ATT_QB - ATT_SIDE, :]], axis=0)
                    elif c == nchunk - 1:
                        kc = jnp.concatenate([km_ref[lo:, :], kn_ref[0:ATT_SIDE, :]], axis=0)
                        vc = jnp.concatenate([vm_ref[lo:, :], vn_ref[0:ATT_SIDE, :]], axis=0)
                    else:
                        kc = km_ref[lo:lo + ATT_QB, :]
                        vc = vm_ref[lo:lo + ATT_QB, :]
                    kts[(o + c * ATT_QB) // ATT_QB] = kc.T.astype(BF16)
                    vs[o + c * ATT_QB:o + (c + 1) * ATT_QB, 0:LANES] = vc.astype(BF16)
                qs[qbase:qbase + m, :] = q_ref[...].astype(BF16)
            else:
                r4, s = r % mid, r // mid
                step = dil // mid
                first = r4 * win4 + (halo4 - ATT_SIDE * step) + s
                for c in range(nchunk):
                    key_rows(k4, v4, first + c * ATT_QB * step, step, o + c * ATT_QB)
                qs[qbase + r * m:qbase + (r + 1) * m, :] = _strided(q4, r4 * main4 + s, m, step).astype(BF16)

    lane = lax.broadcasted_iota(jnp.int32, (ATT_QB, LANES), 1)
    head0 = lane < HEAD_DIM
    cells = ATT_TILE // ATT_QB
    total = len(layout) * cells
    blk_bits = [(m // ATT_QB).bit_length() - 1 for _, m, _, _, _ in layout]
    seg_chunks = [seg // ATT_QB for _, _, seg, _, _ in layout]
    base_chunks = [kbase // ATT_QB for _, _, _, kbase, _ in layout]

    def cell_params(g):
        bi = g // cells
        idx = g % cells
        if isinstance(g, int):
            pick = lambda vals: vals[bi]
        else:
            pick = lambda vals: jnp.where(bi == 0, vals[0], jnp.where(bi == 1, vals[1], vals[2]))
        bits = pick(blk_bits)
        last_blk = (1 << bits) - 1
        blk = idx & last_blk
        return bi, blk, last_blk, pick(base_chunks) + (idx >> bits) * pick(seg_chunks) + blk

    def stage_logits(g, slot):
        bi, blk, last_blk, kchunk = cell_params(g)
        left_bad = jnp.logical_and(blk == 0, jnp.logical_not(left_ok))
        right_bad = jnp.logical_and(blk == last_blk, jnp.logical_not(right_ok))
        variant = left_bad.astype(jnp.int32) + 2 * right_bad.astype(jnp.int32)
        qc = qs[pl.ds(pl.multiple_of(g * ATT_QB, ATT_QB), ATT_QB), :]
        zero = jnp.zeros_like(qc)
        lhs = jnp.concatenate([jnp.where(head0, qc, zero), jnp.where(head0, zero, qc)], axis=0)
        kt = jnp.concatenate([kts[kchunk], kts[kchunk + 1]], axis=1)
        s_scr[slot] = _dot(lhs, kt) + bias_ref[bi, variant]

    def stage_softmax(slot):
        part = ATT_QB // 2
        for c in range(2 * ATT_QB // part):
            s = s_scr[slot, c * part:(c + 1) * part, :]
            mx = jnp.max(s, axis=-1, keepdims=True)
            p_scr[slot, c * part:(c + 1) * part, :] = jnp.exp(s - mx).astype(BF16)
            m_scr[slot, c * part:(c + 1) * part, :] = jnp.broadcast_to(mx, (part, LANES))

    def stage_values(g, slot):
        _, _, _, kchunk = cell_params(g)
        koff = pl.multiple_of(kchunk * ATT_QB, ATT_QB)
        r = _dot(p_scr[slot], vs[pl.ds(koff, 2 * ATT_QB), :])
        num = jnp.where(head0, r[0:ATT_QB, 0:LANES], r[ATT_QB:, 0:LANES])
        den = jnp.where(head0, r[0:ATT_QB, LANES:], r[ATT_QB:, LANES:])
        mx = jnp.where(head0, m_scr[slot, 0:ATT_QB, :], m_scr[slot, ATT_QB:, :])
        rows = pl.ds(pl.multiple_of(g * ATT_QB, ATT_QB), ATT_QB)
        o_scr[rows, :] = num
        d_scr[rows, :] = den
        l_scr[rows, :] = mx

    width = ATT_PIPE

    def group(c, do_values, do_softmax, do_logits):
        if do_values:
            for u in range(width):
                stage_values(c - width + u, u)
        if do_softmax:
            for u in range(width):
                stage_softmax(u)
        if do_logits:
            for u in range(width):
                stage_logits(c + width + u, u)

    group(-width, False, False, True)
    group(0, False, True, True)

    def trip(t, carry):
        group((t + 1) * width, True, True, True)
        return carry

    lax.fori_loop(0, total // width - 2, trip, 0)
    group(total - width, True, True, False)
    group(total, True, False, False)

    big = WINDOWS[-1][1]
    rows = ATT_TILE // big
    dil0, m0, _, _, qb0 = layout[0]
    assert dil0 == 1 and big == mid * mid
    for r4 in range(mid):
        k4[r4 * main4:(r4 + 1) * main4, :] = _strided(o_scr, qb0 + r4, main4, mid)
        k4[m0 + r4 * main4:m0 + (r4 + 1) * main4, :] = _strided(d_scr, qb0 + r4, main4, mid)
        v4[r4 * main4:(r4 + 1) * main4, :] = _strided(l_scr, qb0 + r4, main4, mid)
    _, m1, _, _, qb1 = layout[1]
    _, m2, _, _, qb2 = layout[2]
    for r in range(big):
        r4, s = r % mid, r // mid
        first = r4 * main4 + s
        nums = [_strided(k4, first, rows, mid), _strided(o_scr, qb1 + r4 * m1 + s, rows, mid),
                o_scr[qb2 + r * m2:qb2 + (r + 1) * m2, :]]
        dens = [_strided(k4, m0 + first, rows, mid), _strided(d_scr, qb1 + r4 * m1 + s, rows, mid),
                d_scr[qb2 + r * m2:qb2 + (r + 1) * m2, :]]
        mxs = [_strided(v4, first, rows, mid), _strided(l_scr, qb1 + r4 * m1 + s, rows, mid),
               l_scr[qb2 + r * m2:qb2 + (r + 1) * m2, :]]
        top = jnp.maximum(jnp.maximum(mxs[0], mxs[1]), mxs[2])
        ws = [jnp.exp(mx - top) for mx in mxs]
        num = ws[0] * nums[0] + ws[1] * nums[1] + ws[2] * nums[2]
        den = ws[0] * dens[0] + ws[1] * dens[1] + ws[2] * dens[2]
        q4[pl.ds(first, rows, stride=mid), :] = num / den
    for r4 in range(mid):
        o_ref[pl.ds(r4, main4, stride=mid), :] = q4[r4 * main4:(r4 + 1) * main4, :]


def _attention(q, k, v, rel_bias, tp, seq_p, seq_s):
    t, attn_w = q.shape
    n_heads = attn_w // HEAD_DIM
    pair = LANES // HEAD_DIM
    assert seq_p % ATT_TILE == 0 and seq_s % ATT_TILE == 0 and ATT_TILE == 2 * ATT_HALO
    assert ATT_TILE // WINDOWS[-1][1] == ATT_QB and pair == 2
    layout, krows = _branch_layout()

    ii = np.arange(ATT_QB)[:, None]
    jj = np.arange(2 * ATT_QB)[None, :]
    rel = jj - ATT_SIDE - ii
    band = np.abs(rel) <= ATT_SIDE
    buckets = np.stack([_t5_bucket(rel * dil) for _, dil in WINDOWS])
    bucket_of = jnp.asarray(buckets)[None]
    bias = jnp.zeros((n_heads,) + buckets.shape, F32)
    for b in range(N_BUCKETS):
        bias = jnp.where(bucket_of == b, rel_bias[b].astype(F32)[:, None, None, None], bias)
    bias = jnp.where(jnp.asarray(band)[None, None], bias, NEG_INF)
    nb = len(WINDOWS)
    bias = bias.reshape(n_heads // pair, pair, nb, ATT_QB, 2 * ATT_QB).transpose(0, 2, 1, 3, 4)
    bias = bias.reshape(n_heads // pair, nb, 1, pair * ATT_QB, 2 * ATT_QB)
    left = jnp.asarray((jj < ATT_SIDE)[None, None, None] & (np.arange(4) % 2 == 1)[None, None, :, None, None])
    right = jnp.asarray((jj >= 2 * ATT_QB - ATT_SIDE)[None, None, None]
                        & (np.arange(4) // 2 == 1)[None, None, :, None, None])
    bias = jnp.where(left | right, NEG_INF, bias)

    halo_blocks = t // ATT_HALO
    per_tile = ATT_TILE // ATT_HALO
    mid = WINDOWS[1][1]
    main = pl.BlockSpec((ATT_TILE, LANES), lambda i, j: (i, j))
    prev = pl.BlockSpec((ATT_HALO, LANES), lambda i, j: (jnp.maximum(i * per_tile - 1, 0), j))
    nxt = pl.BlockSpec((ATT_HALO, LANES), lambda i, j: (jnp.minimum((i + 1) * per_tile, halo_blocks - 1), j))
    n_q = nb * ATT_TILE
    return pl.pallas_call(
        functools.partial(_attn_kernel, tp // ATT_TILE, seq_p // ATT_TILE, seq_s // ATT_TILE),
        grid=(t // ATT_TILE, attn_w // LANES),
        in_specs=[main, main, prev, nxt, main, prev, nxt,
                  pl.BlockSpec((None, nb, 4, pair * ATT_QB, 2 * ATT_QB), lambda i, j: (j, 0, 0, 0, 0))],
        out_specs=main,
        out_shape=jax.ShapeDtypeStruct((t, attn_w), F32),
        scratch_shapes=[pltpu.VMEM((n_q, LANES), BF16),
                        pltpu.VMEM((krows // ATT_QB, LANES, ATT_QB), BF16),
                        pltpu.VMEM((krows, 2 * LANES), BF16),
                        pltpu.VMEM((n_q, LANES), F32), pltpu.VMEM((n_q, LANES), F32),
                        pltpu.VMEM((n_q, LANES), F32),
                        pltpu.VMEM((ATT_PIPE, pair * ATT_QB, 2 * ATT_QB), F32),
                        pltpu.VMEM((ATT_PIPE, pair * ATT_QB, 2 * ATT_QB), BF16),
                        pltpu.VMEM((ATT_PIPE, pair * ATT_QB, LANES), F32),
                        pltpu.VMEM(((ATT_TILE + 2 * ATT_HALO), LANES), F32),
                        pltpu.VMEM(((ATT_TILE + 2 * ATT_HALO), LANES), F32),
                        pltpu.VMEM((ATT_TILE, LANES), F32)],
        compiler_params=pltpu.CompilerParams(dimension_semantics=("arbitrary", "arbitrary"),
                                             vmem_limit_bytes=VMEM_LIMIT),
        name="attn",
    )(q, k, k, k, v, v, v, bias)


def _outproj_kernel(sgu_w, n_exp, a_ref, attn_ref, x_ref, onb_ref, wo_ref, norm2_ref, wrh_ref, wrl_ref,
                    br_ref, tri_ref, x1_ref, h2_ref, meta_ref, cnt_ref, route_ref, carry):
    i = pl.program_id(0)

    @pl.when(i == 0)
    def _():
        carry[...] = jnp.zeros_like(carry)

    bn = _rms(attn_ref[...], onb_ref[...]).astype(BF16)
    x1 = x_ref[...] + _dot(a_ref[...], wo_ref[0:sgu_w, :]) + _dot(bn, wo_ref[sgu_w:, :])
    x1_ref[...] = x1
    h2 = _rms(x1, norm2_ref[...])
    h2_ref[...] = h2.astype(BF16)
    hi, lo = _split_bf16(h2)
    logits = _dot(hi, wrh_ref[...]) + _dot(lo, wrh_ref[...]) + _dot(hi, wrl_ref[...]) + br_ref[...]

    lane = lax.broadcasted_iota(jnp.int32, logits.shape, 1)
    work = jnp.where(lane < n_exp, logits, -jnp.inf)
    chosen = jnp.zeros(logits.shape, jnp.bool_)
    experts, values = [], []
    for _ in range(TOP_K):
        top = jnp.max(work, axis=-1, keepdims=True)
        idx = jnp.min(jnp.where(work == top, lane, LANES), axis=-1, keepdims=True)
        hit = lane == idx
        chosen = jnp.logical_or(chosen, hit)
        work = jnp.where(hit, -jnp.inf, work)
        experts.append(idx)
        values.append(top)
    exps = [jnp.exp(v - values[0]) for v in values]
    den = exps[0] + exps[1] + exps[2] + exps[3]

    onehot = jnp.where(chosen, 1.0, 0.0)
    before = _dot(tri_ref[...], onehot.astype(BF16)) + carry[0:1, :]
    carry[0:1, :] = carry[0:1, :] + jnp.sum(onehot, axis=0, keepdims=True)
    cnt_ref[...] = carry[...]

    meta = jnp.zeros(logits.shape, F32)
    for kk in range(TOP_K):
        rank = jnp.sum(jnp.where(lane == experts[kk], before, 0.0), axis=-1, keepdims=True)
        meta = jnp.where(lane == kk, experts[kk].astype(F32), meta)
        meta = jnp.where(lane == TOP_K + kk, exps[kk] / den, meta)
        meta = jnp.where(lane == 2 * TOP_K + kk, rank, meta)
    meta_ref[...] = meta
    route_ref[...] = meta.T[0:route_ref.shape[0], :]


def _outproj(a_n, attn, x, row0, out_norm_b, w_o, norm2, w_router, b_router):
    sgu_w = a_n.shape[1]
    attn_w = attn.shape[1]
    t, d_model = x.shape
    n_exp = w_router.shape[1]
    assert n_exp <= LANES and row0 % TOK_TILE == 0 and t % TOK_TILE == 0
    b0 = row0 // TOK_TILE
    wr = jnp.pad(w_router, ((0, 0), (0, LANES - n_exp)))
    wr_hi = wr.astype(BF16)
    wr_lo = (wr - wr_hi.astype(F32)).astype(BF16)
    br = jnp.pad(b_router, (0, LANES - n_exp))[None]
    tri = np.tril(np.ones((TOK_TILE, TOK_TILE), np.float32), -1)

    const = lambda shape: pl.BlockSpec(shape, lambda i: (0,) * len(shape))
    tok = lambda w: pl.BlockSpec((TOK_TILE, w), lambda i: (i, 0))
    src = lambda w: pl.BlockSpec((TOK_TILE, w), lambda i: (i + b0, 0))
    return pl.pallas_call(
        functools.partial(_outproj_kernel, sgu_w, n_exp),
        grid=(t // TOK_TILE,),
        in_specs=[
            src(sgu_w), src(attn_w), tok(d_model),
            const((1, attn_w)), const(w_o.shape), const((1, d_model)), const((d_model, LANES)),
            const((d_model, LANES)), const((1, LANES)), const((TOK_TILE, TOK_TILE)),
        ],
        out_specs=[tok(d_model), src(d_model), tok(LANES), const((8, LANES)),
                   pl.BlockSpec((4 * TOP_K, TOK_TILE), lambda i: (0, i))],
        out_shape=[jax.ShapeDtypeStruct((t, d_model), F32), jax.ShapeDtypeStruct((a_n.shape[0], d_model), BF16),
                   jax.ShapeDtypeStruct((t, LANES), F32), jax.ShapeDtypeStruct((8, LANES), F32),
                   jax.ShapeDtypeStruct((4 * TOP_K, t), F32)],
        scratch_shapes=[pltpu.VMEM((8, LANES), F32)],
        compiler_params=pltpu.CompilerParams(dimension_semantics=("arbitrary",), vmem_limit_bytes=VMEM_LIMIT),
        name="outproj",
    )(a_n, attn, x, out_norm_b[None], w_o.astype(BF16), norm2[None], wr_hi, wr_lo, br, jnp.asarray(tri, BF16))


def _moe_kernel(d_exp, blk_ref, exp_ref, lo_ref, hi_ref, x_ref, wgu_ref, bgu_ref, wd_ref, bd_ref, y_ref,
                wgu_s, wd_s, act_s):
    it = pl.program_id(0)
    prev = jnp.maximum(it - 1, 0)
    new_expert = jnp.logical_or(it == 0, exp_ref[it] != exp_ref[prev])
    new_block = jnp.logical_or(it == 0, blk_ref[it] != blk_ref[prev])
    lo, hi = lo_ref[it], hi_ref[it]
    active = hi > lo

    @pl.when(jnp.logical_and(new_expert, active))
    def _():
        wgu_s[...] = wgu_ref[...].astype(BF16)
        wd_s[...] = wd_ref[...].astype(BF16)

    @pl.when(active)
    def _():
        x = x_ref[...]
        step = 512
        for n in range(0, d_exp, step):
            gate = _dot(x, wgu_s[:, n:n + step]) + bgu_ref[:, n:n + step]
            up = _dot(x, wgu_s[:, d_exp + n:d_exp + n + step]) + bgu_ref[:, d_exp + n:d_exp + n + step]
            gate = jnp.minimum(gate, SWIGLU_LIMIT)
            up = jnp.clip(up, -SWIGLU_LIMIT, SWIGLU_LIMIT)
            glu = gate * jax.nn.sigmoid(SWIGLU_ALPHA * gate)
            act_s[:, n:n + step] = ((up + 1.0) * glu).astype(BF16)
        y = (_dot(act_s[...], wd_s[...]) + bd_ref[...]).astype(y_ref.dtype)
        row = lax.broadcasted_iota(jnp.int32, (MOE_ROWS, 1), 0)
        mine = jnp.logical_and(row >= lo, row < hi)

        @pl.when(new_block)
        def _():
            y_ref[...] = jnp.where(mine, y, jnp.zeros_like(y))

        @pl.when(jnp.logical_not(new_block))
        def _():
            y_ref[...] = jnp.where(mine, y, y_ref[...])


def _moe_experts(x_rows, items, w_gu, b_gu, w_down, b_down):
    n_rows, d_model = x_rows.shape
    n_exp, _, two_de = w_gu.shape
    d_exp = two_de // 2
    n_items = items[0].shape[0]
    grid_spec = pltpu.PrefetchScalarGridSpec(
        num_scalar_prefetch=4,
        grid=(n_items,),
        in_specs=[
            pl.BlockSpec((MOE_ROWS, d_model), lambda i, blk, ex, lo, hi: (blk[i], 0)),
            pl.BlockSpec((None, d_model, two_de), lambda i, blk, ex, lo, hi: (ex[i], 0, 0)),
            pl.BlockSpec((None, 1, two_de), lambda i, blk, ex, lo, hi: (ex[i], 0, 0)),
            pl.BlockSpec((None, d_exp, d_model), lambda i, blk, ex, lo, hi: (ex[i], 0, 0)),
            pl.BlockSpec((None, 1, d_model), lambda i, blk, ex, lo, hi: (ex[i], 0, 0)),
        ],
        out_specs=pl.BlockSpec((MOE_ROWS, d_model), lambda i, blk, ex, lo, hi: (blk[i], 0)),
        scratch_shapes=[pltpu.VMEM((d_model, two_de), BF16), pltpu.VMEM((d_exp, d_model), BF16),
                        pltpu.VMEM((MOE_ROWS, d_exp), BF16)],
    )
    return pl.pallas_call(
        functools.partial(_moe_kernel, d_exp),
        grid_spec=grid_spec,
        out_shape=jax.ShapeDtypeStruct((n_rows, d_model), BF16),
        compiler_params=pltpu.CompilerParams(dimension_semantics=("arbitrary",), vmem_limit_bytes=VMEM_LIMIT),
        name="moe",
    )(*items, x_rows, w_gu, b_gu[:, None, :], w_down, b_down[:, None, :])


def _route(meta, counts_f, n_exp):
    t = meta.shape[1]
    n_assign = t * TOP_K
    assert n_assign % MOE_ROWS == 0
    top_e = meta[0:TOP_K].astype(jnp.int32)
    rank = meta[2 * TOP_K:3 * TOP_K].astype(jnp.int32)
    counts = counts_f[0, :n_exp].astype(jnp.int32)
    end = jnp.cumsum(counts)
    start = end - counts
    experts = jnp.arange(n_exp, dtype=jnp.int32)
    start_of = jnp.zeros_like(top_e)
    for e in range(n_exp):
        start_of = jnp.where(top_e == e, start[e], start_of)
    pos = (start_of + rank).reshape(-1)

    shift = (n_assign - 1).bit_length()
    assert (n_exp << shift) < 2 ** 31
    assign = jnp.arange(t, dtype=jnp.int32)[None, :] * TOP_K + jnp.arange(TOP_K, dtype=jnp.int32)[:, None]
    keys = ((top_e << shift) + assign).reshape(-1)
    sorted_tok = (jnp.sort(keys) & ((1 << shift) - 1)) // TOP_K

    n_blocks = n_assign // MOE_ROWS
    n_items = n_blocks + n_exp
    first_blk = start // MOE_ROWS
    n_it = jnp.where(counts > 0, (end - 1) // MOE_ROWS - first_blk + 1, 0)
    it_end = jnp.cumsum(n_it)
    it_start = it_end - n_it
    i = jnp.arange(n_items, dtype=jnp.int32)
    valid = i < it_end[-1]
    e_i = jnp.sum(jnp.minimum(i, it_end[-1] - 1)[:, None] >= it_end[None, :], axis=1).astype(jnp.int32)
    e_i = jnp.minimum(e_i, n_exp - 1)
    pick = lambda table: jnp.sum(jnp.where(e_i[:, None] == experts[None, :], table[None, :], 0), axis=1)
    blk_i = jnp.where(valid, pick(first_blk) + i - pick(it_start), n_blocks - 1)
    lo_i = jnp.where(valid, jnp.clip(pick(start) - blk_i * MOE_ROWS, 0, MOE_ROWS), 0)
    hi_i = jnp.where(valid, jnp.clip(pick(end) - blk_i * MOE_ROWS, 0, MOE_ROWS), 0)
    items = tuple(a.astype(jnp.int32) for a in (blk_i, e_i, lo_i, hi_i))
    return pos, sorted_tok, items


def _combine_kernel(x1_ref, yg_ref, meta_ref, o_ref):
    acc = x1_ref[...]
    for kk in range(TOP_K):
        gate = meta_ref[:, TOP_K + kk:TOP_K + kk + 1]
        acc = acc + gate * yg_ref[kk].astype(F32)
    o_ref[...] = acc


def _combine(x1, yg, meta):
    rows, d_model = x1.shape
    tok = lambda w: pl.BlockSpec((TOK_TILE, w), lambda i: (i, 0))
    return pl.pallas_call(
        _combine_kernel,
        grid=(rows // TOK_TILE,),
        in_specs=[tok(d_model), pl.BlockSpec((TOP_K, TOK_TILE, d_model), lambda i: (0, i, 0)), tok(LANES)],
        out_specs=tok(d_model),
        out_shape=jax.ShapeDtypeStruct((rows, d_model), F32),
        compiler_params=pltpu.CompilerParams(dimension_semantics=("arbitrary",), vmem_limit_bytes=VMEM_LIMIT),
        name="combine",
    )(x1, yg, meta)


def kernel(x_prompt, x_sample, norm1, w_in, q_gain, k_gain, rel_bias, sgu_norm, w_s, b_s, out_norm_a, out_norm_b,
           w_o, norm2, w_router, b_router, w_gu, b_gu, w_down, b_down):
    assert norm1.shape[0] == 1, "single-layer trunk"
    bp, sp, d_model = x_prompt.shape
    bs, ss, _ = x_sample.shape
    xp = x_prompt.reshape(bp * sp, d_model)
    xs = x_sample.reshape(bs * ss, d_model)
    tp, t = bp * sp, bp * sp + bs * ss

    q, k, v, a_n = _inproj(xp, xs, norm1[0], w_in[0], q_gain[0], k_gain[0], sgu_norm[0], w_s[0], b_s[0],
                           out_norm_a[0])
    attn = _attention(q, k, v, rel_bias, tp, sp, ss)

    outs = {}
    for x, row0 in sorted(((xp, 0), (xs, tp)), key=lambda c: -c[0].shape[0]):
        rows = x.shape[0]
        x1, h2, meta, counts, route = _outproj(a_n, attn, x, row0, out_norm_b[0], w_o[0], norm2[0], w_router[0],
                                               b_router[0])
        pos, sorted_tok, items = _route(route, counts, w_router.shape[-1])
        x_rows = h2.at[sorted_tok + row0].get(mode="promise_in_bounds")
        y_rows = _moe_experts(x_rows, items, w_gu[0], b_gu[0], w_down[0], b_down[0])
        yg = y_rows.at[pos].get(mode="promise_in_bounds").reshape(TOP_K, rows, d_model)
        outs[row0] = _combine(x1, yg, meta)
    return (outs[0].reshape(bp, sp, d_model), outs[tp].reshape(bs, ss, d_model))
```

```python
import functools
import math

import numpy as np
import jax
import jax.numpy as jnp
from jax import lax
from jax.experimental import pallas as pl
from jax.experimental.pallas import tpu as pltpu

F32 = jnp.float32
BF16 = jnp.bfloat16

EPS = 1e-6
NEG_INF = -1e30
HEAD_DIM = 64
SGU_CHUNK = 128
WINDOWS = ((128, 1), (512, 4), (2048, 16))
N_BUCKETS = 32
MAX_DISTANCE = 1024
TOP_K = 4
SWIGLU_LIMIT = 7.0
SWIGLU_ALPHA = 1.702

LANES = 128
TOK_TILE = 512
ATT_TILE = 2048
ATT_HALO = 1024
ATT_QB = 128
ATT_SIDE = 64
ATT_PIPE = 8
MOE_ROWS = 512
VMEM_LIMIT = 56 * 1024 * 1024


def _dot(a, b):
    return jnp.dot(a, b, preferred_element_type=F32)


def _dot_nt(a, b):
    return lax.dot_general(a, b, (((1,), (1,)), ((), ())), preferred_element_type=F32)


def _rms(x, g):
    ms = jnp.mean(x * x, axis=-1, keepdims=True)
    return x * lax.rsqrt(ms + EPS) * g


def _gelu(x):
    return 0.5 * x * (1.0 + lax.erf(x * (1.0 / math.sqrt(2.0))))


def _split_bf16(x):
    hi = x.astype(BF16)
    lo = (x - hi.astype(F32)).astype(BF16)
    return hi, lo


def _inproj_kernel(n1, attn_w, sgu_w, xp_ref, xs_ref, norm1_ref, win_ref, gsum_ref, gexp_ref,
                   qkg_ref, sgun_ref, wsp_ref, bsb_ref, ona_ref, q_ref, k_ref, v_ref, a_ref):
    i = pl.program_id(0)
    x = jnp.where(i < n1, xp_ref[...], xs_ref[...])
    h = _rms(x, norm1_ref[...]).astype(BF16)

    zqk = _dot(h, win_ref[:, 0:2 * attn_w])
    sq_hi, sq_lo = _split_bf16(zqk * zqk)
    ss = _dot(sq_hi, gsum_ref[...]) + _dot(sq_lo, gsum_ref[...])
    inv = lax.rsqrt(ss * (1.0 / HEAD_DIM) + EPS)
    inv_hi, inv_lo = _split_bf16(inv)
    invb = _dot(inv_hi, gexp_ref[...]) + _dot(inv_lo, gexp_ref[...])
    qk = zqk * invb * qkg_ref[...]
    q_ref[...] = qk[:, 0:attn_w]
    k_ref[...] = qk[:, attn_w:2 * attn_w]
    v_ref[...] = _dot(h, win_ref[:, 2 * attn_w:3 * attn_w])

    c0 = 3 * attn_w
    u = _gelu(_dot(h, win_ref[:, c0:c0 + sgu_w]))
    gv = _gelu(_dot(h, win_ref[:, c0 + sgu_w:c0 + 2 * sgu_w]))
    vsn = _rms(gv, sgun_ref[...]).astype(BF16)
    lane = lax.broadcasted_iota(jnp.int32, (SGU_CHUNK, LANES), 1)
    lo_half = lane < HEAD_DIM
    zero = jnp.zeros((SGU_CHUNK, LANES), BF16)
    for c in range(TOK_TILE // SGU_CHUNK):
        r0 = c * SGU_CHUNK
        parts = []
        for j in range(sgu_w // LANES):
            blk = vsn[r0:r0 + SGU_CHUNK, j * LANES:(j + 1) * LANES]
            rhs = jnp.concatenate([jnp.where(lo_half, blk, zero), jnp.where(lo_half, zero, blk)], axis=0)
            parts.append(_dot(wsp_ref[j], rhs))
        s = jnp.concatenate(parts, axis=1) + bsb_ref[...]
        a = u[r0:r0 + SGU_CHUNK, :] * s
        a_ref[r0:r0 + SGU_CHUNK, :] = _rms(a, ona_ref[...]).astype(BF16)


def _inproj(xp, xs, norm1, w_in, q_gain, k_gain, sgu_norm, w_s, b_s, out_norm_a):
    tp, d_model = xp.shape
    ts = xs.shape[0]
    t = tp + ts
    n_heads_w = q_gain.shape[0]
    sgu_w = sgu_norm.shape[0]
    attn_w = (w_in.shape[1] - 2 * sgu_w) // 3
    n_heads = attn_w // n_heads_w
    n_groups = w_s.shape[0]
    assert n_heads_w == HEAD_DIM and sgu_w // n_groups == HEAD_DIM and w_s.shape[1] == SGU_CHUNK
    assert tp % TOK_TILE == 0 and ts % TOK_TILE == 0 and 2 * n_heads <= LANES
    n1 = tp // TOK_TILE

    heads = np.arange(2 * attn_w) // HEAD_DIM
    gsum = (heads[:, None] == np.arange(LANES)[None, :]).astype(np.float32)
    gexp = gsum.T
    qkg = jnp.concatenate([jnp.tile(q_gain, n_heads) * (HEAD_DIM ** -0.5), jnp.tile(k_gain, n_heads)])
    wsp = jnp.concatenate([w_s[0::2], w_s[1::2]], axis=2).astype(BF16)
    bsb = jnp.repeat(b_s.T, HEAD_DIM, axis=1)

    const = lambda shape: pl.BlockSpec(shape, lambda i: (0,) * len(shape))
    tok = lambda w: pl.BlockSpec((TOK_TILE, w), lambda i: (i, 0))
    return pl.pallas_call(
        functools.partial(_inproj_kernel, n1, attn_w, sgu_w),
        grid=(t // TOK_TILE,),
        in_specs=[
            pl.BlockSpec((TOK_TILE, d_model), lambda i: (jnp.minimum(i, n1 - 1), 0)),
            pl.BlockSpec((TOK_TILE, d_model), lambda i: (jnp.maximum(i - n1, 0), 0)),
            const((1, d_model)), const(w_in.shape), const((2 * attn_w, LANES)), const((LANES, 2 * attn_w)),
            const((1, 2 * attn_w)), const((1, sgu_w)), const(wsp.shape), const((SGU_CHUNK, sgu_w)),
            const((1, sgu_w)),
        ],
        out_specs=[tok(attn_w), tok(attn_w), tok(attn_w), tok(sgu_w)],
        out_shape=[jax.ShapeDtypeStruct((t, attn_w), F32)] * 3 + [jax.ShapeDtypeStruct((t, sgu_w), BF16)],
        compiler_params=pltpu.CompilerParams(dimension_semantics=("arbitrary",), vmem_limit_bytes=VMEM_LIMIT),
        name="inproj",
    )(xp, xs, norm1[None], w_in.astype(BF16), jnp.asarray(gsum, BF16), jnp.asarray(gexp, BF16),
      qkg[None], sgu_norm[None], wsp, bsb, out_norm_a[None])


def _t5_bucket(rel):
    nb = N_BUCKETS // 2
    bucket = (rel > 0).astype(np.int32) * nb
    n = np.abs(rel)
    max_exact = nb // 2
    large = max_exact + (np.log(np.maximum(n, 1) / max_exact)
                         / np.log(MAX_DISTANCE / max_exact) * (nb - max_exact)).astype(np.int32)
    large = np.minimum(large, nb - 1)
    return (bucket + np.where(n < max_exact, n, large)).astype(np.int32)


def _branch_layout():
    out, kbase = [], 0
    for bi, (window, dil) in enumerate(WINDOWS):
        assert window // (2 * dil) == ATT_SIDE
        m = ATT_TILE // dil
        seg = m + 2 * ATT_SIDE
        out.append((dil, m, seg, kbase, bi * ATT_TILE))
        kbase += dil * seg
    return out, kbase


def _strided(ref, start, size, stride):
    if stride == 1:
        return ref[start:start + size, :]
    return ref[pl.ds(start, size, stride=stride), :]


def _attn_kernel(tiles_p, per_seq_p, per_seq_s, q_ref, km_ref, kp_ref, kn_ref, vm_ref, vp_ref, vn_ref,
                 bias_ref, o_ref, qs, kts, vs, o_scr, d_scr, l_scr, s_scr, p_scr, m_scr, k4, v4, q4):
    i = pl.program_id(0)
    w = jnp.where(i < tiles_p, i % per_seq_p, (i - tiles_p) % per_seq_s)
    last = jnp.where(i < tiles_p, per_seq_p - 1, per_seq_s - 1)
    left_ok = w > 0
    right_ok = w < last
    layout, _ = _branch_layout()
    mid = WINDOWS[1][1]
    assert WINDOWS[2][1] == mid * mid
    halo4, main4 = ATT_HALO // mid, ATT_TILE // mid
    win4 = 2 * halo4 + main4

    @pl.when(jnp.logical_and(i == 0, pl.program_id(1) == 0))
    def _():
        vs[:, LANES:] = jnp.ones((vs.shape[0], LANES), BF16)

    for r in range(mid):
        for dst, main, prev, nxt in ((k4, km_ref, kp_ref, kn_ref), (v4, vm_ref, vp_ref, vn_ref)):
            dst[r * win4:r * win4 + halo4, :] = _strided(prev, r, halo4, mid)
            dst[r * win4 + halo4:r * win4 + halo4 + main4, :] = _strided(main, r, main4, mid)
            dst[r * win4 + halo4 + main4:(r + 1) * win4, :] = _strided(nxt, r, halo4, mid)
        q4[r * main4:(r + 1) * main4, :] = _strided(q_ref, r, main4, mid)

    def key_rows(src_k, src_v, start, stride, kbase_rows):
        kc = _strided(src_k, start, ATT_QB, stride)
        kts[kbase_rows // ATT_QB] = kc.T.astype(BF16)
        vs[kbase_rows:kbase_rows + ATT_QB, 0:LANES] = _strided(src_v, start, ATT_QB, stride).astype(BF16)

    for bi, (dil, m, seg, kbase, qbase) in enumerate(layout):
        nchunk = seg // ATT_QB
        for r in range(dil):
            o = kbase + r * seg
            if dil == 1:
                for c in range(nchunk):
                    lo = c * ATT_QB - ATT_SIDE
                    if c == 0:
                        kc = jnp.concatenate([kp_ref[ATT_HALO - ATT_SIDE:, :], km_ref[0:ATT_QB - ATT_SIDE, :]], axis=0)
                        vc = jnp.concatenate([vp_ref[ATT_HALO - ATT_SIDE:, :], vm_ref[0:ATT_QB - ATT_SIDE, :]], axis=0)
                    elif c == nchunk - 1:
                        kc = jnp.concatenate([km_ref[lo:, :], kn_ref[0:ATT_SIDE, :]], axis=0)
                        vc = jnp.concatenate([vm_ref[lo:, :], vn_ref[0:ATT_SIDE, :]], axis=0)
                    else:
                        kc = km_ref[lo:lo + ATT_QB, :]
                        vc = vm_ref[lo:lo + ATT_QB, :]
                    kts[(o + c * ATT_QB) // ATT_QB] = kc.T.astype(BF16)
                    vs[o + c * ATT_QB:o + (c + 1) * ATT_QB, 0:LANES] = vc.astype(BF16)
                qs[qbase:qbase + m, :] = q_ref[...].astype(BF16)
            else:
                r4, s = r % mid, r // mid
                step = dil // mid
                first = r4 * win4 + (halo4 - ATT_SIDE * step) + s
                for c in range(nchunk):
                    key_rows(k4, v4, first + c * ATT_QB * step, step, o + c * ATT_QB)
                qs[qbase + r * m:qbase + (r + 1) * m, :] = _strided(q4, r4 * main4 + s, m, step).astype(BF16)

    lane = lax.broadcasted_iota(jnp.int32, (ATT_QB, LANES), 1)
    head0 = lane < HEAD_DIM
    cells = ATT_TILE // ATT_QB
    total = len(layout) * cells
    blk_bits = [(m // ATT_QB).bit_length() - 1 for _, m, _, _, _ in layout]
    seg_chunks = [seg // ATT_QB for _, _, seg, _, _ in layout]
    base_chunks = [kbase // ATT_QB for _, _, _, kbase, _ in layout]

    def cell_params(g):
        bi = g // cells
        idx = g % cells
        if isinstance(g, int):
            pick = lambda vals: vals[bi]
        else:
            pick = lambda vals: jnp.where(bi == 0, vals[0], jnp.where(bi == 1, vals[1], vals[2]))
        bits = pick(blk_bits)
        last_blk = (1 << bits) - 1
        blk = idx & last_blk
        return bi, blk, last_blk, pick(base_chunks) + (idx >> bits) * pick(seg_chunks) + blk

    def stage_logits(g, slot):
        bi, blk, last_blk, kchunk = cell_params(g)
        left_bad = jnp.logical_and(blk == 0, jnp.logical_not(left_ok))
        right_bad = jnp.logical_and(blk == last_blk, jnp.logical_not(right_ok))
        variant = left_bad.astype(jnp.int32) + 2 * right_bad.astype(jnp.int32)
        qc = qs[pl.ds(pl.multiple_of(g * ATT_QB, ATT_QB), ATT_QB), :]
        zero = jnp.zeros_like(qc)
        lhs = jnp.concatenate([jnp.where(head0, qc, zero), jnp.where(head0, zero, qc)], axis=0)
        kt = jnp.concatenate([kts[kchunk], kts[kchunk + 1]], axis=1)
        s_scr[slot] = _dot(lhs, kt) + bias_ref[bi, variant]

    def stage_softmax(slot):
        part = ATT_QB // 2
        for c in range(2 * ATT_QB // part):
            s = s_scr[slot, c * part:(c + 1) * part, :]
            mx = jnp.max(s, axis=-1, keepdims=True)
            p_scr[slot, c * part:(c + 1) * part, :] = jnp.exp(s - mx).astype(BF16)
            m_scr[slot, c * part:(c + 1) * part, :] = jnp.broadcast_to(mx, (part, LANES))

    def stage_values(g, slot):
        _, _, _, kchunk = cell_params(g)
        koff = pl.multiple_of(kchunk * ATT_QB, ATT_QB)
        r = _dot(p_scr[slot], vs[pl.ds(koff, 2 * ATT_QB), :])
        num = jnp.where(head0, r[0:ATT_QB, 0:LANES], r[ATT_QB:, 0:LANES])
        den = jnp.where(head0, r[0:ATT_QB, LANES:], r[ATT_QB:, LANES:])
        mx = jnp.where(head0, m_scr[slot, 0:ATT_QB, :], m_scr[slot, ATT_QB:, :])
        rows = pl.ds(pl.multiple_of(g * ATT_QB, ATT_QB), ATT_QB)
        o_scr[rows, :] = num
        d_scr[rows, :] = den
        l_scr[rows, :] = mx

    width = ATT_PIPE

    def group(c, do_values, do_softmax, do_logits):
        if do_values:
            for u in range(width):
                stage_values(c - width + u, u)
        if do_softmax:
            for u in range(width):
                stage_softmax(u)
        if do_logits:
            for u in range(width):
                stage_logits(c + width + u, u)

    group(-width, False, False, True)
    group(0, False, True, True)

    def trip(t, carry):
        group((t + 1) * width, True, True, True)
        return carry

    lax.fori_loop(0, total // width - 2, trip, 0)
    group(total - width, True, True, False)
    group(total, True, False, False)

    big = WINDOWS[-1][1]
    rows = ATT_TILE // big
    dil0, m0, _, _, qb0 = layout[0]
    assert dil0 == 1 and big == mid * mid
    for r4 in range(mid):
        k4[r4 * main4:(r4 + 1) * main4, :] = _strided(o_scr, qb0 + r4, main4, mid)
        k4[m0 + r4 * main4:m0 + (r4 + 1) * main4, :] = _strided(d_scr, qb0 + r4, main4, mid)
        v4[r4 * main4:(r4 + 1) * main4, :] = _strided(l_scr, qb0 + r4, main4, mid)
    _, m1, _, _, qb1 = layout[1]
    _, m2, _, _, qb2 = layout[2]
    for r in range(big):
        r4, s = r % mid, r // mid
        first = r4 * main4 + s
        nums = [_strided(k4, first, rows, mid), _strided(o_scr, qb1 + r4 * m1 + s, rows, mid),
                o_scr[qb2 + r * m2:qb2 + (r + 1) * m2, :]]
        dens = [_strided(k4, m0 + first, rows, mid), _strided(d_scr, qb1 + r4 * m1 + s, rows, mid),
                d_scr[qb2 + r * m2:qb2 + (r + 1) * m2, :]]
        mxs = [_strided(v4, first, rows, mid), _strided(l_scr, qb1 + r4 * m1 + s, rows, mid),
               l_scr[qb2 + r * m2:qb2 + (r + 1) * m2, :]]
        top = jnp.maximum(jnp.maximum(mxs[0], mxs[1]), mxs[2])
        ws = [jnp.exp(mx - top) for mx in mxs]
        num = ws[0] * nums[0] + ws[1] * nums[1] + ws[2] * nums[2]
        den = ws[0] * dens[0] + ws[1] * dens[1] + ws[2] * dens[2]
        q4[pl.ds(first, rows, stride=mid), :] = num / den
    for r4 in range(mid):
        o_ref[pl.ds(r4, main4, stride=mid), :] = q4[r4 * main4:(r4 + 1) * main4, :]


def _attention(q, k, v, rel_bias, tp, seq_p, seq_s):
    t, attn_w = q.shape
    n_heads = attn_w // HEAD_DIM
    pair = LANES // HEAD_DIM
    assert seq_p % ATT_TILE == 0 and seq_s % ATT_TILE == 0 and ATT_TILE == 2 * ATT_HALO
    assert ATT_TILE // WINDOWS[-1][1] == ATT_QB and pair == 2
    layout, krows = _branch_layout()

    nb = len(WINDOWS)
    span = 2 * ATT_SIDE + 1
    jj = np.arange(2 * ATT_QB)[None, :]
    offsets = np.arange(span) - ATT_SIDE
    table_bucket = np.stack([_t5_bucket(offsets * dil) for _, dil in WINDOWS])
    table = jnp.zeros((n_heads, nb, span), F32)
    for b in range(N_BUCKETS):
        table = jnp.where(jnp.asarray(table_bucket == b)[None], rel_bias[b].astype(F32)[:, None, None], table)
    row_len = 3 * ATT_QB
    period = row_len + 1
    padded = jnp.concatenate([table, jnp.full((n_heads, nb, period - span), NEG_INF, F32)], axis=-1)
    flat = jnp.tile(padded, (1, 1, ATT_QB + 1))[..., :ATT_QB * row_len]
    bias = flat.reshape(n_heads, nb, ATT_QB, row_len)[..., :2 * ATT_QB]
    bias = bias.reshape(n_heads // pair, pair, nb, ATT_QB, 2 * ATT_QB).transpose(0, 2, 1, 3, 4)
    bias = bias.reshape(n_heads // pair, nb, 1, pair * ATT_QB, 2 * ATT_QB)
    left = jnp.asarray((jj < ATT_SIDE)[None, None, None] & (np.arange(4) % 2 == 1)[None, None, :, None, None])
    right = jnp.asarray((jj >= 2 * ATT_QB - ATT_SIDE)[None, None, None]
                        & (np.arange(4) // 2 == 1)[None, None, :, None, None])
    bias = jnp.where(left | right, NEG_INF, bias)

    halo_blocks = t // ATT_HALO
    per_tile = ATT_TILE // ATT_HALO
    mid = WINDOWS[1][1]
    main = pl.BlockSpec((ATT_TILE, LANES), lambda i, j: (i, j))
    prev = pl.BlockSpec((ATT_HALO, LANES), lambda i, j: (jnp.maximum(i * per_tile - 1, 0), j))
    nxt = pl.BlockSpec((ATT_HALO, LANES), lambda i, j: (jnp.minimum((i + 1) * per_tile, halo_blocks - 1), j))
    n_q = nb * ATT_TILE
    return pl.pallas_call(
        functools.partial(_attn_kernel, tp // ATT_TILE, seq_p // ATT_TILE, seq_s // ATT_TILE),
        grid=(t // ATT_TILE, attn_w // LANES),
        in_specs=[main, main, prev, nxt, main, prev, nxt,
                  pl.BlockSpec((None, nb, 4, pair * ATT_QB, 2 * ATT_QB), lambda i, j: (j, 0, 0, 0, 0))],
        out_specs=main,
        out_shape=jax.ShapeDtypeStruct((t, attn_w), F32),
        scratch_shapes=[pltpu.VMEM((n_q, LANES), BF16),
                        pltpu.VMEM((krows // ATT_QB, LANES, ATT_QB), BF16),
                        pltpu.VMEM((krows, 2 * LANES), BF16),
                        pltpu.VMEM((n_q, LANES), F32), pltpu.VMEM((n_q, LANES), F32),
                        pltpu.VMEM((n_q, LANES), F32),
                        pltpu.VMEM((ATT_PIPE, pair * ATT_QB, 2 * ATT_QB), F32),
                        pltpu.VMEM((ATT_PIPE, pair * ATT_QB, 2 * ATT_QB), BF16),
                        pltpu.VMEM((ATT_PIPE, pair * ATT_QB, LANES), F32),
                        pltpu.VMEM(((ATT_TILE + 2 * ATT_HALO), LANES), F32),
                        pltpu.VMEM(((ATT_TILE + 2 * ATT_HALO), LANES), F32),
                        pltpu.VMEM((ATT_TILE, LANES), F32)],
        compiler_params=pltpu.CompilerParams(dimension_semantics=("arbitrary", "arbitrary"),
                                             vmem_limit_bytes=VMEM_LIMIT),
        name="attn",
    )(q, k, k, k, v, v, v, bias)


def _outproj_kernel(sgu_w, n_exp, a_ref, attn_ref, x_ref, onb_ref, wo_ref, norm2_ref, wrh_ref, wrl_ref,
                    br_ref, tri_ref, x1_ref, h2_ref, meta_ref, cnt_ref, route_ref, carry):
    i = pl.program_id(0)

    @pl.when(i == 0)
    def _():
        carry[...] = jnp.zeros_like(carry)

    bn = _rms(attn_ref[...], onb_ref[...]).astype(BF16)
    x1 = x_ref[...] + _dot(a_ref[...], wo_ref[0:sgu_w, :]) + _dot(bn, wo_ref[sgu_w:, :])
    x1_ref[...] = x1
    h2 = _rms(x1, norm2_ref[...])
    h2_ref[...] = h2.astype(BF16)
    hi, lo = _split_bf16(h2)
    logits = _dot(hi, wrh_ref[...]) + _dot(lo, wrh_ref[...]) + _dot(hi, wrl_ref[...]) + br_ref[...]

    lane = lax.broadcasted_iota(jnp.int32, logits.shape, 1)
    work = jnp.where(lane < n_exp, logits, -jnp.inf)
    chosen = jnp.zeros(logits.shape, jnp.bool_)
    experts, values = [], []
    for _ in range(TOP_K):
        top = jnp.max(work, axis=-1, keepdims=True)
        idx = jnp.min(jnp.where(work == top, lane, LANES), axis=-1, keepdims=True)
        hit = lane == idx
        chosen = jnp.logical_or(chosen, hit)
        work = jnp.where(hit, -jnp.inf, work)
        experts.append(idx)
        values.append(top)
    exps = [jnp.exp(v - values[0]) for v in values]
    den = exps[0] + exps[1] + exps[2] + exps[3]

    onehot = jnp.where(chosen, 1.0, 0.0)
    before = _dot(tri_ref[...], onehot.astype(BF16)) + carry[0:1, :]
    carry[0:1, :] = carry[0:1, :] + jnp.sum(onehot, axis=0, keepdims=True)
    cnt_ref[...] = carry[...]

    meta = jnp.zeros(logits.shape, F32)
    for kk in range(TOP_K):
        rank = jnp.sum(jnp.where(lane == experts[kk], before, 0.0), axis=-1, keepdims=True)
        meta = jnp.where(lane == kk, experts[kk].astype(F32), meta)
        meta = jnp.where(lane == TOP_K + kk, exps[kk] / den, meta)
        meta = jnp.where(lane == 2 * TOP_K + kk, rank, meta)
    meta_ref[...] = meta
    route_ref[...] = meta.T[0:route_ref.shape[0], :]


def _outproj(a_n, attn, x, row0, out_norm_b, w_o, norm2, w_router, b_router):
    sgu_w = a_n.shape[1]
    attn_w = attn.shape[1]
    t, d_model = x.shape
    n_exp = w_router.shape[1]
    assert n_exp <= LANES and row0 % TOK_TILE == 0 and t % TOK_TILE == 0
    b0 = row0 // TOK_TILE
    wr = jnp.pad(w_router, ((0, 0), (0, LANES - n_exp)))
    wr_hi = wr.astype(BF16)
    wr_lo = (wr - wr_hi.astype(F32)).astype(BF16)
    br = jnp.pad(b_router, (0, LANES - n_exp))[None]
    tri = np.tril(np.ones((TOK_TILE, TOK_TILE), np.float32), -1)

    const = lambda shape: pl.BlockSpec(shape, lambda i: (0,) * len(shape))
    tok = lambda w: pl.BlockSpec((TOK_TILE, w), lambda i: (i, 0))
    src = lambda w: pl.BlockSpec((TOK_TILE, w), lambda i: (i + b0, 0))
    return pl.pallas_call(
        functools.partial(_outproj_kernel, sgu_w, n_exp),
        grid=(t // TOK_TILE,),
        in_specs=[
            src(sgu_w), src(attn_w), tok(d_model),
            const((1, attn_w)), const(w_o.shape), const((1, d_model)), const((d_model, LANES)),
            const((d_model, LANES)), const((1, LANES)), const((TOK_TILE, TOK_TILE)),
        ],
        out_specs=[tok(d_model), src(d_model), tok(LANES), const((8, LANES)),
                   pl.BlockSpec((4 * TOP_K, TOK_TILE), lambda i: (0, i))],
        out_shape=[jax.ShapeDtypeStruct((t, d_model), F32), jax.ShapeDtypeStruct((a_n.shape[0], d_model), BF16),
                   jax.ShapeDtypeStruct((t, LANES), F32), jax.ShapeDtypeStruct((8, LANES), F32),
                   jax.ShapeDtypeStruct((4 * TOP_K, t), F32)],
        scratch_shapes=[pltpu.VMEM((8, LANES), F32)],
        compiler_params=pltpu.CompilerParams(dimension_semantics=("arbitrary",), vmem_limit_bytes=VMEM_LIMIT),
        name="outproj",
    )(a_n, attn, x, out_norm_b[None], w_o.astype(BF16), norm2[None], wr_hi, wr_lo, br, jnp.asarray(tri, BF16))


def _moe_kernel(d_exp, blk_ref, exp_ref, lo_ref, hi_ref, x_ref, wgu_ref, bgu_ref, wd_ref, bd_ref, y_ref,
                wgu_s, wd_s, act_s):
    it = pl.program_id(0)
    prev = jnp.maximum(it - 1, 0)
    new_expert = jnp.logical_or(it == 0, exp_ref[it] != exp_ref[prev])
    new_block = jnp.logical_or(it == 0, blk_ref[it] != blk_ref[prev])
    lo, hi = lo_ref[it], hi_ref[it]
    active = hi > lo

    @pl.when(jnp.logical_and(new_expert, active))
    def _():
        wgu_s[...] = wgu_ref[...].astype(BF16)
        wd_s[...] = wd_ref[...].astype(BF16)

    @pl.when(active)
    def _():
        x = x_ref[...]
        step = 512
        for n in range(0, d_exp, step):
            gate = _dot(x, wgu_s[:, n:n + step]) + bgu_ref[:, n:n + step]
            up = _dot(x, wgu_s[:, d_exp + n:d_exp + n + step]) + bgu_ref[:, d_exp + n:d_exp + n + step]
            gate = jnp.minimum(gate, SWIGLU_LIMIT)
            up = jnp.clip(up, -SWIGLU_LIMIT, SWIGLU_LIMIT)
            glu = gate * jax.nn.sigmoid(SWIGLU_ALPHA * gate)
            act_s[:, n:n + step] = ((up + 1.0) * glu).astype(BF16)
        y = (_dot(act_s[...], wd_s[...]) + bd_ref[...]).astype(y_ref.dtype)
        row = lax.broadcasted_iota(jnp.int32, (MOE_ROWS, 1), 0)
        mine = jnp.logical_and(row >= lo, row < hi)

        @pl.when(new_block)
        def _():
            y_ref[...] = jnp.where(mine, y, jnp.zeros_like(y))

        @pl.when(jnp.logical_not(new_block))
        def _():
            y_ref[...] = jnp.where(mine, y, y_ref[...])


def _moe_experts(x_rows, items, w_gu, b_gu, w_down, b_down):
    n_rows, d_model = x_rows.shape
    n_exp, _, two_de = w_gu.shape
    d_exp = two_de // 2
    n_items = items[0].shape[0]
    grid_spec = pltpu.PrefetchScalarGridSpec(
        num_scalar_prefetch=4,
        grid=(n_items,),
        in_specs=[
            pl.BlockSpec((MOE_ROWS, d_model), lambda i, blk, ex, lo, hi: (blk[i], 0)),
            pl.BlockSpec((None, d_model, two_de), lambda i, blk, ex, lo, hi: (ex[i], 0, 0)),
            pl.BlockSpec((None, 1, two_de), lambda i, blk, ex, lo, hi: (ex[i], 0, 0)),
            pl.BlockSpec((None, d_exp, d_model), lambda i, blk, ex, lo, hi: (ex[i], 0, 0)),
            pl.BlockSpec((None, 1, d_model), lambda i, blk, ex, lo, hi: (ex[i], 0, 0)),
        ],
        out_specs=pl.BlockSpec((MOE_ROWS, d_model), lambda i, blk, ex, lo, hi: (blk[i], 0)),
        scratch_shapes=[pltpu.VMEM((d_model, two_de), BF16), pltpu.VMEM((d_exp, d_model), BF16),
                        pltpu.VMEM((MOE_ROWS, d_exp), BF16)],
    )
    return pl.pallas_call(
        functools.partial(_moe_kernel, d_exp),
        grid_spec=grid_spec,
        out_shape=jax.ShapeDtypeStruct((n_rows, d_model), BF16),
        compiler_params=pltpu.CompilerParams(dimension_semantics=("arbitrary",), vmem_limit_bytes=VMEM_LIMIT),
        name="moe",
    )(*items, x_rows, w_gu, b_gu[:, None, :], w_down, b_down[:, None, :])


def _route(meta, counts_f, n_exp):
    t = meta.shape[1]
    n_assign = t * TOP_K
    assert n_assign % MOE_ROWS == 0
    top_e = meta[0:TOP_K].astype(jnp.int32)
    rank = meta[2 * TOP_K:3 * TOP_K].astype(jnp.int32)
    counts = counts_f[0, :n_exp].astype(jnp.int32)
    end = jnp.cumsum(counts)
    start = end - counts
    experts = jnp.arange(n_exp, dtype=jnp.int32)
    start_of = jnp.zeros_like(top_e)
    for e in range(n_exp):
        start_of = jnp.where(top_e == e, start[e], start_of)
    pos = (start_of + rank).reshape(-1)

    shift = (n_assign - 1).bit_length()
    assert (n_exp << shift) < 2 ** 31
    assign = jnp.arange(t, dtype=jnp.int32)[None, :] * TOP_K + jnp.arange(TOP_K, dtype=jnp.int32)[:, None]
    keys = ((top_e << shift) + assign).reshape(-1)
    sorted_tok = (jnp.sort(keys) & ((1 << shift) - 1)) // TOP_K

    n_blocks = n_assign // MOE_ROWS
    n_items = n_blocks + n_exp
    first_blk = start // MOE_ROWS
    n_it = jnp.where(counts > 0, (end - 1) // MOE_ROWS - first_blk + 1, 0)
    it_end = jnp.cumsum(n_it)
    it_start = it_end - n_it
    i = jnp.arange(n_items, dtype=jnp.int32)
    valid = i < it_end[-1]
    e_i = jnp.sum(jnp.minimum(i, it_end[-1] - 1)[:, None] >= it_end[None, :], axis=1).astype(jnp.int32)
    e_i = jnp.minimum(e_i, n_exp - 1)
    pick = lambda table: jnp.sum(jnp.where(e_i[:, None] == experts[None, :], table[None, :], 0), axis=1)
    blk_i = jnp.where(valid, pick(first_blk) + i - pick(it_start), n_blocks - 1)
    lo_i = jnp.where(valid, jnp.clip(pick(start) - blk_i * MOE_ROWS, 0, MOE_ROWS), 0)
    hi_i = jnp.where(valid, jnp.clip(pick(end) - blk_i * MOE_ROWS, 0, MOE_ROWS), 0)
    items = tuple(a.astype(jnp.int32) for a in (blk_i, e_i, lo_i, hi_i))
    return pos, sorted_tok, items


def _combine_kernel(x1_ref, yg_ref, meta_ref, o_ref):
    acc = x1_ref[...]
    for kk in range(TOP_K):
        gate = meta_ref[:, TOP_K + kk:TOP_K + kk + 1]
        acc = acc + gate * yg_ref[kk].astype(F32)
    o_ref[...] = acc


def _combine(x1, yg, meta):
    rows, d_model = x1.shape
    tok = lambda w: pl.BlockSpec((TOK_TILE, w), lambda i: (i, 0))
    return pl.pallas_call(
        _combine_kernel,
        grid=(rows // TOK_TILE,),
        in_specs=[tok(d_model), pl.BlockSpec((TOP_K, TOK_TILE, d_model), lambda i: (0, i, 0)), tok(LANES)],
        out_specs=tok(d_model),
        out_shape=jax.ShapeDtypeStruct((rows, d_model), F32),
        compiler_params=pltpu.CompilerParams(dimension_semantics=("arbitrary",), vmem_limit_bytes=VMEM_LIMIT),
        name="combine",
    )(x1, yg, meta)


def kernel(x_prompt, x_sample, norm1, w_in, q_gain, k_gain, rel_bias, sgu_norm, w_s, b_s, out_norm_a, out_norm_b,
           w_o, norm2, w_router, b_router, w_gu, b_gu, w_down, b_down):
    assert norm1.shape[0] == 1, "single-layer trunk"
    bp, sp, d_model = x_prompt.shape
    bs, ss, _ = x_sample.shape
    xp = x_prompt.reshape(bp * sp, d_model)
    xs = x_sample.reshape(bs * ss, d_model)
    tp, t = bp * sp, bp * sp + bs * ss

    q, k, v, a_n = _inproj(xp, xs, norm1[0], w_in[0], q_gain[0], k_gain[0], sgu_norm[0], w_s[0], b_s[0],
                           out_norm_a[0])
    attn = _attention(q, k, v, rel_bias, tp, sp, ss)

    outs = {}
    after = None
    for x, row0 in sorted(((xp, 0), (xs, tp)), key=lambda c: -c[0].shape[0]):
        rows = x.shape[0]
        if after is not None:
            x, _ = lax.optimization_barrier((x, after[0]))
        x1, h2, meta, counts, route = _outproj(a_n, attn, x, row0, out_norm_b[0], w_o[0], norm2[0], w_router[0],
                                               b_router[0])
        pos, sorted_tok, items = _route(route, counts, w_router.shape[-1])
        x_rows = h2.at[sorted_tok + row0].get(mode="promise_in_bounds")
        if after is not None:
            x_rows, _ = lax.optimization_barrier((x_rows, after[1]))
        y_rows = _moe_experts(x_rows, items, w_gu[0], b_gu[0], w_down[0], b_down[0])
        yg = y_rows.at[pos].get(mode="promise_in_bounds").reshape(TOP_K, rows, d_model)
        if after is not None:
            yg, _ = lax.optimization_barrier((yg, after[2]))
        outs[row0] = _combine(x1, yg, meta)
        after = (counts, y_rows, outs[row0])
    return (outs[0].reshape(bp, sp, d_model), outs[tp].reshape(bs, ss, d_model))
```

```python
import functools
import math

import numpy as np
import jax
import jax.numpy as jnp
from jax import lax
from jax.experimental import pallas as pl
from jax.experimental.pallas import tpu as pltpu

F32 = jnp.float32
BF16 = jnp.bfloat16

EPS = 1e-6
NEG_INF = -1e30
HEAD_DIM = 64
SGU_CHUNK = 128
WINDOWS = ((128, 1), (512, 4), (2048, 16))
N_BUCKETS = 32
MAX_DISTANCE = 1024
TOP_K = 4
SWIGLU_LIMIT = 7.0
SWIGLU_ALPHA = 1.702

LANES = 128
TOK_TILE = 512
ATT_TILE = 2048
ATT_HALO = 1024
ATT_QB = 128
ATT_SIDE = 64
ATT_PIPE = 8
MOE_ROWS = 512
GATHER_MIN_ROWS = 32768
VMEM_LIMIT = 56 * 1024 * 1024


def _dot(a, b):
    return jnp.dot(a, b, preferred_element_type=F32)


def _dot_nt(a, b):
    return lax.dot_general(a, b, (((1,), (1,)), ((), ())), preferred_element_type=F32)


def _rms(x, g):
    ms = jnp.mean(x * x, axis=-1, keepdims=True)
    return x * lax.rsqrt(ms + EPS) * g


def _gelu(x):
    return 0.5 * x * (1.0 + lax.erf(x * (1.0 / math.sqrt(2.0))))


def _split_bf16(x):
    hi = x.astype(BF16)
    lo = (x - hi.astype(F32)).astype(BF16)
    return hi, lo


def _inproj_kernel(n1, attn_w, sgu_w, xp_ref, xs_ref, norm1_ref, win_ref, gsum_ref, gexp_ref,
                   qkg_ref, sgun_ref, wsp_ref, bsb_ref, ona_ref, q_ref, k_ref, v_ref, a_ref):
    i = pl.program_id(0)
    x = jnp.where(i < n1, xp_ref[...], xs_ref[...])
    h = _rms(x, norm1_ref[...]).astype(BF16)

    zqk = _dot(h, win_ref[:, 0:2 * attn_w])
    sq_hi, sq_lo = _split_bf16(zqk * zqk)
    ss = _dot(sq_hi, gsum_ref[...]) + _dot(sq_lo, gsum_ref[...])
    inv = lax.rsqrt(ss * (1.0 / HEAD_DIM) + EPS)
    inv_hi, inv_lo = _split_bf16(inv)
    invb = _dot(inv_hi, gexp_ref[...]) + _dot(inv_lo, gexp_ref[...])
    qk = zqk * invb * qkg_ref[...]
    q_ref[...] = qk[:, 0:attn_w]
    k_ref[...] = qk[:, attn_w:2 * attn_w]
    v_ref[...] = _dot(h, win_ref[:, 2 * attn_w:3 * attn_w])

    c0 = 3 * attn_w
    u = _gelu(_dot(h, win_ref[:, c0:c0 + sgu_w]))
    gv = _gelu(_dot(h, win_ref[:, c0 + sgu_w:c0 + 2 * sgu_w]))
    vsn = _rms(gv, sgun_ref[...]).astype(BF16)
    lane = lax.broadcasted_iota(jnp.int32, (SGU_CHUNK, LANES), 1)
    lo_half = lane < HEAD_DIM
    zero = jnp.zeros((SGU_CHUNK, LANES), BF16)
    for c in range(TOK_TILE // SGU_CHUNK):
        r0 = c * SGU_CHUNK
        parts = []
        for j in range(sgu_w // LANES):
            blk = vsn[r0:r0 + SGU_CHUNK, j * LANES:(j + 1) * LANES]
            rhs = jnp.concatenate([jnp.where(lo_half, blk, zero), jnp.where(lo_half, zero, blk)], axis=0)
            parts.append(_dot(wsp_ref[j], rhs))
        s = jnp.concatenate(parts, axis=1) + bsb_ref[...]
        a = u[r0:r0 + SGU_CHUNK, :] * s
        a_ref[r0:r0 + SGU_CHUNK, :] = _rms(a, ona_ref[...]).astype(BF16)


def _inproj(xp, xs, norm1, w_in, q_gain, k_gain, sgu_norm, w_s, b_s, out_norm_a):
    tp, d_model = xp.shape
    ts = xs.shape[0]
    t = tp + ts
    n_heads_w = q_gain.shape[0]
    sgu_w = sgu_norm.shape[0]
    attn_w = (w_in.shape[1] - 2 * sgu_w) // 3
    n_heads = attn_w // n_heads_w
    n_groups = w_s.shape[0]
    assert n_heads_w == HEAD_DIM and sgu_w // n_groups == HEAD_DIM and w_s.shape[1] == SGU_CHUNK
    assert tp % TOK_TILE == 0 and ts % TOK_TILE == 0 and 2 * n_heads <= LANES
    n1 = tp // TOK_TILE

    heads = np.arange(2 * attn_w) // HEAD_DIM
    gsum = (heads[:, None] == np.arange(LANES)[None, :]).astype(np.float32)
    gexp = gsum.T
    qkg = jnp.concatenate([jnp.tile(q_gain, n_heads) * (HEAD_DIM ** -0.5), jnp.tile(k_gain, n_heads)])
    wsp = jnp.concatenate([w_s[0::2], w_s[1::2]], axis=2).astype(BF16)
    bsb = jnp.repeat(b_s.T, HEAD_DIM, axis=1)

    const = lambda shape: pl.BlockSpec(shape, lambda i: (0,) * len(shape))
    tok = lambda w: pl.BlockSpec((TOK_TILE, w), lambda i: (i, 0))
    return pl.pallas_call(
        functools.partial(_inproj_kernel, n1, attn_w, sgu_w),
        grid=(t // TOK_TILE,),
        in_specs=[
            pl.BlockSpec((TOK_TILE, d_model), lambda i: (jnp.minimum(i, n1 - 1), 0)),
            pl.BlockSpec((TOK_TILE, d_model), lambda i: (jnp.maximum(i - n1, 0), 0)),
            const((1, d_model)), const(w_in.shape), const((2 * attn_w, LANES)), const((LANES, 2 * attn_w)),
            const((1, 2 * attn_w)), const((1, sgu_w)), const(wsp.shape), const((SGU_CHUNK, sgu_w)),
            const((1, sgu_w)),
        ],
        out_specs=[tok(attn_w), tok(attn_w), tok(attn_w), tok(sgu_w)],
        out_shape=[jax.ShapeDtypeStruct((t, attn_w), F32)] * 3 + [jax.ShapeDtypeStruct((t, sgu_w), BF16)],
        compiler_params=pltpu.CompilerParams(dimension_semantics=("arbitrary",), vmem_limit_bytes=VMEM_LIMIT),
        name="inproj",
    )(xp, xs, norm1[None], w_in.astype(BF16), jnp.asarray(gsum, BF16), jnp.asarray(gexp, BF16),
      qkg[None], sgu_norm[None], wsp, bsb, out_norm_a[None])


def _t5_bucket(rel):
    nb = N_BUCKETS // 2
    bucket = (rel > 0).astype(np.int32) * nb
    n = np.abs(rel)
    max_exact = nb // 2
    large = max_exact + (np.log(np.maximum(n, 1) / max_exact)
                         / np.log(MAX_DISTANCE / max_exact) * (nb - max_exact)).astype(np.int32)
    large = np.minimum(large, nb - 1)
    return (bucket + np.where(n < max_exact, n, large)).astype(np.int32)


def _branch_layout():
    out, kbase = [], 0
    for bi, (window, dil) in enumerate(WINDOWS):
        assert window // (2 * dil) == ATT_SIDE
        m = ATT_TILE // dil
        seg = m + 2 * ATT_SIDE
        out.append((dil, m, seg, kbase, bi * ATT_TILE))
        kbase += dil * seg
    return out, kbase


def _strided(ref, start, size, stride):
    if stride == 1:
        return ref[start:start + size, :]
    return ref[pl.ds(start, size, stride=stride), :]


def _attn_kernel(tiles_p, per_seq_p, per_seq_s, q_ref, km_ref, kp_ref, kn_ref, vm_ref, vp_ref, vn_ref,
                 bias_ref, o_ref, qs, kts, vs, o_scr, d_scr, l_scr, s_scr, p_scr, m_scr, k4, v4, q4):
    i = pl.program_id(0)
    w = jnp.where(i < tiles_p, i % per_seq_p, (i - tiles_p) % per_seq_s)
    last = jnp.where(i < tiles_p, per_seq_p - 1, per_seq_s - 1)
    left_ok = w > 0
    right_ok = w < last
    layout, _ = _branch_layout()
    mid = WINDOWS[1][1]
    assert WINDOWS[2][1] == mid * mid
    halo4, main4 = ATT_HALO // mid, ATT_TILE // mid
    win4 = 2 * halo4 + main4

    @pl.when(jnp.logical_and(i == 0, pl.program_id(1) == 0))
    def _():
        vs[:, LANES:] = jnp.ones((vs.shape[0], LANES), BF16)

    for r in range(mid):
        for dst, main, prev, nxt in ((k4, km_ref, kp_ref, kn_ref), (v4, vm_ref, vp_ref, vn_ref)):
            dst[r * win4:r * win4 + halo4, :] = _strided(prev, r, halo4, mid)
            dst[r * win4 + halo4:r * win4 + halo4 + main4, :] = _strided(main, r, main4, mid)
            dst[r * win4 + halo4 + main4:(r + 1) * win4, :] = _strided(nxt, r, halo4, mid)
        q4[r * main4:(r + 1) * main4, :] = _strided(q_ref, r, main4, mid)

    def key_rows(src_k, src_v, start, stride, kbase_rows):
        kc = _strided(src_k, start, ATT_QB, stride)
        kts[kbase_rows // ATT_QB] = kc.T.astype(BF16)
        vs[kbase_rows:kbase_rows + ATT_QB, 0:LANES] = _strided(src_v, start, ATT_QB, stride).astype(BF16)

    for bi, (dil, m, seg, kbase, qbase) in enumerate(layout):
        nchunk = seg // ATT_QB
        for r in range(dil):
            o = kbase + r * seg
            if dil == 1:
                for c in range(nchunk):
                    lo = c * ATT_QB - ATT_SIDE
                    if c == 0:
                        kc = jnp.concatenate([kp_ref[ATT_HALO - ATT_SIDE:, :], km_ref[0:ATT_QB - ATT_SIDE, :]], axis=0)
                        vc = jnp.concatenate([vp_ref[ATT_HALO - ATT_SIDE:, :], vm_ref[0:ATT_QB - ATT_SIDE, :]], axis=0)
                    elif c == nchunk - 1:
                        kc = jnp.concatenate([km_ref[lo:, :], kn_ref[0:ATT_SIDE, :]], axis=0)
                        vc = jnp.concatenate([vm_ref[lo:, :], vn_ref[0:ATT_SIDE, :]], axis=0)
                    else:
                        kc = km_ref[lo:lo + ATT_QB, :]
                        vc = vm_ref[lo:lo + ATT_QB, :]
                    kts[(o + c * ATT_QB) // ATT_QB] = kc.T.astype(BF16)
                    vs[o + c * ATT_QB:o + (c + 1) * ATT_QB, 0:LANES] = vc.astype(BF16)
                qs[qbase:qbase + m, :] = q_ref[...].astype(BF16)
            else:
                r4, s = r % mid, r // mid
                step = dil // mid
                first = r4 * win4 + (halo4 - ATT_SIDE * step) + s
                for c in range(nchunk):
                    key_rows(k4, v4, first + c * ATT_QB * step, step, o + c * ATT_QB)
                qs[qbase + r * m:qbase + (r + 1) * m, :] = _strided(q4, r4 * main4 + s, m, step).astype(BF16)

    lane = lax.broadcasted_iota(jnp.int32, (ATT_QB, LANES), 1)
    head0 = lane < HEAD_DIM
    cells = ATT_TILE // ATT_QB
    total = len(layout) * cells
    blk_bits = [(m // ATT_QB).bit_length() - 1 for _, m, _, _, _ in layout]
    seg_chunks = [seg // ATT_QB for _, _, seg, _, _ in layout]
    base_chunks = [kbase // ATT_QB for _, _, _, kbase, _ in layout]

    def cell_params(g):
        bi = g // cells
        idx = g % cells
        if isinstance(g, int):
            pick = lambda vals: vals[bi]
        else:
            pick = lambda vals: jnp.where(bi == 0, vals[0], jnp.where(bi == 1, vals[1], vals[2]))
        bits = pick(blk_bits)
        last_blk = (1 << bits) - 1
        blk = idx & last_blk
        return bi, blk, last_blk, pick(base_chunks) + (idx >> bits) * pick(seg_chunks) + blk

    def stage_logits(g, slot):
        bi, blk, last_blk, kchunk = cell_params(g)
        left_bad = jnp.logical_and(blk == 0, jnp.logical_not(left_ok))
        right_bad = jnp.logical_and(blk == last_blk, jnp.logical_not(right_ok))
        variant = left_bad.astype(jnp.int32) + 2 * right_bad.astype(jnp.int32)
        qc = qs[pl.ds(pl.multiple_of(g * ATT_QB, ATT_QB), ATT_QB), :]
        zero = jnp.zeros_like(qc)
        lhs = jnp.concatenate([jnp.where(head0, qc, zero), jnp.where(head0, zero, qc)], axis=0)
        kt = jnp.concatenate([kts[kchunk], kts[kchunk + 1]], axis=1)
        s_scr[slot] = _dot(lhs, kt) + bias_ref[bi, variant]

    def stage_softmax(slot):
        part = ATT_QB // 2
        for c in range(2 * ATT_QB // part):
            s = s_scr[slot, c * part:(c + 1) * part, :]
            mx = jnp.max(s, axis=-1, keepdims=True)
            p_scr[slot, c * part:(c + 1) * part, :] = jnp.exp(s - mx).astype(BF16)
            m_scr[slot, c * part:(c + 1) * part, :] = jnp.broadcast_to(mx, (part, LANES))

    def stage_values(g, slot):
        _, _, _, kchunk = cell_params(g)
        koff = pl.multiple_of(kchunk * ATT_QB, ATT_QB)
        r = _dot(p_scr[slot], vs[pl.ds(koff, 2 * ATT_QB), :])
        num = jnp.where(head0, r[0:ATT_QB, 0:LANES], r[ATT_QB:, 0:LANES])
        den = jnp.where(head0, r[0:ATT_QB, LANES:], r[ATT_QB:, LANES:])
        mx = jnp.where(head0, m_scr[slot, 0:ATT_QB, :], m_scr[slot, ATT_QB:, :])
        rows = pl.ds(pl.multiple_of(g * ATT_QB, ATT_QB), ATT_QB)
        o_scr[rows, :] = num
        d_scr[rows, :] = den
        l_scr[rows, :] = mx

    width = ATT_PIPE

    def group(c, do_values, do_softmax, do_logits):
        if do_values:
            for u in range(width):
                stage_values(c - width + u, u)
        if do_softmax:
            for u in range(width):
                stage_softmax(u)
        if do_logits:
            for u in range(width):
                stage_logits(c + width + u, u)

    group(-width, False, False, True)
    group(0, False, True, True)

    def trip(t, carry):
        group((t + 1) * width, True, True, True)
        return carry

    lax.fori_loop(0, total // width - 2, trip, 0)
    group(total - width, True, True, False)
    group(total, True, False, False)

    big = WINDOWS[-1][1]
    rows = ATT_TILE // big
    dil0, m0, _, _, qb0 = layout[0]
    assert dil0 == 1 and big == mid * mid
    for r4 in range(mid):
        k4[r4 * main4:(r4 + 1) * main4, :] = _strided(o_scr, qb0 + r4, main4, mid)
        k4[m0 + r4 * main4:m0 + (r4 + 1) * main4, :] = _strided(d_scr, qb0 + r4, main4, mid)
        v4[r4 * main4:(r4 + 1) * main4, :] = _strided(l_scr, qb0 + r4, main4, mid)
    _, m1, _, _, qb1 = layout[1]
    _, m2, _, _, qb2 = layout[2]
    for r in range(big):
        r4, s = r % mid, r // mid
        first = r4 * main4 + s
        nums = [_strided(k4, first, rows, mid), _strided(o_scr, qb1 + r4 * m1 + s, rows, mid),
                o_scr[qb2 + r * m2:qb2 + (r + 1) * m2, :]]
        dens = [_strided(k4, m0 + first, rows, mid), _strided(d_scr, qb1 + r4 * m1 + s, rows, mid),
                d_scr[qb2 + r * m2:qb2 + (r + 1) * m2, :]]
        mxs = [_strided(v4, first, rows, mid), _strided(l_scr, qb1 + r4 * m1 + s, rows, mid),
               l_scr[qb2 + r * m2:qb2 + (r + 1) * m2, :]]
        top = jnp.maximum(jnp.maximum(mxs[0], mxs[1]), mxs[2])
        ws = [jnp.exp(mx - top) for mx in mxs]
        num = ws[0] * nums[0] + ws[1] * nums[1] + ws[2] * nums[2]
        den = ws[0] * dens[0] + ws[1] * dens[1] + ws[2] * dens[2]
        q4[pl.ds(first, rows, stride=mid), :] = num / den
    for r4 in range(mid):
        o_ref[pl.ds(r4, main4, stride=mid), :] = q4[r4 * main4:(r4 + 1) * main4, :]


def _attention(q, k, v, rel_bias, tp, seq_p, seq_s):
    t, attn_w = q.shape
    n_heads = attn_w // HEAD_DIM
    pair = LANES // HEAD_DIM
    assert seq_p % ATT_TILE == 0 and seq_s % ATT_TILE == 0 and ATT_TILE == 2 * ATT_HALO
    assert ATT_TILE // WINDOWS[-1][1] == ATT_QB and pair == 2
    layout, krows = _branch_layout()

    nb = len(WINDOWS)
    span = 2 * ATT_SIDE + 1
    jj = np.arange(2 * ATT_QB)[None, :]
    offsets = np.arange(span) - ATT_SIDE
    table_bucket = np.stack([_t5_bucket(offsets * dil) for _, dil in WINDOWS])
    table = jnp.zeros((n_heads, nb, span), F32)
    for b in range(N_BUCKETS):
        table = jnp.where(jnp.asarray(table_bucket == b)[None], rel_bias[b].astype(F32)[:, None, None], table)
    row_len = 3 * ATT_QB
    period = row_len + 1
    padded = jnp.concatenate([table, jnp.full((n_heads, nb, period - span), NEG_INF, F32)], axis=-1)
    flat = jnp.tile(padded, (1, 1, ATT_QB + 1))[..., :ATT_QB * row_len]
    bias = flat.reshape(n_heads, nb, ATT_QB, row_len)[..., :2 * ATT_QB]
    bias = bias.reshape(n_heads // pair, pair, nb, ATT_QB, 2 * ATT_QB).transpose(0, 2, 1, 3, 4)
    bias = bias.reshape(n_heads // pair, nb, 1, pair * ATT_QB, 2 * ATT_QB)
    left = jnp.asarray((jj < ATT_SIDE)[None, None, None] & (np.arange(4) % 2 == 1)[None, None, :, None, None])
    right = jnp.asarray((jj >= 2 * ATT_QB - ATT_SIDE)[None, None, None]
                        & (np.arange(4) // 2 == 1)[None, None, :, None, None])
    bias = jnp.where(left | right, NEG_INF, bias)

    halo_blocks = t // ATT_HALO
    per_tile = ATT_TILE // ATT_HALO
    mid = WINDOWS[1][1]
    main = pl.BlockSpec((ATT_TILE, LANES), lambda i, j: (i, j))
    prev = pl.BlockSpec((ATT_HALO, LANES), lambda i, j: (jnp.maximum(i * per_tile - 1, 0), j))
    nxt = pl.BlockSpec((ATT_HALO, LANES), lambda i, j: (jnp.minimum((i + 1) * per_tile, halo_blocks - 1), j))
    n_q = nb * ATT_TILE
    return pl.pallas_call(
        functools.partial(_attn_kernel, tp // ATT_TILE, seq_p // ATT_TILE, seq_s // ATT_TILE),
        grid=(t // ATT_TILE, attn_w // LANES),
        in_specs=[main, main, prev, nxt, main, prev, nxt,
                  pl.BlockSpec((None, nb, 4, pair * ATT_QB, 2 * ATT_QB), lambda i, j: (j, 0, 0, 0, 0))],
        out_specs=main,
        out_shape=jax.ShapeDtypeStruct((t, attn_w), F32),
        scratch_shapes=[pltpu.VMEM((n_q, LANES), BF16),
                        pltpu.VMEM((krows // ATT_QB, LANES, ATT_QB), BF16),
                        pltpu.VMEM((krows, 2 * LANES), BF16),
                        pltpu.VMEM((n_q, LANES), F32), pltpu.VMEM((n_q, LANES), F32),
                        pltpu.VMEM((n_q, LANES), F32),
                        pltpu.VMEM((ATT_PIPE, pair * ATT_QB, 2 * ATT_QB), F32),
                        pltpu.VMEM((ATT_PIPE, pair * ATT_QB, 2 * ATT_QB), BF16),
                        pltpu.VMEM((ATT_PIPE, pair * ATT_QB, LANES), F32),
                        pltpu.VMEM(((ATT_TILE + 2 * ATT_HALO), LANES), F32),
                        pltpu.VMEM(((ATT_TILE + 2 * ATT_HALO), LANES), F32),
                        pltpu.VMEM((ATT_TILE, LANES), F32)],
        compiler_params=pltpu.CompilerParams(dimension_semantics=("arbitrary", "arbitrary"),
                                             vmem_limit_bytes=VMEM_LIMIT),
        name="attn",
    )(q, k, k, k, v, v, v, bias)


def _outproj_kernel(sgu_w, n_exp, h2_init_ref, a_ref, attn_ref, x_ref, onb_ref, wo_ref, norm2_ref, wrh_ref, wrl_ref,
                    br_ref, tri_ref, x1_ref, h2_ref, meta_ref, cnt_ref, route_ref, carry):
    del h2_init_ref
    i = pl.program_id(0)

    @pl.when(i == 0)
    def _():
        carry[...] = jnp.zeros_like(carry)

    bn = _rms(attn_ref[...], onb_ref[...]).astype(BF16)
    x1 = x_ref[...] + _dot(a_ref[...], wo_ref[0:sgu_w, :]) + _dot(bn, wo_ref[sgu_w:, :])
    x1_ref[...] = x1
    h2 = _rms(x1, norm2_ref[...])
    h2_ref[...] = h2.astype(BF16)
    hi, lo = _split_bf16(h2)
    logits = _dot(hi, wrh_ref[...]) + _dot(lo, wrh_ref[...]) + _dot(hi, wrl_ref[...]) + br_ref[...]

    lane = lax.broadcasted_iota(jnp.int32, logits.shape, 1)
    work = jnp.where(lane < n_exp, logits, -jnp.inf)
    chosen = jnp.zeros(logits.shape, jnp.bool_)
    experts, values = [], []
    for _ in range(TOP_K):
        top = jnp.max(work, axis=-1, keepdims=True)
        idx = jnp.min(jnp.where(work == top, lane, LANES), axis=-1, keepdims=True)
        hit = lane == idx
        chosen = jnp.logical_or(chosen, hit)
        work = jnp.where(hit, -jnp.inf, work)
        experts.append(idx)
        values.append(top)
    exps = [jnp.exp(v - values[0]) for v in values]
    den = exps[0] + exps[1] + exps[2] + exps[3]

    onehot = jnp.where(chosen, 1.0, 0.0)
    before = _dot(tri_ref[...], onehot.astype(BF16)) + carry[0:1, :]
    carry[0:1, :] = carry[0:1, :] + jnp.sum(onehot, axis=0, keepdims=True)
    cnt_ref[...] = carry[...]

    meta = jnp.zeros(logits.shape, F32)
    for kk in range(TOP_K):
        rank = jnp.sum(jnp.where(lane == experts[kk], before, 0.0), axis=-1, keepdims=True)
        meta = jnp.where(lane == kk, experts[kk].astype(F32), meta)
        meta = jnp.where(lane == TOP_K + kk, exps[kk] / den, meta)
        meta = jnp.where(lane == 2 * TOP_K + kk, rank, meta)
    meta_ref[...] = meta
    route_ref[...] = meta.T[0:route_ref.shape[0], :]


def _outproj(a_n, attn, x, row0, out_norm_b, w_o, norm2, w_router, b_router):
    sgu_w = a_n.shape[1]
    attn_w = attn.shape[1]
    t, d_model = x.shape
    n_exp = w_router.shape[1]
    assert n_exp <= LANES and row0 % TOK_TILE == 0 and t % TOK_TILE == 0
    b0 = row0 // TOK_TILE
    wr = jnp.pad(w_router, ((0, 0), (0, LANES - n_exp)))
    wr_hi = wr.astype(BF16)
    wr_lo = (wr - wr_hi.astype(F32)).astype(BF16)
    br = jnp.pad(b_router, (0, LANES - n_exp))[None]
    tri = np.tril(np.ones((TOK_TILE, TOK_TILE), np.float32), -1)
    h2_rows = max(t, GATHER_MIN_ROWS)
    padded = h2_rows > t
    h2_init = jnp.zeros((h2_rows, d_model) if padded else (8, LANES), BF16)

    const = lambda shape: pl.BlockSpec(shape, lambda i: (0,) * len(shape))
    tok = lambda w: pl.BlockSpec((TOK_TILE, w), lambda i: (i, 0))
    src = lambda w: pl.BlockSpec((TOK_TILE, w), lambda i: (i + b0, 0))
    return pl.pallas_call(
        functools.partial(_outproj_kernel, sgu_w, n_exp),
        grid=(t // TOK_TILE,),
        in_specs=[
            pl.BlockSpec(memory_space=pl.ANY),
            src(sgu_w), src(attn_w), tok(d_model),
            const((1, attn_w)), const(w_o.shape), const((1, d_model)), const((d_model, LANES)),
            const((d_model, LANES)), const((1, LANES)), const((TOK_TILE, TOK_TILE)),
        ],
        out_specs=[tok(d_model), tok(d_model), tok(LANES), const((8, LANES)),
                   pl.BlockSpec((4 * TOP_K, TOK_TILE), lambda i: (0, i))],
        out_shape=[jax.ShapeDtypeStruct((t, d_model), F32), jax.ShapeDtypeStruct((h2_rows, d_model), BF16),
                   jax.ShapeDtypeStruct((t, LANES), F32), jax.ShapeDtypeStruct((8, LANES), F32),
                   jax.ShapeDtypeStruct((4 * TOP_K, t), F32)],
        scratch_shapes=[pltpu.VMEM((8, LANES), F32)],
        input_output_aliases={0: 1} if padded else {},
        compiler_params=pltpu.CompilerParams(dimension_semantics=("arbitrary",), vmem_limit_bytes=VMEM_LIMIT),
        name="outproj",
    )(h2_init, a_n, attn, x, out_norm_b[None], w_o.astype(BF16), norm2[None], wr_hi, wr_lo, br,
      jnp.asarray(tri, BF16))


def _moe_kernel(d_exp, blk_ref, exp_ref, lo_ref, hi_ref, x_ref, wgu_ref, bgu_ref, wd_ref, bd_ref, y_ref,
                wgu_s, wd_s, act_s):
    it = pl.program_id(0)
    prev = jnp.maximum(it - 1, 0)
    new_expert = jnp.logical_or(it == 0, exp_ref[it] != exp_ref[prev])
    new_block = jnp.logical_or(it == 0, blk_ref[it] != blk_ref[prev])
    lo, hi = lo_ref[it], hi_ref[it]
    active = hi > lo

    @pl.when(jnp.logical_and(new_expert, active))
    def _():
        wgu_s[...] = wgu_ref[...].astype(BF16)
        wd_s[...] = wd_ref[...].astype(BF16)

    @pl.when(active)
    def _():
        x = x_ref[...]
        step = 512
        for n in range(0, d_exp, step):
            gate = _dot(x, wgu_s[:, n:n + step]) + bgu_ref[:, n:n + step]
            up = _dot(x, wgu_s[:, d_exp + n:d_exp + n + step]) + bgu_ref[:, d_exp + n:d_exp + n + step]
            gate = jnp.minimum(gate, SWIGLU_LIMIT)
            up = jnp.clip(up, -SWIGLU_LIMIT, SWIGLU_LIMIT)
            glu = gate * jax.nn.sigmoid(SWIGLU_ALPHA * gate)
            act_s[:, n:n + step] = ((up + 1.0) * glu).astype(BF16)
        y = (_dot(act_s[...], wd_s[...]) + bd_ref[...]).astype(y_ref.dtype)
        row = lax.broadcasted_iota(jnp.int32, (MOE_ROWS, 1), 0)
        mine = jnp.logical_and(row >= lo, row < hi)

        @pl.when(new_block)
        def _():
            y_ref[...] = jnp.where(mine, y, jnp.zeros_like(y))

        @pl.when(jnp.logical_not(new_block))
        def _():
            y_ref[...] = jnp.where(mine, y, y_ref[...])


def _moe_experts(x_rows, items, w_gu, b_gu, w_down, b_down):
    n_rows, d_model = x_rows.shape
    n_exp, _, two_de = w_gu.shape
    d_exp = two_de // 2
    n_items = items[0].shape[0]
    grid_spec = pltpu.PrefetchScalarGridSpec(
        num_scalar_prefetch=4,
        grid=(n_items,),
        in_specs=[
            pl.BlockSpec((MOE_ROWS, d_model), lambda i, blk, ex, lo, hi: (blk[i], 0)),
            pl.BlockSpec((None, d_model, two_de), lambda i, blk, ex, lo, hi: (ex[i], 0, 0)),
            pl.BlockSpec((None, 1, two_de), lambda i, blk, ex, lo, hi: (ex[i], 0, 0)),
            pl.BlockSpec((None, d_exp, d_model), lambda i, blk, ex, lo, hi: (ex[i], 0, 0)),
            pl.BlockSpec((None, 1, d_model), lambda i, blk, ex, lo, hi: (ex[i], 0, 0)),
        ],
        out_specs=pl.BlockSpec((MOE_ROWS, d_model), lambda i, blk, ex, lo, hi: (blk[i], 0)),
        scratch_shapes=[pltpu.VMEM((d_model, two_de), BF16), pltpu.VMEM((d_exp, d_model), BF16),
                        pltpu.VMEM((MOE_ROWS, d_exp), BF16)],
    )
    return pl.pallas_call(
        functools.partial(_moe_kernel, d_exp),
        grid_spec=grid_spec,
        out_shape=jax.ShapeDtypeStruct((n_rows, d_model), BF16),
        compiler_params=pltpu.CompilerParams(dimension_semantics=("arbitrary",), vmem_limit_bytes=VMEM_LIMIT),
        name="moe",
    )(*items, x_rows, w_gu, b_gu[:, None, :], w_down, b_down[:, None, :])


def _route(meta, counts_f, n_exp):
    t = meta.shape[1]
    n_assign = t * TOP_K
    assert n_assign % MOE_ROWS == 0
    top_e = meta[0:TOP_K].astype(jnp.int32)
    rank = meta[2 * TOP_K:3 * TOP_K].astype(jnp.int32)
    counts = counts_f[0, :n_exp].astype(jnp.int32)
    end = jnp.cumsum(counts)
    start = end - counts
    experts = jnp.arange(n_exp, dtype=jnp.int32)
    start_of = jnp.zeros_like(top_e)
    for e in range(n_exp):
        start_of = jnp.where(top_e == e, start[e], start_of)
    pos = (start_of + rank).reshape(-1)

    shift = (n_assign - 1).bit_length()
    assert (n_exp << shift) < 2 ** 31
    assign = jnp.arange(t, dtype=jnp.int32)[None, :] * TOP_K + jnp.arange(TOP_K, dtype=jnp.int32)[:, None]
    keys = ((top_e << shift) + assign).reshape(-1)
    sorted_tok = (jnp.sort(keys) & ((1 << shift) - 1)) // TOP_K

    n_blocks = n_assign // MOE_ROWS
    n_items = n_blocks + n_exp
    first_blk = start // MOE_ROWS
    n_it = jnp.where(counts > 0, (end - 1) // MOE_ROWS - first_blk + 1, 0)
    it_end = jnp.cumsum(n_it)
    it_start = it_end - n_it
    i = jnp.arange(n_items, dtype=jnp.int32)
    valid = i < it_end[-1]
    e_i = jnp.sum(jnp.minimum(i, it_end[-1] - 1)[:, None] >= it_end[None, :], axis=1).astype(jnp.int32)
    e_i = jnp.minimum(e_i, n_exp - 1)
    pick = lambda table: jnp.sum(jnp.where(e_i[:, None] == experts[None, :], table[None, :], 0), axis=1)
    blk_i = jnp.where(valid, pick(first_blk) + i - pick(it_start), n_blocks - 1)
    lo_i = jnp.where(valid, jnp.clip(pick(start) - blk_i * MOE_ROWS, 0, MOE_ROWS), 0)
    hi_i = jnp.where(valid, jnp.clip(pick(end) - blk_i * MOE_ROWS, 0, MOE_ROWS), 0)
    items = tuple(a.astype(jnp.int32) for a in (blk_i, e_i, lo_i, hi_i))
    return pos, sorted_tok, items


def _combine_kernel(x1_ref, yg_ref, meta_ref, o_ref):
    acc = x1_ref[...]
    for kk in range(TOP_K):
        gate = meta_ref[:, TOP_K + kk:TOP_K + kk + 1]
        acc = acc + gate * yg_ref[kk].astype(F32)
    o_ref[...] = acc


def _combine(x1, yg, meta):
    rows, d_model = x1.shape
    tok = lambda w: pl.BlockSpec((TOK_TILE, w), lambda i: (i, 0))
    return pl.pallas_call(
        _combine_kernel,
        grid=(rows // TOK_TILE,),
        in_specs=[tok(d_model), pl.BlockSpec((TOP_K, TOK_TILE, d_model), lambda i: (0, i, 0)), tok(LANES)],
        out_specs=tok(d_model),
        out_shape=jax.ShapeDtypeStruct((rows, d_model), F32),
        compiler_params=pltpu.CompilerParams(dimension_semantics=("arbitrary",), vmem_limit_bytes=VMEM_LIMIT),
        name="combine",
    )(x1, yg, meta)


def kernel(x_prompt, x_sample, norm1, w_in, q_gain, k_gain, rel_bias, sgu_norm, w_s, b_s, out_norm_a, out_norm_b,
           w_o, norm2, w_router, b_router, w_gu, b_gu, w_down, b_down):
    assert norm1.shape[0] == 1, "single-layer trunk"
    bp, sp, d_model = x_prompt.shape
    bs, ss, _ = x_sample.shape
    xp = x_prompt.reshape(bp * sp, d_model)
    xs = x_sample.reshape(bs * ss, d_model)
    tp, t = bp * sp, bp * sp + bs * ss

    q, k, v, a_n = _inproj(xp, xs, norm1[0], w_in[0], q_gain[0], k_gain[0], sgu_norm[0], w_s[0], b_s[0],
                           out_norm_a[0])
    attn = _attention(q, k, v, rel_bias, tp, sp, ss)

    outs = {}
    after = None
    for x, row0 in sorted(((xp, 0), (xs, tp)), key=lambda c: -c[0].shape[0]):
        rows = x.shape[0]
        if after is not None:
            x, _ = lax.optimization_barrier((x, after[0]))
        x1, h2, meta, counts, route = _outproj(a_n, attn, x, row0, out_norm_b[0], w_o[0], norm2[0], w_router[0],
                                               b_router[0])
        pos, sorted_tok, items = _route(route, counts, w_router.shape[-1])
        x_rows = h2.at[sorted_tok].get(mode="promise_in_bounds")
        if after is not None:
            x_rows, _ = lax.optimization_barrier((x_rows, after[1]))
        y_rows = _moe_experts(x_rows, items, w_gu[0], b_gu[0], w_down[0], b_down[0])
        yg = y_rows.at[pos].get(mode="promise_in_bounds").reshape(TOP_K, rows, d_model)
        if after is not None:
            yg, _ = lax.optimization_barrier((yg, after[2]))
        outs[row0] = _combine(x1, yg, meta)
        after = (counts, y_rows, outs[row0])
    return (outs[0].reshape(bp, sp, d_model), outs[tp].reshape(bs, ss, d_model))
```

```python
import functools
import math

import numpy as np
import jax
import jax.numpy as jnp
from jax import lax
from jax.experimental import pallas as pl
from jax.experimental.pallas import tpu as pltpu

F32 = jnp.float32
BF16 = jnp.bfloat16

EPS = 1e-6
NEG_INF = -1e30
HEAD_DIM = 64
SGU_CHUNK = 128
WINDOWS = ((128, 1), (512, 4), (2048, 16))
N_BUCKETS = 32
MAX_DISTANCE = 1024
TOP_K = 4
SWIGLU_LIMIT = 7.0
SWIGLU_ALPHA = 1.702

LANES = 128
TOK_TILE = 512
ATT_TILE = 2048
ATT_HALO = 1024
ATT_QB = 128
ATT_SIDE = 64
ATT_PIPE = 8
MOE_ROWS = 512
GATHER_MIN_ROWS = 32768
VMEM_LIMIT = 56 * 1024 * 1024


def _dot(a, b):
    return jnp.dot(a, b, preferred_element_type=F32)


def _dot_nt(a, b):
    return lax.dot_general(a, b, (((1,), (1,)), ((), ())), preferred_element_type=F32)


def _rms(x, g):
    ms = jnp.mean(x * x, axis=-1, keepdims=True)
    return x * lax.rsqrt(ms + EPS) * g


def _gelu(x):
    return 0.5 * x * (1.0 + lax.erf(x * (1.0 / math.sqrt(2.0))))


def _split_bf16(x):
    hi = x.astype(BF16)
    lo = (x - hi.astype(F32)).astype(BF16)
    return hi, lo


def _inproj_kernel(n1, attn_w, sgu_w, xp_ref, xs_ref, norm1_ref, win_ref, gsum_ref, gexp_ref,
                   qkg_ref, sgun_ref, wsp_ref, bsb_ref, ona_ref, q_ref, k_ref, v_ref, a_ref):
    i = pl.program_id(0)
    x = jnp.where(i < n1, xp_ref[...], xs_ref[...])
    h = _rms(x, norm1_ref[...]).astype(BF16)

    zqk = _dot(h, win_ref[:, 0:2 * attn_w])
    sq_hi, sq_lo = _split_bf16(zqk * zqk)
    ss = _dot(sq_hi, gsum_ref[...]) + _dot(sq_lo, gsum_ref[...])
    inv = lax.rsqrt(ss * (1.0 / HEAD_DIM) + EPS)
    inv_hi, inv_lo = _split_bf16(inv)
    invb = _dot(inv_hi, gexp_ref[...]) + _dot(inv_lo, gexp_ref[...])
    qk = zqk * invb * qkg_ref[...]
    q_ref[...] = qk[:, 0:attn_w]
    k_ref[...] = qk[:, attn_w:2 * attn_w]
    v_ref[...] = _dot(h, win_ref[:, 2 * attn_w:3 * attn_w])

    c0 = 3 * attn_w
    u = _gelu(_dot(h, win_ref[:, c0:c0 + sgu_w]))
    gv = _gelu(_dot(h, win_ref[:, c0 + sgu_w:c0 + 2 * sgu_w]))
    vsn = _rms(gv, sgun_ref[...]).astype(BF16)
    lane = lax.broadcasted_iota(jnp.int32, (SGU_CHUNK, LANES), 1)
    lo_half = lane < HEAD_DIM
    zero = jnp.zeros((SGU_CHUNK, LANES), BF16)
    for c in range(TOK_TILE // SGU_CHUNK):
        r0 = c * SGU_CHUNK
        parts = []
        for j in range(sgu_w // LANES):
            blk = vsn[r0:r0 + SGU_CHUNK, j * LANES:(j + 1) * LANES]
            rhs = jnp.concatenate([jnp.where(lo_half, blk, zero), jnp.where(lo_half, zero, blk)], axis=0)
            parts.append(_dot(wsp_ref[j], rhs))
        s = jnp.concatenate(parts, axis=1) + bsb_ref[...]
        a = u[r0:r0 + SGU_CHUNK, :] * s
        a_ref[r0:r0 + SGU_CHUNK, :] = _rms(a, ona_ref[...]).astype(BF16)


def _inproj(xp, xs, norm1, w_in, q_gain, k_gain, sgu_norm, w_s, b_s, out_norm_a):
    tp, d_model = xp.shape
    ts = xs.shape[0]
    t = tp + ts
    n_heads_w = q_gain.shape[0]
    sgu_w = sgu_norm.shape[0]
    attn_w = (w_in.shape[1] - 2 * sgu_w) // 3
    n_heads = attn_w // n_heads_w
    n_groups = w_s.shape[0]
    assert n_heads_w == HEAD_DIM and sgu_w // n_groups == HEAD_DIM and w_s.shape[1] == SGU_CHUNK
    assert tp % TOK_TILE == 0 and ts % TOK_TILE == 0 and 2 * n_heads <= LANES
    n1 = tp // TOK_TILE

    heads = np.arange(2 * attn_w) // HEAD_DIM
    gsum = (heads[:, None] == np.arange(LANES)[None, :]).astype(np.float32)
    gexp = gsum.T
    qkg = jnp.concatenate([jnp.tile(q_gain, n_heads) * (HEAD_DIM ** -0.5), jnp.tile(k_gain, n_heads)])
    wsp = jnp.concatenate([w_s[0::2], w_s[1::2]], axis=2).astype(BF16)
    bsb = jnp.repeat(b_s.T, HEAD_DIM, axis=1)

    const = lambda shape: pl.BlockSpec(shape, lambda i: (0,) * len(shape))
    tok = lambda w: pl.BlockSpec((TOK_TILE, w), lambda i: (i, 0))
    return pl.pallas_call(
        functools.partial(_inproj_kernel, n1, attn_w, sgu_w),
        grid=(t // TOK_TILE,),
        in_specs=[
            pl.BlockSpec((TOK_TILE, d_model), lambda i: (jnp.minimum(i, n1 - 1), 0)),
            pl.BlockSpec((TOK_TILE, d_model), lambda i: (jnp.maximum(i - n1, 0), 0)),
            const((1, d_model)), const(w_in.shape), const((2 * attn_w, LANES)), const((LANES, 2 * attn_w)),
            const((1, 2 * attn_w)), const((1, sgu_w)), const(wsp.shape), const((SGU_CHUNK, sgu_w)),
            const((1, sgu_w)),
        ],
        out_specs=[tok(attn_w), tok(attn_w), tok(attn_w), tok(sgu_w)],
        out_shape=[jax.ShapeDtypeStruct((t, attn_w), F32)] * 3 + [jax.ShapeDtypeStruct((t, sgu_w), BF16)],
        compiler_params=pltpu.CompilerParams(dimension_semantics=("arbitrary",), vmem_limit_bytes=VMEM_LIMIT),
        name="inproj",
    )(xp, xs, norm1[None], w_in.astype(BF16), jnp.asarray(gsum, BF16), jnp.asarray(gexp, BF16),
      qkg[None], sgu_norm[None], wsp, bsb, out_norm_a[None])


def _t5_bucket(rel):
    nb = N_BUCKETS // 2
    bucket = (rel > 0).astype(np.int32) * nb
    n = np.abs(rel)
    max_exact = nb // 2
    large = max_exact + (np.log(np.maximum(n, 1) / max_exact)
                         / np.log(MAX_DISTANCE / max_exact) * (nb - max_exact)).astype(np.int32)
    large = np.minimum(large, nb - 1)
    return (bucket + np.where(n < max_exact, n, large)).astype(np.int32)


def _branch_layout():
    out, kbase = [], 0
    for bi, (window, dil) in enumerate(WINDOWS):
        assert window // (2 * dil) == ATT_SIDE
        m = ATT_TILE // dil
        seg = m + 2 * ATT_SIDE
        out.append((dil, m, seg, kbase, bi * ATT_TILE))
        kbase += dil * seg
    return out, kbase


def _strided(ref, start, size, stride):
    if stride == 1:
        return ref[start:start + size, :]
    return ref[pl.ds(start, size, stride=stride), :]


def _attn_kernel(tiles_p, per_seq_p, per_seq_s, q_ref, km_ref, kp_ref, kn_ref, vm_ref, vp_ref, vn_ref,
                 bias_ref, o_ref, qs, kts, vs, o_scr, d_scr, l_scr, s_scr, p_scr, m_scr, k4, v4, q4):
    i = pl.program_id(0)
    w = jnp.where(i < tiles_p, i % per_seq_p, (i - tiles_p) % per_seq_s)
    last = jnp.where(i < tiles_p, per_seq_p - 1, per_seq_s - 1)
    left_ok = w > 0
    right_ok = w < last
    layout, _ = _branch_layout()
    mid = WINDOWS[1][1]
    assert WINDOWS[2][1] == mid * mid
    halo4, main4 = ATT_HALO // mid, ATT_TILE // mid
    win4 = 2 * halo4 + main4

    @pl.when(jnp.logical_and(i == 0, pl.program_id(1) == 0))
    def _():
        vs[:, LANES:] = jnp.ones((vs.shape[0], LANES), BF16)

    for r in range(mid):
        for dst, main, prev, nxt in ((k4, km_ref, kp_ref, kn_ref), (v4, vm_ref, vp_ref, vn_ref)):
            dst[r * win4:r * win4 + halo4, :] = _strided(prev, r, halo4, mid)
            dst[r * win4 + halo4:r * win4 + halo4 + main4, :] = _strided(main, r, main4, mid)
            dst[r * win4 + halo4 + main4:(r + 1) * win4, :] = _strided(nxt, r, halo4, mid)
        q4[r * main4:(r + 1) * main4, :] = _strided(q_ref, r, main4, mid)

    def key_rows(src_k, src_v, start, stride, kbase_rows):
        kc = _strided(src_k, start, ATT_QB, stride)
        kts[kbase_rows // ATT_QB] = kc.T.astype(BF16)
        vs[kbase_rows:kbase_rows + ATT_QB, 0:LANES] = _strided(src_v, start, ATT_QB, stride).astype(BF16)

    for bi, (dil, m, seg, kbase, qbase) in enumerate(layout):
        nchunk = seg // ATT_QB
        for r in range(dil):
            o = kbase + r * seg
            if dil == 1:
                for c in range(nchunk):
                    lo = c * ATT_QB - ATT_SIDE
                    if c == 0:
                        kc = jnp.concatenate([kp_ref[ATT_HALO - ATT_SIDE:, :], km_ref[0:ATT_QB - ATT_SIDE, :]], axis=0)
                        vc = jnp.concatenate([vp_ref[ATT_HALO - ATT_SIDE:, :], vm_ref[0:ATT_QB - ATT_SIDE, :]], axis=0)
                    elif c == nchunk - 1:
                        kc = jnp.concatenate([km_ref[lo:, :], kn_ref[0:ATT_SIDE, :]], axis=0)
                        vc = jnp.concatenate([vm_ref[lo:, :], vn_ref[0:ATT_SIDE, :]], axis=0)
                    else:
                        kc = km_ref[lo:lo + ATT_QB, :]
                        vc = vm_ref[lo:lo + ATT_QB, :]
                    kts[(o + c * ATT_QB) // ATT_QB] = kc.T.astype(BF16)
                    vs[o + c * ATT_QB:o + (c + 1) * ATT_QB, 0:LANES] = vc.astype(BF16)
                qs[qbase:qbase + m, :] = q_ref[...].astype(BF16)
            else:
                r4, s = r % mid, r // mid
                step = dil // mid
                first = r4 * win4 + (halo4 - ATT_SIDE * step) + s
                for c in range(nchunk):
                    key_rows(k4, v4, first + c * ATT_QB * step, step, o + c * ATT_QB)
                qs[qbase + r * m:qbase + (r + 1) * m, :] = _strided(q4, r4 * main4 + s, m, step).astype(BF16)

    lane = lax.broadcasted_iota(jnp.int32, (ATT_QB, LANES), 1)
    head0 = lane < HEAD_DIM
    cells = ATT_TILE // ATT_QB
    total = len(layout) * cells
    blk_bits = [(m // ATT_QB).bit_length() - 1 for _, m, _, _, _ in layout]
    seg_chunks = [seg // ATT_QB for _, _, seg, _, _ in layout]
    base_chunks = [kbase // ATT_QB for _, _, _, kbase, _ in layout]

    def cell_params(g):
        bi = g // cells
        idx = g % cells
        if isinstance(g, int):
            pick = lambda vals: vals[bi]
        else:
            pick = lambda vals: jnp.where(bi == 0, vals[0], jnp.where(bi == 1, vals[1], vals[2]))
        bits = pick(blk_bits)
        last_blk = (1 << bits) - 1
        blk = idx & last_blk
        return bi, blk, last_blk, pick(base_chunks) + (idx >> bits) * pick(seg_chunks) + blk

    def stage_logits(g, slot):
        bi, blk, last_blk, kchunk = cell_params(g)
        left_bad = jnp.logical_and(blk == 0, jnp.logical_not(left_ok))
        right_bad = jnp.logical_and(blk == last_blk, jnp.logical_not(right_ok))
        variant = left_bad.astype(jnp.int32) + 2 * right_bad.astype(jnp.int32)
        qc = qs[pl.ds(pl.multiple_of(g * ATT_QB, ATT_QB), ATT_QB), :]
        zero = jnp.zeros_like(qc)
        lhs = jnp.concatenate([jnp.where(head0, qc, zero), jnp.where(head0, zero, qc)], axis=0)
        kt = jnp.concatenate([kts[kchunk], kts[kchunk + 1]], axis=1)
        s_scr[slot] = _dot(lhs, kt) + bias_ref[bi, variant]

    def stage_softmax(slot):
        part = ATT_QB // 2
        for c in range(2 * ATT_QB // part):
            s = s_scr[slot, c * part:(c + 1) * part, :]
            mx = jnp.max(s, axis=-1, keepdims=True)
            p_scr[slot, c * part:(c + 1) * part, :] = jnp.exp(s - mx).astype(BF16)
            m_scr[slot, c * part:(c + 1) * part, :] = jnp.broadcast_to(mx, (part, LANES))

    def stage_values(g, slot):
        _, _, _, kchunk = cell_params(g)
        koff = pl.multiple_of(kchunk * ATT_QB, ATT_QB)
        r = _dot(p_scr[slot], vs[pl.ds(koff, 2 * ATT_QB), :])
        num = jnp.where(head0, r[0:ATT_QB, 0:LANES], r[ATT_QB:, 0:LANES])
        den = jnp.where(head0, r[0:ATT_QB, LANES:], r[ATT_QB:, LANES:])
        mx = jnp.where(head0, m_scr[slot, 0:ATT_QB, :], m_scr[slot, ATT_QB:, :])
        rows = pl.ds(pl.multiple_of(g * ATT_QB, ATT_QB), ATT_QB)
        o_scr[rows, :] = num
        d_scr[rows, :] = den
        l_scr[rows, :] = mx

    width = ATT_PIPE

    def group(c, do_values, do_softmax, do_logits):
        if do_values:
            for u in range(width):
                stage_values(c - width + u, u)
        if do_softmax:
            for u in range(width):
                stage_softmax(u)
        if do_logits:
            for u in range(width):
                stage_logits(c + width + u, u)

    group(-width, False, False, True)
    group(0, False, True, True)

    def trip(t, carry):
        group((t + 1) * width, True, True, True)
        return carry

    lax.fori_loop(0, total // width - 2, trip, 0)
    group(total - width, True, True, False)
    group(total, True, False, False)

    big = WINDOWS[-1][1]
    rows = ATT_TILE // big
    dil0, m0, _, _, qb0 = layout[0]
    assert dil0 == 1 and big == mid * mid
    for r4 in range(mid):
        k4[r4 * main4:(r4 + 1) * main4, :] = _strided(o_scr, qb0 + r4, main4, mid)
        k4[m0 + r4 * main4:m0 + (r4 + 1) * main4, :] = _strided(d_scr, qb0 + r4, main4, mid)
        v4[r4 * main4:(r4 + 1) * main4, :] = _strided(l_scr, qb0 + r4, main4, mid)
    _, m1, _, _, qb1 = layout[1]
    _, m2, _, _, qb2 = layout[2]
    for r in range(big):
        r4, s = r % mid, r // mid
        first = r4 * main4 + s
        nums = [_strided(k4, first, rows, mid), _strided(o_scr, qb1 + r4 * m1 + s, rows, mid),
                o_scr[qb2 + r * m2:qb2 + (r + 1) * m2, :]]
        dens = [_strided(k4, m0 + first, rows, mid), _strided(d_scr, qb1 + r4 * m1 + s, rows, mid),
                d_scr[qb2 + r * m2:qb2 + (r + 1) * m2, :]]
        mxs = [_strided(v4, first, rows, mid), _strided(l_scr, qb1 + r4 * m1 + s, rows, mid),
               l_scr[qb2 + r * m2:qb2 + (r + 1) * m2, :]]
        top = jnp.maximum(jnp.maximum(mxs[0], mxs[1]), mxs[2])
        ws = [jnp.exp(mx - top) for mx in mxs]
        num = ws[0] * nums[0] + ws[1] * nums[1] + ws[2] * nums[2]
        den = ws[0] * dens[0] + ws[1] * dens[1] + ws[2] * dens[2]
        q4[pl.ds(first, rows, stride=mid), :] = num / den
    for r4 in range(mid):
        o_ref[pl.ds(r4, main4, stride=mid), :] = q4[r4 * main4:(r4 + 1) * main4, :]


def _attention(q, k, v, rel_bias, tp, seq_p, seq_s):
    t, attn_w = q.shape
    n_heads = attn_w // HEAD_DIM
    pair = LANES // HEAD_DIM
    assert seq_p % ATT_TILE == 0 and seq_s % ATT_TILE == 0 and ATT_TILE == 2 * ATT_HALO
    assert ATT_TILE // WINDOWS[-1][1] == ATT_QB and pair == 2
    layout, krows = _branch_layout()

    nb = len(WINDOWS)
    span = 2 * ATT_SIDE + 1
    jj = np.arange(2 * ATT_QB)[None, :]
    offsets = np.arange(span) - ATT_SIDE
    table_bucket = np.stack([_t5_bucket(offsets * dil) for _, dil in WINDOWS])
    table = jnp.zeros((n_heads, nb, span), F32)
    for b in range(N_BUCKETS):
        table = jnp.where(jnp.asarray(table_bucket == b)[None], rel_bias[b].astype(F32)[:, None, None], table)
    row_len = 3 * ATT_QB
    period = row_len + 1
    padded = jnp.concatenate([table, jnp.full((n_heads, nb, period - span), NEG_INF, F32)], axis=-1)
    flat = jnp.tile(padded, (1, 1, ATT_QB + 1))[..., :ATT_QB * row_len]
    bias = flat.reshape(n_heads, nb, ATT_QB, row_len)[..., :2 * ATT_QB]
    bias = bias.reshape(n_heads // pair, pair, nb, ATT_QB, 2 * ATT_QB).transpose(0, 2, 1, 3, 4)
    bias = bias.reshape(n_heads // pair, nb, 1, pair * ATT_QB, 2 * ATT_QB)
    left = jnp.asarray((jj < ATT_SIDE)[None, None, None] & (np.arange(4) % 2 == 1)[None, None, :, None, None])
    right = jnp.asarray((jj >= 2 * ATT_QB - ATT_SIDE)[None, None, None]
                        & (np.arange(4) // 2 == 1)[None, None, :, None, None])
    bias = jnp.where(left | right, NEG_INF, bias)

    halo_blocks = t // ATT_HALO
    per_tile = ATT_TILE // ATT_HALO
    mid = WINDOWS[1][1]
    main = pl.BlockSpec((ATT_TILE, LANES), lambda i, j: (i, j))
    prev = pl.BlockSpec((ATT_HALO, LANES), lambda i, j: (jnp.maximum(i * per_tile - 1, 0), j))
    nxt = pl.BlockSpec((ATT_HALO, LANES), lambda i, j: (jnp.minimum((i + 1) * per_tile, halo_blocks - 1), j))
    n_q = nb * ATT_TILE
    return pl.pallas_call(
        functools.partial(_attn_kernel, tp // ATT_TILE, seq_p // ATT_TILE, seq_s // ATT_TILE),
        grid=(t // ATT_TILE, attn_w // LANES),
        in_specs=[main, main, prev, nxt, main, prev, nxt,
                  pl.BlockSpec((None, nb, 4, pair * ATT_QB, 2 * ATT_QB), lambda i, j: (j, 0, 0, 0, 0))],
        out_specs=main,
        out_shape=jax.ShapeDtypeStruct((t, attn_w), F32),
        scratch_shapes=[pltpu.VMEM((n_q, LANES), BF16),
                        pltpu.VMEM((krows // ATT_QB, LANES, ATT_QB), BF16),
                        pltpu.VMEM((krows, 2 * LANES), BF16),
                        pltpu.VMEM((n_q, LANES), F32), pltpu.VMEM((n_q, LANES), F32),
                        pltpu.VMEM((n_q, LANES), F32),
                        pltpu.VMEM((ATT_PIPE, pair * ATT_QB, 2 * ATT_QB), F32),
                        pltpu.VMEM((ATT_PIPE, pair * ATT_QB, 2 * ATT_QB), BF16),
                        pltpu.VMEM((ATT_PIPE, pair * ATT_QB, LANES), F32),
                        pltpu.VMEM(((ATT_TILE + 2 * ATT_HALO), LANES), F32),
                        pltpu.VMEM(((ATT_TILE + 2 * ATT_HALO), LANES), F32),
                        pltpu.VMEM((ATT_TILE, LANES), F32)],
        compiler_params=pltpu.CompilerParams(dimension_semantics=("arbitrary", "arbitrary"),
                                             vmem_limit_bytes=VMEM_LIMIT),
        name="attn",
    )(q, k, k, k, v, v, v, bias)


def _outproj_kernel(n_tiles, sgu_w, n_exp, *refs):
    h2_ref = refs[11]
    i = pl.program_id(0)

    @pl.when(i < n_tiles)
    def _():
        _outproj_tile(sgu_w, n_exp, *refs)

    @pl.when(i >= n_tiles)
    def _():
        h2_ref[...] = jnp.zeros_like(h2_ref)


def _outproj_tile(sgu_w, n_exp, a_ref, attn_ref, x_ref, onb_ref, wo_ref, norm2_ref, wrh_ref, wrl_ref,
                  br_ref, tri_ref, x1_ref, h2_ref, meta_ref, cnt_ref, route_ref, carry):
    i = pl.program_id(0)

    @pl.when(i == 0)
    def _():
        carry[...] = jnp.zeros_like(carry)

    bn = _rms(attn_ref[...], onb_ref[...]).astype(BF16)
    x1 = x_ref[...] + _dot(a_ref[...], wo_ref[0:sgu_w, :]) + _dot(bn, wo_ref[sgu_w:, :])
    x1_ref[...] = x1
    h2 = _rms(x1, norm2_ref[...])
    h2_ref[...] = h2.astype(BF16)
    hi, lo = _split_bf16(h2)
    logits = _dot(hi, wrh_ref[...]) + _dot(lo, wrh_ref[...]) + _dot(hi, wrl_ref[...]) + br_ref[...]

    lane = lax.broadcasted_iota(jnp.int32, logits.shape, 1)
    work = jnp.where(lane < n_exp, logits, -jnp.inf)
    chosen = jnp.zeros(logits.shape, jnp.bool_)
    experts, values = [], []
    for _ in range(TOP_K):
        top = jnp.max(work, axis=-1, keepdims=True)
        idx = jnp.min(jnp.where(work == top, lane, LANES), axis=-1, keepdims=True)
        hit = lane == idx
        chosen = jnp.logical_or(chosen, hit)
        work = jnp.where(hit, -jnp.inf, work)
        experts.append(idx)
        values.append(top)
    exps = [jnp.exp(v - values[0]) for v in values]
    den = exps[0] + exps[1] + exps[2] + exps[3]

    onehot = jnp.where(chosen, 1.0, 0.0)
    before = _dot(tri_ref[...], onehot.astype(BF16)) + carry[0:1, :]
    carry[0:1, :] = carry[0:1, :] + jnp.sum(onehot, axis=0, keepdims=True)
    cnt_ref[...] = carry[...]

    meta = jnp.zeros(logits.shape, F32)
    for kk in range(TOP_K):
        rank = jnp.sum(jnp.where(lane == experts[kk], before, 0.0), axis=-1, keepdims=True)
        meta = jnp.where(lane == kk, experts[kk].astype(F32), meta)
        meta = jnp.where(lane == TOP_K + kk, exps[kk] / den, meta)
        meta = jnp.where(lane == 2 * TOP_K + kk, rank, meta)
    meta_ref[...] = meta
    route_ref[...] = meta.T[0:route_ref.shape[0], :]


def _outproj(a_n, attn, x, row0, out_norm_b, w_o, norm2, w_router, b_router):
    sgu_w = a_n.shape[1]
    attn_w = attn.shape[1]
    t, d_model = x.shape
    n_exp = w_router.shape[1]
    assert n_exp <= LANES and row0 % TOK_TILE == 0 and t % TOK_TILE == 0
    b0 = row0 // TOK_TILE
    wr = jnp.pad(w_router, ((0, 0), (0, LANES - n_exp)))
    wr_hi = wr.astype(BF16)
    wr_lo = (wr - wr_hi.astype(F32)).astype(BF16)
    br = jnp.pad(b_router, (0, LANES - n_exp))[None]
    tri = np.tril(np.ones((TOK_TILE, TOK_TILE), np.float32), -1)
    n_tiles = t // TOK_TILE
    h2_rows = max(t, GATHER_MIN_ROWS)
    assert h2_rows % TOK_TILE == 0
    last = n_tiles - 1

    const = lambda shape: pl.BlockSpec(shape, lambda i: (0,) * len(shape))
    tok = lambda w: pl.BlockSpec((TOK_TILE, w), lambda i: (jnp.minimum(i, last), 0))
    src = lambda w: pl.BlockSpec((TOK_TILE, w), lambda i: (jnp.minimum(i, last) + b0, 0))
    return pl.pallas_call(
        functools.partial(_outproj_kernel, n_tiles, sgu_w, n_exp),
        grid=(h2_rows // TOK_TILE,),
        in_specs=[
            src(sgu_w), src(attn_w), tok(d_model),
            const((1, attn_w)), const(w_o.shape), const((1, d_model)), const((d_model, LANES)),
            const((d_model, LANES)), const((1, LANES)), const((TOK_TILE, TOK_TILE)),
        ],
        out_specs=[tok(d_model), pl.BlockSpec((TOK_TILE, d_model), lambda i: (i, 0)), tok(LANES), const((8, LANES)),
                   pl.BlockSpec((4 * TOP_K, TOK_TILE), lambda i: (0, jnp.minimum(i, last)))],
        out_shape=[jax.ShapeDtypeStruct((t, d_model), F32), jax.ShapeDtypeStruct((h2_rows, d_model), BF16),
                   jax.ShapeDtypeStruct((t, LANES), F32), jax.ShapeDtypeStruct((8, LANES), F32),
                   jax.ShapeDtypeStruct((4 * TOP_K, t), F32)],
        scratch_shapes=[pltpu.VMEM((8, LANES), F32)],
        compiler_params=pltpu.CompilerParams(dimension_semantics=("arbitrary",), vmem_limit_bytes=VMEM_LIMIT),
        name="outproj",
    )(a_n, attn, x, out_norm_b[None], w_o.astype(BF16), norm2[None], wr_hi, wr_lo, br, jnp.asarray(tri, BF16))


def _moe_kernel(d_exp, blk_ref, exp_ref, lo_ref, hi_ref, slot_ref, nxt_ref, x_ref, wgu_hbm, bgu_ref, wd_hbm, bd_ref,
                y_ref, wgu_f, wd_f, wgu_s, wd_s, act_s, sem):
    it = pl.program_id(0)
    prev = jnp.maximum(it - 1, 0)
    new_expert = jnp.logical_or(it == 0, exp_ref[it] != exp_ref[prev])
    new_block = jnp.logical_or(it == 0, blk_ref[it] != blk_ref[prev])
    lo, hi = lo_ref[it], hi_ref[it]
    active = hi > lo

    def weight_copies(e, slot):
        return (pltpu.make_async_copy(wgu_hbm.at[e], wgu_f.at[slot], sem.at[0, slot]),
                pltpu.make_async_copy(wd_hbm.at[e], wd_f.at[slot], sem.at[1, slot]))

    @pl.when(it == 0)
    def _():
        for c in weight_copies(exp_ref[0], slot_ref[0]):
            c.start()

    @pl.when(jnp.logical_and(new_expert, active))
    def _():
        slot = slot_ref[it]
        for c in weight_copies(exp_ref[it], slot):
            c.wait()

        @pl.when(nxt_ref[it] >= 0)
        def _():
            for c in weight_copies(nxt_ref[it], 1 - slot):
                c.start()

        wgu_s[...] = wgu_f[slot].astype(BF16)
        wd_s[...] = wd_f[slot].astype(BF16)

    @pl.when(active)
    def _():
        x = x_ref[...]
        step = 512
        for n in range(0, d_exp, step):
            gate = _dot(x, wgu_s[:, n:n + step]) + bgu_ref[:, n:n + step]
            up = _dot(x, wgu_s[:, d_exp + n:d_exp + n + step]) + bgu_ref[:, d_exp + n:d_exp + n + step]
            gate = jnp.minimum(gate, SWIGLU_LIMIT)
            up = jnp.clip(up, -SWIGLU_LIMIT, SWIGLU_LIMIT)
            glu = gate * jax.nn.sigmoid(SWIGLU_ALPHA * gate)
            act_s[:, n:n + step] = ((up + 1.0) * glu).astype(BF16)
        y = (_dot(act_s[...], wd_s[...]) + bd_ref[...]).astype(y_ref.dtype)
        row = lax.broadcasted_iota(jnp.int32, (MOE_ROWS, 1), 0)
        mine = jnp.logical_and(row >= lo, row < hi)

        @pl.when(new_block)
        def _():
            y_ref[...] = jnp.where(mine, y, jnp.zeros_like(y))

        @pl.when(jnp.logical_not(new_block))
        def _():
            y_ref[...] = jnp.where(mine, y, y_ref[...])


def _moe_experts(x_rows, items, w_gu, b_gu, w_down, b_down):
    n_rows, d_model = x_rows.shape
    n_exp, _, two_de = w_gu.shape
    d_exp = two_de // 2
    n_items = items[0].shape[0]
    grid_spec = pltpu.PrefetchScalarGridSpec(
        num_scalar_prefetch=len(items),
        grid=(n_items,),
        in_specs=[
            pl.BlockSpec((MOE_ROWS, d_model), lambda i, blk, ex, *_: (blk[i], 0)),
            pl.BlockSpec(memory_space=pl.ANY),
            pl.BlockSpec((None, 1, two_de), lambda i, blk, ex, *_: (ex[i], 0, 0)),
            pl.BlockSpec(memory_space=pl.ANY),
            pl.BlockSpec((None, 1, d_model), lambda i, blk, ex, *_: (ex[i], 0, 0)),
        ],
        out_specs=pl.BlockSpec((MOE_ROWS, d_model), lambda i, blk, ex, *_: (blk[i], 0)),
        scratch_shapes=[pltpu.VMEM((2, d_model, two_de), F32), pltpu.VMEM((2, d_exp, d_model), F32),
                        pltpu.VMEM((d_model, two_de), BF16), pltpu.VMEM((d_exp, d_model), BF16),
                        pltpu.VMEM((MOE_ROWS, d_exp), BF16), pltpu.SemaphoreType.DMA((2, 2))],
    )
    return pl.pallas_call(
        functools.partial(_moe_kernel, d_exp),
        grid_spec=grid_spec,
        out_shape=jax.ShapeDtypeStruct((n_rows, d_model), BF16),
        compiler_params=pltpu.CompilerParams(dimension_semantics=("arbitrary",), vmem_limit_bytes=VMEM_LIMIT),
        name="moe",
    )(*items, x_rows, w_gu, b_gu[:, None, :], w_down, b_down[:, None, :])


def _route(meta, counts_f, n_exp):
    t = meta.shape[1]
    n_assign = t * TOP_K
    assert n_assign % MOE_ROWS == 0
    top_e = meta[0:TOP_K].astype(jnp.int32)
    rank = meta[2 * TOP_K:3 * TOP_K].astype(jnp.int32)
    counts = counts_f[0, :n_exp].astype(jnp.int32)
    end = jnp.cumsum(counts)
    start = end - counts
    experts = jnp.arange(n_exp, dtype=jnp.int32)
    start_of = jnp.zeros_like(top_e)
    for e in range(n_exp):
        start_of = jnp.where(top_e == e, start[e], start_of)
    pos = (start_of + rank).reshape(-1)

    shift = (n_assign - 1).bit_length()
    assert (n_exp << shift) < 2 ** 31
    assign = jnp.arange(t, dtype=jnp.int32)[None, :] * TOP_K + jnp.arange(TOP_K, dtype=jnp.int32)[:, None]
    keys = ((top_e << shift) + assign).reshape(-1)
    sorted_tok = (jnp.sort(keys) & ((1 << shift) - 1)) // TOP_K

    n_blocks = n_assign // MOE_ROWS
    n_items = n_blocks + n_exp
    first_blk = start // MOE_ROWS
    n_it = jnp.where(counts > 0, (end - 1) // MOE_ROWS - first_blk + 1, 0)
    it_end = jnp.cumsum(n_it)
    it_start = it_end - n_it
    i = jnp.arange(n_items, dtype=jnp.int32)
    valid = i < it_end[-1]
    e_i = jnp.sum(jnp.minimum(i, it_end[-1] - 1)[:, None] >= it_end[None, :], axis=1).astype(jnp.int32)
    e_i = jnp.minimum(e_i, n_exp - 1)
    pick = lambda table: jnp.sum(jnp.where(e_i[:, None] == experts[None, :], table[None, :], 0), axis=1)
    blk_i = jnp.where(valid, pick(first_blk) + i - pick(it_start), n_blocks - 1)
    lo_i = jnp.where(valid, jnp.clip(pick(start) - blk_i * MOE_ROWS, 0, MOE_ROWS), 0)
    hi_i = jnp.where(valid, jnp.clip(pick(end) - blk_i * MOE_ROWS, 0, MOE_ROWS), 0)
    has = counts > 0
    slot_e = (jnp.cumsum(has.astype(jnp.int32)) - 1) % 2
    later = jnp.where(jnp.logical_and(has[None, :], experts[None, :] > experts[:, None]), experts[None, :], n_exp)
    next_e = jnp.min(later, axis=1)
    next_e = jnp.where(next_e == n_exp, -1, next_e)
    items = tuple(a.astype(jnp.int32) for a in (blk_i, e_i, lo_i, hi_i, pick(slot_e), pick(next_e)))
    return pos, sorted_tok, items


def _combine_kernel(x1_ref, yg_ref, meta_ref, o_ref):
    acc = x1_ref[...]
    for kk in range(TOP_K):
        gate = meta_ref[:, TOP_K + kk:TOP_K + kk + 1]
        acc = acc + gate * yg_ref[kk].astype(F32)
    o_ref[...] = acc


def _combine(x1, yg, meta):
    rows, d_model = x1.shape
    tok = lambda w: pl.BlockSpec((TOK_TILE, w), lambda i: (i, 0))
    return pl.pallas_call(
        _combine_kernel,
        grid=(rows // TOK_TILE,),
        in_specs=[tok(d_model), pl.BlockSpec((TOP_K, TOK_TILE, d_model), lambda i: (0, i, 0)), tok(LANES)],
        out_specs=tok(d_model),
        out_shape=jax.ShapeDtypeStruct((rows, d_model), F32),
        compiler_params=pltpu.CompilerParams(dimension_semantics=("arbitrary",), vmem_limit_bytes=VMEM_LIMIT),
        name="combine",
    )(x1, yg, meta)


def kernel(x_prompt, x_sample, norm1, w_in, q_gain, k_gain, rel_bias, sgu_norm, w_s, b_s, out_norm_a, out_norm_b,
           w_o, norm2, w_router, b_router, w_gu, b_gu, w_down, b_down):
    assert norm1.shape[0] == 1, "single-layer trunk"
    bp, sp, d_model = x_prompt.shape
    bs, ss, _ = x_sample.shape
    xp = x_prompt.reshape(bp * sp, d_model)
    xs = x_sample.reshape(bs * ss, d_model)
    tp, t = bp * sp, bp * sp + bs * ss

    q, k, v, a_n = _inproj(xp, xs, norm1[0], w_in[0], q_gain[0], k_gain[0], sgu_norm[0], w_s[0], b_s[0],
                           out_norm_a[0])
    attn = _attention(q, k, v, rel_bias, tp, sp, ss)

    outs = {}
    after = None
    for x, row0 in sorted(((xp, 0), (xs, tp)), key=lambda c: -c[0].shape[0]):
        rows = x.shape[0]
        if after is not None:
            x, _ = lax.optimization_barrier((x, after[0]))
        x1, h2, meta, counts, route = _outproj(a_n, attn, x, row0, out_norm_b[0], w_o[0], norm2[0], w_router[0],
                                               b_router[0])
        pos, sorted_tok, items = _route(route, counts, w_router.shape[-1])
        x_rows = h2.at[sorted_tok].get(mode="promise_in_bounds")
        if after is not None:
            x_rows, _ = lax.optimization_barrier((x_rows, after[1]))
        y_rows = _moe_experts(x_rows, items, w_gu[0], b_gu[0], w_down[0], b_down[0])
        yg = y_rows.at[pos].get(mode="promise_in_bounds").reshape(TOP_K, rows, d_model)
        if after is not None:
            yg, _ = lax.optimization_barrier((yg, after[2]))
        outs[row0] = _combine(x1, yg, meta)
        after = (counts, y_rows, outs[row0])
    return (outs[0].reshape(bp, sp, d_model), outs[tp].reshape(bs, ss, d_model))
```

```python
import functools
import math

import numpy as np
import jax
import jax.numpy as jnp
from jax import lax
from jax.experimental import pallas as pl
from jax.experimental.pallas import tpu as pltpu

F32 = jnp.float32
BF16 = jnp.bfloat16

EPS = 1e-6
NEG_INF = -1e30
HEAD_DIM = 64
SGU_CHUNK = 128
WINDOWS = ((128, 1), (512, 4), (2048, 16))
N_BUCKETS = 32
MAX_DISTANCE = 1024
TOP_K = 4
SWIGLU_LIMIT = 7.0
SWIGLU_ALPHA = 1.702

LANES = 128
TOK_TILE = 512
ATT_TILE = 2048
ATT_HALO = 1024
ATT_QB = 128
ATT_SIDE = 64
ATT_PIPE = 8
MOE_ROWS = 512
GATHER_MIN_ROWS = 32768
VMEM_LIMIT = 56 * 1024 * 1024


def _dot(a, b):
    return jnp.dot(a, b, preferred_element_type=F32)


def _dot_nt(a, b):
    return lax.dot_general(a, b, (((1,), (1,)), ((), ())), preferred_element_type=F32)


def _rms(x, g):
    ms = jnp.mean(x * x, axis=-1, keepdims=True)
    return x * lax.rsqrt(ms + EPS) * g


def _gelu(x):
    return 0.5 * x * (1.0 + lax.erf(x * (1.0 / math.sqrt(2.0))))


def _split_bf16(x):
    hi = x.astype(BF16)
    lo = (x - hi.astype(F32)).astype(BF16)
    return hi, lo


def _inproj_kernel(attn_w, sgu_w, x_ref, norm1_ref, win_ref, gsum_ref, gexp_ref,
                   qkg_ref, sgun_ref, wsp_ref, bsb_ref, ona_ref, q_ref, k_ref, v_ref, a_ref):
    h = _rms(x_ref[...], norm1_ref[...]).astype(BF16)

    zqk = _dot(h, win_ref[:, 0:2 * attn_w])
    sq_hi, sq_lo = _split_bf16(zqk * zqk)
    ss = _dot(sq_hi, gsum_ref[...]) + _dot(sq_lo, gsum_ref[...])
    inv = lax.rsqrt(ss * (1.0 / HEAD_DIM) + EPS)
    inv_hi, inv_lo = _split_bf16(inv)
    invb = _dot(inv_hi, gexp_ref[...]) + _dot(inv_lo, gexp_ref[...])
    qk = zqk * invb * qkg_ref[...]
    q_ref[...] = qk[:, 0:attn_w]
    k_ref[...] = qk[:, attn_w:2 * attn_w]
    v_ref[...] = _dot(h, win_ref[:, 2 * attn_w:3 * attn_w])

    c0 = 3 * attn_w
    u = _gelu(_dot(h, win_ref[:, c0:c0 + sgu_w]))
    gv = _gelu(_dot(h, win_ref[:, c0 + sgu_w:c0 + 2 * sgu_w]))
    vsn = _rms(gv, sgun_ref[...]).astype(BF16)
    lane = lax.broadcasted_iota(jnp.int32, (SGU_CHUNK, LANES), 1)
    lo_half = lane < HEAD_DIM
    zero = jnp.zeros((SGU_CHUNK, LANES), BF16)
    for c in range(TOK_TILE // SGU_CHUNK):
        r0 = c * SGU_CHUNK
        parts = []
        for j in range(sgu_w // LANES):
            blk = vsn[r0:r0 + SGU_CHUNK, j * LANES:(j + 1) * LANES]
            rhs = jnp.concatenate([jnp.where(lo_half, blk, zero), jnp.where(lo_half, zero, blk)], axis=0)
            parts.append(_dot(wsp_ref[j], rhs))
        s = jnp.concatenate(parts, axis=1) + bsb_ref[...]
        a = u[r0:r0 + SGU_CHUNK, :] * s
        a_ref[r0:r0 + SGU_CHUNK, :] = _rms(a, ona_ref[...]).astype(BF16)


def _inproj(x, norm1, w_in, q_gain, k_gain, sgu_norm, w_s, b_s, out_norm_a):
    t, d_model = x.shape
    n_heads_w = q_gain.shape[0]
    sgu_w = sgu_norm.shape[0]
    attn_w = (w_in.shape[1] - 2 * sgu_w) // 3
    n_heads = attn_w // n_heads_w
    n_groups = w_s.shape[0]
    assert n_heads_w == HEAD_DIM and sgu_w // n_groups == HEAD_DIM and w_s.shape[1] == SGU_CHUNK
    assert t % TOK_TILE == 0 and 2 * n_heads <= LANES

    heads = np.arange(2 * attn_w) // HEAD_DIM
    gsum = (heads[:, None] == np.arange(LANES)[None, :]).astype(np.float32)
    gexp = gsum.T
    qkg = jnp.concatenate([jnp.tile(q_gain, n_heads) * (HEAD_DIM ** -0.5), jnp.tile(k_gain, n_heads)])
    wsp = jnp.concatenate([w_s[0::2], w_s[1::2]], axis=2).astype(BF16)
    bsb = jnp.repeat(b_s.T, HEAD_DIM, axis=1)

    const = lambda shape: pl.BlockSpec(shape, lambda i: (0,) * len(shape))
    tok = lambda w: pl.BlockSpec((TOK_TILE, w), lambda i: (i, 0))
    return pl.pallas_call(
        functools.partial(_inproj_kernel, attn_w, sgu_w),
        grid=(t // TOK_TILE,),
        in_specs=[
            tok(d_model), const((1, d_model)), const(w_in.shape), const((2 * attn_w, LANES)), const((LANES, 2 * attn_w)),
            const((1, 2 * attn_w)), const((1, sgu_w)), const(wsp.shape), const((SGU_CHUNK, sgu_w)),
            const((1, sgu_w)),
        ],
        out_specs=[tok(attn_w), tok(attn_w), tok(attn_w), tok(sgu_w)],
        out_shape=[jax.ShapeDtypeStruct((t, attn_w), F32)] * 3 + [jax.ShapeDtypeStruct((t, sgu_w), BF16)],
        compiler_params=pltpu.CompilerParams(dimension_semantics=("arbitrary",), vmem_limit_bytes=VMEM_LIMIT),
        name="inproj",
    )(x, norm1[None], w_in.astype(BF16), jnp.asarray(gsum, BF16), jnp.asarray(gexp, BF16),
      qkg[None], sgu_norm[None], wsp, bsb, out_norm_a[None])


def _t5_bucket(rel):
    nb = N_BUCKETS // 2
    bucket = (rel > 0).astype(np.int32) * nb
    n = np.abs(rel)
    max_exact = nb // 2
    large = max_exact + (np.log(np.maximum(n, 1) / max_exact)
                         / np.log(MAX_DISTANCE / max_exact) * (nb - max_exact)).astype(np.int32)
    large = np.minimum(large, nb - 1)
    return (bucket + np.where(n < max_exact, n, large)).astype(np.int32)


def _branch_layout():
    out, kbase = [], 0
    for bi, (window, dil) in enumerate(WINDOWS):
        assert window // (2 * dil) == ATT_SIDE
        m = ATT_TILE // dil
        seg = m + 2 * ATT_SIDE
        out.append((dil, m, seg, kbase, bi * ATT_TILE))
        kbase += dil * seg
    return out, kbase


def _strided(ref, start, size, stride):
    if stride == 1:
        return ref[start:start + size, :]
    return ref[pl.ds(start, size, stride=stride), :]


def _attn_kernel(per_seq, q_ref, km_ref, kp_ref, kn_ref, vm_ref, vp_ref, vn_ref,
                 bias_ref, o_ref, qs, kts, vs, o_scr, d_scr, l_scr, s_scr, p_scr, m_scr, k4, v4, q4):
    i = pl.program_id(0)
    w = i % per_seq
    left_ok = w > 0
    right_ok = w < per_seq - 1
    layout, _ = _branch_layout()
    mid = WINDOWS[1][1]
    assert WINDOWS[2][1] == mid * mid
    halo4, main4 = ATT_HALO // mid, ATT_TILE // mid
    win4 = 2 * halo4 + main4

    @pl.when(jnp.logical_and(i == 0, pl.program_id(1) == 0))
    def _():
        vs[:, LANES:] = jnp.ones((vs.shape[0], LANES), BF16)

    for r in range(mid):
        for dst, main, prev, nxt in ((k4, km_ref, kp_ref, kn_ref), (v4, vm_ref, vp_ref, vn_ref)):
            dst[r * win4:r * win4 + halo4, :] = _strided(prev, r, halo4, mid)
            dst[r * win4 + halo4:r * win4 + halo4 + main4, :] = _strided(main, r, main4, mid)
            dst[r * win4 + halo4 + main4:(r + 1) * win4, :] = _strided(nxt, r, halo4, mid)
        q4[r * main4:(r + 1) * main4, :] = _strided(q_ref, r, main4, mid)

    def key_rows(src_k, src_v, start, stride, kbase_rows):
        kc = _strided(src_k, start, ATT_QB, stride)
        kts[kbase_rows // ATT_QB] = kc.T.astype(BF16)
        vs[kbase_rows:kbase_rows + ATT_QB, 0:LANES] = _strided(src_v, start, ATT_QB, stride).astype(BF16)

    for bi, (dil, m, seg, kbase, qbase) in enumerate(layout):
        nchunk = seg // ATT_QB
        for r in range(dil):
            o = kbase + r * seg
            if dil == 1:
                for c in range(nchunk):
                    lo = c * ATT_QB - ATT_SIDE
                    if c == 0:
                        kc = jnp.concatenate([kp_ref[ATT_HALO - ATT_SIDE:, :], km_ref[0:ATT_QB - ATT_SIDE, :]], axis=0)
                        vc = jnp.concatenate([vp_ref[ATT_HALO - ATT_SIDE:, :], vm_ref[0:ATT_QB - ATT_SIDE, :]], axis=0)
                    elif c == nchunk - 1:
                        kc = jnp.concatenate([km_ref[lo:, :], kn_ref[0:ATT_SIDE, :]], axis=0)
                        vc = jnp.concatenate([vm_ref[lo:, :], vn_ref[0:ATT_SIDE, :]], axis=0)
                    else:
                        kc = km_ref[lo:lo + ATT_QB, :]
                        vc = vm_ref[lo:lo + ATT_QB, :]
                    kts[(o + c * ATT_QB) // ATT_QB] = kc.T.astype(BF16)
                    vs[o + c * ATT_QB:o + (c + 1) * ATT_QB, 0:LANES] = vc.astype(BF16)
                qs[qbase:qbase + m, :] = q_ref[...].astype(BF16)
            else:
                r4, s = r % mid, r // mid
                step = dil // mid
                first = r4 * win4 + (halo4 - ATT_SIDE * step) + s
                for c in range(nchunk):
                    key_rows(k4, v4, first + c * ATT_QB * step, step, o + c * ATT_QB)
                qs[qbase + r * m:qbase + (r + 1) * m, :] = _strided(q4, r4 * main4 + s, m, step).astype(BF16)

    lane = lax.broadcasted_iota(jnp.int32, (ATT_QB, LANES), 1)
    head0 = lane < HEAD_DIM
    cells = ATT_TILE // ATT_QB
    total = len(layout) * cells
    blk_bits = [(m // ATT_QB).bit_length() - 1 for _, m, _, _, _ in layout]
    seg_chunks = [seg // ATT_QB for _, _, seg, _, _ in layout]
    base_chunks = [kbase // ATT_QB for _, _, _, kbase, _ in layout]

    def cell_params(g):
        bi = g // cells
        idx = g % cells
        if isinstance(g, int):
            pick = lambda vals: vals[bi]
        else:
            pick = lambda vals: jnp.where(bi == 0, vals[0], jnp.where(bi == 1, vals[1], vals[2]))
        bits = pick(blk_bits)
        last_blk = (1 << bits) - 1
        blk = idx & last_blk
        return bi, blk, last_blk, pick(base_chunks) + (idx >> bits) * pick(seg_chunks) + blk

    def stage_logits(g, slot):
        bi, blk, last_blk, kchunk = cell_params(g)
        left_bad = jnp.logical_and(blk == 0, jnp.logical_not(left_ok))
        right_bad = jnp.logical_and(blk == last_blk, jnp.logical_not(right_ok))
        variant = left_bad.astype(jnp.int32) + 2 * right_bad.astype(jnp.int32)
        qc = qs[pl.ds(pl.multiple_of(g * ATT_QB, ATT_QB), ATT_QB), :]
        zero = jnp.zeros_like(qc)
        lhs = jnp.concatenate([jnp.where(head0, qc, zero), jnp.where(head0, zero, qc)], axis=0)
        kt = jnp.concatenate([kts[kchunk], kts[kchunk + 1]], axis=1)
        s_scr[slot] = _dot(lhs, kt) + bias_ref[bi, variant]

    def stage_softmax(slot):
        part = ATT_QB // 2
        for c in range(2 * ATT_QB // part):
            s = s_scr[slot, c * part:(c + 1) * part, :]
            mx = jnp.max(s, axis=-1, keepdims=True)
            p_scr[slot, c * part:(c + 1) * part, :] = jnp.exp(s - mx).astype(BF16)
            m_scr[slot, c * part:(c + 1) * part, :] = jnp.broadcast_to(mx, (part, LANES))

    def stage_values(g, slot):
        _, _, _, kchunk = cell_params(g)
        koff = pl.multiple_of(kchunk * ATT_QB, ATT_QB)
        r = _dot(p_scr[slot], vs[pl.ds(koff, 2 * ATT_QB), :])
        num = jnp.where(head0, r[0:ATT_QB, 0:LANES], r[ATT_QB:, 0:LANES])
        den = jnp.where(head0, r[0:ATT_QB, LANES:], r[ATT_QB:, LANES:])
        mx = jnp.where(head0, m_scr[slot, 0:ATT_QB, :], m_scr[slot, ATT_QB:, :])
        rows = pl.ds(pl.multiple_of(g * ATT_QB, ATT_QB), ATT_QB)
        o_scr[rows, :] = num
        d_scr[rows, :] = den
        l_scr[rows, :] = mx

    width = ATT_PIPE

    def group(c, do_values, do_softmax, do_logits):
        if do_values:
            for u in range(width):
                stage_values(c - width + u, u)
        if do_softmax:
            for u in range(width):
                stage_softmax(u)
        if do_logits:
            for u in range(width):
                stage_logits(c + width + u, u)

    group(-width, False, False, True)
    group(0, False, True, True)

    def trip(t, carry):
        group((t + 1) * width, True, True, True)
        return carry

    lax.fori_loop(0, total // width - 2, trip, 0)
    group(total - width, True, True, False)
    group(total, True, False, False)

    big = WINDOWS[-1][1]
    rows = ATT_TILE // big
    dil0, m0, _, _, qb0 = layout[0]
    assert dil0 == 1 and big == mid * mid
    for r4 in range(mid):
        k4[r4 * main4:(r4 + 1) * main4, :] = _strided(o_scr, qb0 + r4, main4, mid)
        k4[m0 + r4 * main4:m0 + (r4 + 1) * main4, :] = _strided(d_scr, qb0 + r4, main4, mid)
        v4[r4 * main4:(r4 + 1) * main4, :] = _strided(l_scr, qb0 + r4, main4, mid)
    _, m1, _, _, qb1 = layout[1]
    _, m2, _, _, qb2 = layout[2]
    for r in range(big):
        r4, s = r % mid, r // mid
        first = r4 * main4 + s
        nums = [_strided(k4, first, rows, mid), _strided(o_scr, qb1 + r4 * m1 + s, rows, mid),
                o_scr[qb2 + r * m2:qb2 + (r + 1) * m2, :]]
        dens = [_strided(k4, m0 + first, rows, mid), _strided(d_scr, qb1 + r4 * m1 + s, rows, mid),
                d_scr[qb2 + r * m2:qb2 + (r + 1) * m2, :]]
        mxs = [_strided(v4, first, rows, mid), _strided(l_scr, qb1 + r4 * m1 + s, rows, mid),
               l_scr[qb2 + r * m2:qb2 + (r + 1) * m2, :]]
        top = jnp.maximum(jnp.maximum(mxs[0], mxs[1]), mxs[2])
        ws = [jnp.exp(mx - top) for mx in mxs]
        num = ws[0] * nums[0] + ws[1] * nums[1] + ws[2] * nums[2]
        den = ws[0] * dens[0] + ws[1] * dens[1] + ws[2] * dens[2]
        q4[pl.ds(first, rows, stride=mid), :] = num / den
    for r4 in range(mid):
        o_ref[pl.ds(r4, main4, stride=mid), :] = q4[r4 * main4:(r4 + 1) * main4, :]


def _attention(q, k, v, rel_bias, seq_len):
    t, attn_w = q.shape
    n_heads = attn_w // HEAD_DIM
    pair = LANES // HEAD_DIM
    assert seq_len % ATT_TILE == 0 and t % seq_len == 0 and ATT_TILE == 2 * ATT_HALO
    assert ATT_TILE // WINDOWS[-1][1] == ATT_QB and pair == 2
    layout, krows = _branch_layout()

    nb = len(WINDOWS)
    span = 2 * ATT_SIDE + 1
    jj = np.arange(2 * ATT_QB)[None, :]
    offsets = np.arange(span) - ATT_SIDE
    table_bucket = np.stack([_t5_bucket(offsets * dil) for _, dil in WINDOWS])
    table = jnp.zeros((n_heads, nb, span), F32)
    for b in range(N_BUCKETS):
        table = jnp.where(jnp.asarray(table_bucket == b)[None], rel_bias[b].astype(F32)[:, None, None], table)
    row_len = 3 * ATT_QB
    period = row_len + 1
    padded = jnp.concatenate([table, jnp.full((n_heads, nb, period - span), NEG_INF, F32)], axis=-1)
    flat = jnp.tile(padded, (1, 1, ATT_QB + 1))[..., :ATT_QB * row_len]
    bias = flat.reshape(n_heads, nb, ATT_QB, row_len)[..., :2 * ATT_QB]
    bias = bias.reshape(n_heads // pair, pair, nb, ATT_QB, 2 * ATT_QB).transpose(0, 2, 1, 3, 4)
    bias = bias.reshape(n_heads // pair, nb, 1, pair * ATT_QB, 2 * ATT_QB)
    left = jnp.asarray((jj < ATT_SIDE)[None, None, None] & (np.arange(4) % 2 == 1)[None, None, :, None, None])
    right = jnp.asarray((jj >= 2 * ATT_QB - ATT_SIDE)[None, None, None]
                        & (np.arange(4) // 2 == 1)[None, None, :, None, None])
    bias = jnp.where(left | right, NEG_INF, bias)

    halo_blocks = t // ATT_HALO
    per_tile = ATT_TILE // ATT_HALO
    mid = WINDOWS[1][1]
    main = pl.BlockSpec((ATT_TILE, LANES), lambda i, j: (i, j))
    prev = pl.BlockSpec((ATT_HALO, LANES), lambda i, j: (jnp.maximum(i * per_tile - 1, 0), j))
    nxt = pl.BlockSpec((ATT_HALO, LANES), lambda i, j: (jnp.minimum((i + 1) * per_tile, halo_blocks - 1), j))
    n_q = nb * ATT_TILE
    return pl.pallas_call(
        functools.partial(_attn_kernel, seq_len // ATT_TILE),
        grid=(t // ATT_TILE, attn_w // LANES),
        in_specs=[main, main, prev, nxt, main, prev, nxt,
                  pl.BlockSpec((None, nb, 4, pair * ATT_QB, 2 * ATT_QB), lambda i, j: (j, 0, 0, 0, 0))],
        out_specs=main,
        out_shape=jax.ShapeDtypeStruct((t, attn_w), F32),
        scratch_shapes=[pltpu.VMEM((n_q, LANES), BF16),
                        pltpu.VMEM((krows // ATT_QB, LANES, ATT_QB), BF16),
                        pltpu.VMEM((krows, 2 * LANES), BF16),
                        pltpu.VMEM((n_q, LANES), F32), pltpu.VMEM((n_q, LANES), F32),
                        pltpu.VMEM((n_q, LANES), F32),
                        pltpu.VMEM((ATT_PIPE, pair * ATT_QB, 2 * ATT_QB), F32),
                        pltpu.VMEM((ATT_PIPE, pair * ATT_QB, 2 * ATT_QB), BF16),
                        pltpu.VMEM((ATT_PIPE, pair * ATT_QB, LANES), F32),
                        pltpu.VMEM(((ATT_TILE + 2 * ATT_HALO), LANES), F32),
                        pltpu.VMEM(((ATT_TILE + 2 * ATT_HALO), LANES), F32),
                        pltpu.VMEM((ATT_TILE, LANES), F32)],
        compiler_params=pltpu.CompilerParams(dimension_semantics=("arbitrary", "arbitrary"),
                                             vmem_limit_bytes=VMEM_LIMIT),
        name="attn",
    )(q, k, k, k, v, v, v, bias)


def _outproj_kernel(n_tiles, sgu_w, n_exp, *refs):
    h2_ref = refs[11]
    i = pl.program_id(0)

    @pl.when(i < n_tiles)
    def _():
        _outproj_tile(sgu_w, n_exp, *refs)

    @pl.when(i >= n_tiles)
    def _():
        h2_ref[...] = jnp.zeros_like(h2_ref)


def _outproj_tile(sgu_w, n_exp, a_ref, attn_ref, x_ref, onb_ref, wo_ref, norm2_ref, wrh_ref, wrl_ref,
                  br_ref, tri_ref, x1_ref, h2_ref, meta_ref, cnt_ref, route_ref, carry):
    i = pl.program_id(0)

    @pl.when(i == 0)
    def _():
        carry[...] = jnp.zeros_like(carry)

    bn = _rms(attn_ref[...], onb_ref[...]).astype(BF16)
    x1 = x_ref[...] + _dot(a_ref[...], wo_ref[0:sgu_w, :]) + _dot(bn, wo_ref[sgu_w:, :])
    x1_ref[...] = x1
    h2 = _rms(x1, norm2_ref[...])
    h2_ref[...] = h2.astype(BF16)
    hi, lo = _split_bf16(h2)
    logits = _dot(hi, wrh_ref[...]) + _dot(lo, wrh_ref[...]) + _dot(hi, wrl_ref[...]) + br_ref[...]

    lane = lax.broadcasted_iota(jnp.int32, logits.shape, 1)
    work = jnp.where(lane < n_exp, logits, -jnp.inf)
    chosen = jnp.zeros(logits.shape, jnp.bool_)
    experts, values = [], []
    for _ in range(TOP_K):
        top = jnp.max(work, axis=-1, keepdims=True)
        idx = jnp.min(jnp.where(work == top, lane, LANES), axis=-1, keepdims=True)
        hit = lane == idx
        chosen = jnp.logical_or(chosen, hit)
        work = jnp.where(hit, -jnp.inf, work)
        experts.append(idx)
        values.append(top)
    exps = [jnp.exp(v - values[0]) for v in values]
    den = exps[0] + exps[1] + exps[2] + exps[3]

    onehot = jnp.where(chosen, 1.0, 0.0)
    before = _dot(tri_ref[...], onehot.astype(BF16)) + carry[0:1, :]
    carry[0:1, :] = carry[0:1, :] + jnp.sum(onehot, axis=0, keepdims=True)
    cnt_ref[...] = carry[...]

    meta = jnp.zeros(logits.shape, F32)
    for kk in range(TOP_K):
        rank = jnp.sum(jnp.where(lane == experts[kk], before, 0.0), axis=-1, keepdims=True)
        meta = jnp.where(lane == kk, experts[kk].astype(F32), meta)
        meta = jnp.where(lane == TOP_K + kk, exps[kk] / den, meta)
        meta = jnp.where(lane == 2 * TOP_K + kk, rank, meta)
    meta_ref[...] = meta
    route_ref[...] = meta.T[0:route_ref.shape[0], :]


def _outproj(a_n, attn, x, out_norm_b, w_o, norm2, w_router, b_router):
    sgu_w = a_n.shape[1]
    attn_w = attn.shape[1]
    t, d_model = x.shape
    n_exp = w_router.shape[1]
    assert n_exp <= LANES and t % TOK_TILE == 0
    wr = jnp.pad(w_router, ((0, 0), (0, LANES - n_exp)))
    wr_hi = wr.astype(BF16)
    wr_lo = (wr - wr_hi.astype(F32)).astype(BF16)
    br = jnp.pad(b_router, (0, LANES - n_exp))[None]
    tri = np.tril(np.ones((TOK_TILE, TOK_TILE), np.float32), -1)
    n_tiles = t // TOK_TILE
    h2_rows = max(t, GATHER_MIN_ROWS)
    assert h2_rows % TOK_TILE == 0
    last = n_tiles - 1

    const = lambda shape: pl.BlockSpec(shape, lambda i: (0,) * len(shape))
    tok = lambda w: pl.BlockSpec((TOK_TILE, w), lambda i: (jnp.minimum(i, last), 0))
    return pl.pallas_call(
        functools.partial(_outproj_kernel, n_tiles, sgu_w, n_exp),
        grid=(h2_rows // TOK_TILE,),
        in_specs=[
            tok(sgu_w), tok(attn_w), tok(d_model),
            const((1, attn_w)), const(w_o.shape), const((1, d_model)), const((d_model, LANES)),
            const((d_model, LANES)), const((1, LANES)), const((TOK_TILE, TOK_TILE)),
        ],
        out_specs=[tok(d_model), pl.BlockSpec((TOK_TILE, d_model), lambda i: (i, 0)), tok(LANES), const((8, LANES)),
                   pl.BlockSpec((4 * TOP_K, TOK_TILE), lambda i: (0, jnp.minimum(i, last)))],
        out_shape=[jax.ShapeDtypeStruct((t, d_model), F32), jax.ShapeDtypeStruct((h2_rows, d_model), BF16),
                   jax.ShapeDtypeStruct((t, LANES), F32), jax.ShapeDtypeStruct((8, LANES), F32),
                   jax.ShapeDtypeStruct((4 * TOP_K, t), F32)],
        scratch_shapes=[pltpu.VMEM((8, LANES), F32)],
        compiler_params=pltpu.CompilerParams(dimension_semantics=("arbitrary",), vmem_limit_bytes=VMEM_LIMIT),
        name="outproj",
    )(a_n, attn, x, out_norm_b[None], w_o.astype(BF16), norm2[None], wr_hi, wr_lo, br, jnp.asarray(tri, BF16))


def _moe_kernel(d_exp, blk_ref, exp_ref, lo_ref, hi_ref, slot_ref, nxt_ref, x_ref, wgu_hbm, bgu_ref, wd_hbm, bd_ref,
                y_ref, wgu_f, wd_f, wgu_s, wd_s, act_s, sem):
    it = pl.program_id(0)
    prev = jnp.maximum(it - 1, 0)
    new_expert = jnp.logical_or(it == 0, exp_ref[it] != exp_ref[prev])
    new_block = jnp.logical_or(it == 0, blk_ref[it] != blk_ref[prev])
    lo, hi = lo_ref[it], hi_ref[it]
    active = hi > lo

    def weight_copies(e, slot):
        return (pltpu.make_async_copy(wgu_hbm.at[e], wgu_f.at[slot], sem.at[0, slot]),
                pltpu.make_async_copy(wd_hbm.at[e], wd_f.at[slot], sem.at[1, slot]))

    @pl.when(it == 0)
    def _():
        for c in weight_copies(exp_ref[0], slot_ref[0]):
            c.start()

    @pl.when(jnp.logical_and(new_expert, active))
    def _():
        slot = slot_ref[it]
        for c in weight_copies(exp_ref[it], slot):
            c.wait()

        @pl.when(nxt_ref[it] >= 0)
        def _():
            for c in weight_copies(nxt_ref[it], 1 - slot):
                c.start()

        wgu_s[...] = wgu_f[slot].astype(BF16)
        wd_s[...] = wd_f[slot].astype(BF16)

    @pl.when(active)
    def _():
        x = x_ref[...]
        step = 512
        for n in range(0, d_exp, step):
            gate = _dot(x, wgu_s[:, n:n + step]) + bgu_ref[:, n:n + step]
            up = _dot(x, wgu_s[:, d_exp + n:d_exp + n + step]) + bgu_ref[:, d_exp + n:d_exp + n + step]
            gate = jnp.minimum(gate, SWIGLU_LIMIT)
            up = jnp.clip(up, -SWIGLU_LIMIT, SWIGLU_LIMIT)
            glu = gate * jax.nn.sigmoid(SWIGLU_ALPHA * gate)
            act_s[:, n:n + step] = ((up + 1.0) * glu).astype(BF16)
        y = (_dot(act_s[...], wd_s[...]) + bd_ref[...]).astype(y_ref.dtype)
        row = lax.broadcasted_iota(jnp.int32, (MOE_ROWS, 1), 0)
        mine = jnp.logical_and(row >= lo, row < hi)

        @pl.when(new_block)
        def _():
            y_ref[...] = jnp.where(mine, y, jnp.zeros_like(y))

        @pl.when(jnp.logical_not(new_block))
        def _():
            y_ref[...] = jnp.where(mine, y, y_ref[...])


def _moe_experts(x_rows, items, w_gu, b_gu, w_down, b_down):
    n_rows, d_model = x_rows.shape
    n_exp, _, two_de = w_gu.shape
    d_exp = two_de // 2
    n_items = items[0].shape[0]
    grid_spec = pltpu.PrefetchScalarGridSpec(
        num_scalar_prefetch=len(items),
        grid=(n_items,),
        in_specs=[
            pl.BlockSpec((MOE_ROWS, d_model), lambda i, blk, ex, *_: (blk[i], 0)),
            pl.BlockSpec(memory_space=pl.ANY),
            pl.BlockSpec((None, 1, two_de), lambda i, blk, ex, *_: (ex[i], 0, 0)),
            pl.BlockSpec(memory_space=pl.ANY),
            pl.BlockSpec((None, 1, d_model), lambda i, blk, ex, *_: (ex[i], 0, 0)),
        ],
        out_specs=pl.BlockSpec((MOE_ROWS, d_model), lambda i, blk, ex, *_: (blk[i], 0)),
        scratch_shapes=[pltpu.VMEM((2, d_model, two_de), F32), pltpu.VMEM((2, d_exp, d_model), F32),
                        pltpu.VMEM((d_model, two_de), BF16), pltpu.VMEM((d_exp, d_model), BF16),
                        pltpu.VMEM((MOE_ROWS, d_exp), BF16), pltpu.SemaphoreType.DMA((2, 2))],
    )
    return pl.pallas_call(
        functools.partial(_moe_kernel, d_exp),
        grid_spec=grid_spec,
        out_shape=jax.ShapeDtypeStruct((n_rows, d_model), BF16),
        compiler_params=pltpu.CompilerParams(dimension_semantics=("arbitrary",), vmem_limit_bytes=VMEM_LIMIT),
        name="moe",
    )(*items, x_rows, w_gu, b_gu[:, None, :], w_down, b_down[:, None, :])


def _route(meta, counts_f, n_exp):
    t = meta.shape[1]
    n_assign = t * TOP_K
    assert n_assign % MOE_ROWS == 0
    top_e = meta[0:TOP_K].astype(jnp.int32)
    rank = meta[2 * TOP_K:3 * TOP_K].astype(jnp.int32)
    counts = counts_f[0, :n_exp].astype(jnp.int32)
    end = jnp.cumsum(counts)
    start = end - counts
    experts = jnp.arange(n_exp, dtype=jnp.int32)
    start_of = jnp.zeros_like(top_e)
    for e in range(n_exp):
        start_of = jnp.where(top_e == e, start[e], start_of)
    pos = (start_of + rank).reshape(-1)

    shift = (n_assign - 1).bit_length()
    assert (n_exp << shift) < 2 ** 31
    assign = jnp.arange(t, dtype=jnp.int32)[None, :] * TOP_K + jnp.arange(TOP_K, dtype=jnp.int32)[:, None]
    keys = ((top_e << shift) + assign).reshape(-1)
    sorted_tok = (jnp.sort(keys) & ((1 << shift) - 1)) // TOP_K

    n_blocks = n_assign // MOE_ROWS
    n_items = n_blocks + n_exp
    first_blk = start // MOE_ROWS
    n_it = jnp.where(counts > 0, (end - 1) // MOE_ROWS - first_blk + 1, 0)
    it_end = jnp.cumsum(n_it)
    it_start = it_end - n_it
    i = jnp.arange(n_items, dtype=jnp.int32)
    valid = i < it_end[-1]
    e_i = jnp.sum(jnp.minimum(i, it_end[-1] - 1)[:, None] >= it_end[None, :], axis=1).astype(jnp.int32)
    e_i = jnp.minimum(e_i, n_exp - 1)
    pick = lambda table: jnp.sum(jnp.where(e_i[:, None] == experts[None, :], table[None, :], 0), axis=1)
    blk_i = jnp.where(valid, pick(first_blk) + i - pick(it_start), n_blocks - 1)
    lo_i = jnp.where(valid, jnp.clip(pick(start) - blk_i * MOE_ROWS, 0, MOE_ROWS), 0)
    hi_i = jnp.where(valid, jnp.clip(pick(end) - blk_i * MOE_ROWS, 0, MOE_ROWS), 0)
    has = counts > 0
    slot_e = (jnp.cumsum(has.astype(jnp.int32)) - 1) % 2
    later = jnp.where(jnp.logical_and(has[None, :], experts[None, :] > experts[:, None]), experts[None, :], n_exp)
    next_e = jnp.min(later, axis=1)
    next_e = jnp.where(next_e == n_exp, -1, next_e)
    items = tuple(a.astype(jnp.int32) for a in (blk_i, e_i, lo_i, hi_i, pick(slot_e), pick(next_e)))
    return pos, sorted_tok, items


def _combine_kernel(x1_ref, yg_ref, meta_ref, o_ref):
    acc = x1_ref[...]
    for kk in range(TOP_K):
        gate = meta_ref[:, TOP_K + kk:TOP_K + kk + 1]
        acc = acc + gate * yg_ref[kk].astype(F32)
    o_ref[...] = acc


def _combine(x1, yg, meta):
    rows, d_model = x1.shape
    tok = lambda w: pl.BlockSpec((TOK_TILE, w), lambda i: (i, 0))
    return pl.pallas_call(
        _combine_kernel,
        grid=(rows // TOK_TILE,),
        in_specs=[tok(d_model), pl.BlockSpec((TOP_K, TOK_TILE, d_model), lambda i: (0, i, 0)), tok(LANES)],
        out_specs=tok(d_model),
        out_shape=jax.ShapeDtypeStruct((rows, d_model), F32),
        compiler_params=pltpu.CompilerParams(dimension_semantics=("arbitrary",), vmem_limit_bytes=VMEM_LIMIT),
        name="combine",
    )(x1, yg, meta)


def kernel(x_prompt, x_sample, norm1, w_in, q_gain, k_gain, rel_bias, sgu_norm, w_s, b_s, out_norm_a, out_norm_b,
           w_o, norm2, w_router, b_router, w_gu, b_gu, w_down, b_down):
    assert norm1.shape[0] == 1, "single-layer trunk"
    bp, sp, d_model = x_prompt.shape
    bs, ss, _ = x_sample.shape
    n_exp = w_router.shape[-1]

    batches = sorted(((x_prompt.reshape(bp * sp, d_model), sp, 0), (x_sample.reshape(bs * ss, d_model), ss, 1)),
                     key=lambda c: -c[0].shape[0])
    staged, counts = [], None
    for x, seq_len, key in batches:
        if counts is not None:
            x, _ = lax.optimization_barrier((x, counts))
        q, k, v, a_n = _inproj(x, norm1[0], w_in[0], q_gain[0], k_gain[0], sgu_norm[0], w_s[0], b_s[0], out_norm_a[0])
        attn = _attention(q, k, v, rel_bias, seq_len)
        x1, h2, meta, counts, route = _outproj(a_n, attn, x, out_norm_b[0], w_o[0], norm2[0], w_router[0],
                                               b_router[0])
        pos, sorted_tok, items = _route(route, counts, n_exp)
        staged.append((key, x1, meta, pos, items, h2.at[sorted_tok].get(mode="promise_in_bounds")))

    outs, order_y, order_out = {}, counts, None
    for key, x1, meta, pos, items, x_rows in staged:
        x_rows, _ = lax.optimization_barrier((x_rows, order_y))
        y_rows = _moe_experts(x_rows, items, w_gu[0], b_gu[0], w_down[0], b_down[0])
        yg = y_rows.at[pos].get(mode="promise_in_bounds").reshape(TOP_K, x1.shape[0], d_model)
        if order_out is not None:
            yg, _ = lax.optimization_barrier((yg, order_out))
        outs[key] = _combine(x1, yg, meta)
        order_y, order_out = y_rows, outs[key]
    return (outs[0].reshape(bp, sp, d_model), outs[1].reshape(bs, ss, d_model))
```

```python
import functools
import math

import numpy as np
import jax
import jax.numpy as jnp
from jax import lax
from jax.experimental import pallas as pl
from jax.experimental.pallas import tpu as pltpu

F32 = jnp.float32
BF16 = jnp.bfloat16

EPS = 1e-6
NEG_INF = -1e30
HEAD_DIM = 64
SGU_CHUNK = 128
WINDOWS = ((128, 1), (512, 4), (2048, 16))
N_BUCKETS = 32
MAX_DISTANCE = 1024
TOP_K = 4
SWIGLU_LIMIT = 7.0
SWIGLU_ALPHA = 1.702

LANES = 128
TOK_TILE = 512
ATT_TILE = 2048
ATT_HALO = 1024
ATT_QB = 128
ATT_SIDE = 64
ATT_PIPE = 8
MOE_ROWS = 512
GATHER_MIN_ROWS = 32768
VMEM_LIMIT = 56 * 1024 * 1024


def _dot(a, b):
    return jnp.dot(a, b, preferred_element_type=F32)


def _dot_nt(a, b):
    return lax.dot_general(a, b, (((1,), (1,)), ((), ())), preferred_element_type=F32)


def _rms(x, g):
    ms = jnp.mean(x * x, axis=-1, keepdims=True)
    return x * lax.rsqrt(ms + EPS) * g


def _gelu(x):
    return 0.5 * x * (1.0 + lax.erf(x * (1.0 / math.sqrt(2.0))))


def _split_bf16(x):
    hi = x.astype(BF16)
    lo = (x - hi.astype(F32)).astype(BF16)
    return hi, lo


def _inproj_kernel(attn_w, sgu_w, x_ref, norm1_ref, win_ref, gsum_ref, gexp_ref,
                   qkg_ref, sgun_ref, wsp_ref, bsb_ref, ona_ref, q_ref, k_ref, v_ref, a_ref):
    h = _rms(x_ref[...], norm1_ref[...]).astype(BF16)

    zqk = _dot(h, win_ref[:, 0:2 * attn_w])
    sq_hi, sq_lo = _split_bf16(zqk * zqk)
    ss = _dot(sq_hi, gsum_ref[...]) + _dot(sq_lo, gsum_ref[...])
    inv = lax.rsqrt(ss * (1.0 / HEAD_DIM) + EPS)
    inv_hi, inv_lo = _split_bf16(inv)
    invb = _dot(inv_hi, gexp_ref[...]) + _dot(inv_lo, gexp_ref[...])
    qk = zqk * invb * qkg_ref[...]
    q_ref[...] = qk[:, 0:attn_w]
    k_ref[...] = qk[:, attn_w:2 * attn_w]
    v_ref[...] = _dot(h, win_ref[:, 2 * attn_w:3 * attn_w])

    c0 = 3 * attn_w
    u = _gelu(_dot(h, win_ref[:, c0:c0 + sgu_w]))
    gv = _gelu(_dot(h, win_ref[:, c0 + sgu_w:c0 + 2 * sgu_w]))
    vsn = _rms(gv, sgun_ref[...]).astype(BF16)
    lane = lax.broadcasted_iota(jnp.int32, (SGU_CHUNK, LANES), 1)
    lo_half = lane < HEAD_DIM
    zero = jnp.zeros((SGU_CHUNK, LANES), BF16)
    for c in range(TOK_TILE // SGU_CHUNK):
        r0 = c * SGU_CHUNK
        parts = []
        for j in range(sgu_w // LANES):
            blk = vsn[r0:r0 + SGU_CHUNK, j * LANES:(j + 1) * LANES]
            rhs = jnp.concatenate([jnp.where(lo_half, blk, zero), jnp.where(lo_half, zero, blk)], axis=0)
            parts.append(_dot(wsp_ref[j], rhs))
        s = jnp.concatenate(parts, axis=1) + bsb_ref[...]
        a = u[r0:r0 + SGU_CHUNK, :] * s
        a_ref[r0:r0 + SGU_CHUNK, :] = _rms(a, ona_ref[...]).astype(BF16)


def _inproj(x, norm1, w_in, q_gain, k_gain, sgu_norm, w_s, b_s, out_norm_a):
    t, d_model = x.shape
    n_heads_w = q_gain.shape[0]
    sgu_w = sgu_norm.shape[0]
    attn_w = (w_in.shape[1] - 2 * sgu_w) // 3
    n_heads = attn_w // n_heads_w
    n_groups = w_s.shape[0]
    assert n_heads_w == HEAD_DIM and sgu_w // n_groups == HEAD_DIM and w_s.shape[1] == SGU_CHUNK
    assert t % TOK_TILE == 0 and 2 * n_heads <= LANES

    heads = np.arange(2 * attn_w) // HEAD_DIM
    gsum = (heads[:, None] == np.arange(LANES)[None, :]).astype(np.float32)
    gexp = gsum.T
    qkg = jnp.concatenate([jnp.tile(q_gain, n_heads) * (HEAD_DIM ** -0.5), jnp.tile(k_gain, n_heads)])
    wsp = jnp.concatenate([w_s[0::2], w_s[1::2]], axis=2).astype(BF16)
    bsb = jnp.repeat(b_s.T, HEAD_DIM, axis=1)

    const = lambda shape: pl.BlockSpec(shape, lambda i: (0,) * len(shape))
    tok = lambda w: pl.BlockSpec((TOK_TILE, w), lambda i: (i, 0))
    return pl.pallas_call(
        functools.partial(_inproj_kernel, attn_w, sgu_w),
        grid=(t // TOK_TILE,),
        in_specs=[
            tok(d_model), const((1, d_model)), const(w_in.shape), const((2 * attn_w, LANES)), const((LANES, 2 * attn_w)),
            const((1, 2 * attn_w)), const((1, sgu_w)), const(wsp.shape), const((SGU_CHUNK, sgu_w)),
            const((1, sgu_w)),
        ],
        out_specs=[tok(attn_w), tok(attn_w), tok(attn_w), tok(sgu_w)],
        out_shape=[jax.ShapeDtypeStruct((t, attn_w), F32)] * 3 + [jax.ShapeDtypeStruct((t, sgu_w), BF16)],
        compiler_params=pltpu.CompilerParams(dimension_semantics=("arbitrary",), vmem_limit_bytes=VMEM_LIMIT),
        name="inproj",
    )(x, norm1[None], w_in.astype(BF16), jnp.asarray(gsum, BF16), jnp.asarray(gexp, BF16),
      qkg[None], sgu_norm[None], wsp, bsb, out_norm_a[None])


def _t5_bucket(rel):
    nb = N_BUCKETS // 2
    bucket = (rel > 0).astype(np.int32) * nb
    n = np.abs(rel)
    max_exact = nb // 2
    large = max_exact + (np.log(np.maximum(n, 1) / max_exact)
                         / np.log(MAX_DISTANCE / max_exact) * (nb - max_exact)).astype(np.int32)
    large = np.minimum(large, nb - 1)
    return (bucket + np.where(n < max_exact, n, large)).astype(np.int32)


def _branch_layout():
    out, kbase = [], 0
    for bi, (window, dil) in enumerate(WINDOWS):
        assert window // (2 * dil) == ATT_SIDE
        m = ATT_TILE // dil
        seg = m + 2 * ATT_SIDE
        out.append((dil, m, seg, kbase, bi * ATT_TILE))
        kbase += dil * seg
    return out, kbase


def _strided(ref, start, size, stride):
    if stride == 1:
        return ref[start:start + size, :]
    return ref[pl.ds(start, size, stride=stride), :]


def _attn_kernel(per_seq, q_ref, km_ref, kp_ref, kn_ref, vm_ref, vp_ref, vn_ref,
                 bias_ref, o_ref, qs, kts, vs, o_scr, d_scr, l_scr, s_scr, p_scr, m_scr, k4, v4, q4):
    i = pl.program_id(0)
    w = i % per_seq
    left_ok = w > 0
    right_ok = w < per_seq - 1
    layout, _ = _branch_layout()
    mid = WINDOWS[1][1]
    assert WINDOWS[2][1] == mid * mid
    halo4, main4 = ATT_HALO // mid, ATT_TILE // mid
    win4 = 2 * halo4 + main4

    @pl.when(jnp.logical_and(i == 0, pl.program_id(1) == 0))
    def _():
        vs[:, LANES:] = jnp.ones((vs.shape[0], LANES), BF16)

    for r in range(mid):
        for dst, main, prev, nxt in ((k4, km_ref, kp_ref, kn_ref), (v4, vm_ref, vp_ref, vn_ref)):
            dst[r * win4:r * win4 + halo4, :] = _strided(prev, r, halo4, mid)
            dst[r * win4 + halo4:r * win4 + halo4 + main4, :] = _strided(main, r, main4, mid)
            dst[r * win4 + halo4 + main4:(r + 1) * win4, :] = _strided(nxt, r, halo4, mid)
        q4[r * main4:(r + 1) * main4, :] = _strided(q_ref, r, main4, mid)

    def key_rows(src_k, src_v, start, stride, kbase_rows):
        kc = _strided(src_k, start, ATT_QB, stride)
        kts[kbase_rows // ATT_QB] = kc.T.astype(BF16)
        vs[kbase_rows:kbase_rows + ATT_QB, 0:LANES] = _strided(src_v, start, ATT_QB, stride).astype(BF16)

    for bi, (dil, m, seg, kbase, qbase) in enumerate(layout):
        nchunk = seg // ATT_QB
        for r in range(dil):
            o = kbase + r * seg
            if dil == 1:
                for c in range(nchunk):
                    lo = c * ATT_QB - ATT_SIDE
                    if c == 0:
                        kc = jnp.concatenate([kp_ref[ATT_HALO - ATT_SIDE:, :], km_ref[0:ATT_QB - ATT_SIDE, :]], axis=0)
                        vc = jnp.concatenate([vp_ref[ATT_HALO - ATT_SIDE:, :], vm_ref[0:ATT_QB - ATT_SIDE, :]], axis=0)
                    elif c == nchunk - 1:
                        kc = jnp.concatenate([km_ref[lo:, :], kn_ref[0:ATT_SIDE, :]], axis=0)
                        vc = jnp.concatenate([vm_ref[lo:, :], vn_ref[0:ATT_SIDE, :]], axis=0)
                    else:
                        kc = km_ref[lo:lo + ATT_QB, :]
                        vc = vm_ref[lo:lo + ATT_QB, :]
                    kts[(o + c * ATT_QB) // ATT_QB] = kc.T.astype(BF16)
                    vs[o + c * ATT_QB:o + (c + 1) * ATT_QB, 0:LANES] = vc.astype(BF16)
                qs[qbase:qbase + m, :] = q_ref[...].astype(BF16)
            else:
                r4, s = r % mid, r // mid
                step = dil // mid
                first = r4 * win4 + (halo4 - ATT_SIDE * step) + s
                for c in range(nchunk):
                    key_rows(k4, v4, first + c * ATT_QB * step, step, o + c * ATT_QB)
                qs[qbase + r * m:qbase + (r + 1) * m, :] = _strided(q4, r4 * main4 + s, m, step).astype(BF16)

    lane = lax.broadcasted_iota(jnp.int32, (ATT_QB, LANES), 1)
    head0 = lane < HEAD_DIM
    cells = ATT_TILE // ATT_QB
    total = len(layout) * cells
    blk_bits = [(m // ATT_QB).bit_length() - 1 for _, m, _, _, _ in layout]
    seg_chunks = [seg // ATT_QB for _, _, seg, _, _ in layout]
    base_chunks = [kbase // ATT_QB for _, _, _, kbase, _ in layout]

    def cell_params(g):
        bi = g // cells
        idx = g % cells
        if isinstance(g, int):
            pick = lambda vals: vals[bi]
        else:
            pick = lambda vals: jnp.where(bi == 0, vals[0], jnp.where(bi == 1, vals[1], vals[2]))
        bits = pick(blk_bits)
        last_blk = (1 << bits) - 1
        blk = idx & last_blk
        return bi, blk, last_blk, pick(base_chunks) + (idx >> bits) * pick(seg_chunks) + blk

    def stage_logits(g, slot):
        bi, blk, last_blk, kchunk = cell_params(g)
        left_bad = jnp.logical_and(blk == 0, jnp.logical_not(left_ok))
        right_bad = jnp.logical_and(blk == last_blk, jnp.logical_not(right_ok))
        variant = left_bad.astype(jnp.int32) + 2 * right_bad.astype(jnp.int32)
        qc = qs[pl.ds(pl.multiple_of(g * ATT_QB, ATT_QB), ATT_QB), :]
        zero = jnp.zeros_like(qc)
        lhs = jnp.concatenate([jnp.where(head0, qc, zero), jnp.where(head0, zero, qc)], axis=0)
        kt = jnp.concatenate([kts[kchunk], kts[kchunk + 1]], axis=1)
        s_scr[slot] = _dot(lhs, kt) + bias_ref[bi, variant]

    def stage_softmax(slot):
        part = ATT_QB // 2
        for c in range(2 * ATT_QB // part):
            s = s_scr[slot, c * part:(c + 1) * part, :]
            mx = jnp.max(s, axis=-1, keepdims=True)
            p_scr[slot, c * part:(c + 1) * part, :] = jnp.exp(s - mx).astype(BF16)
            m_scr[slot, c * part:(c + 1) * part, :] = jnp.broadcast_to(mx, (part, LANES))

    def stage_values(g, slot):
        _, _, _, kchunk = cell_params(g)
        koff = pl.multiple_of(kchunk * ATT_QB, ATT_QB)
        r = _dot(p_scr[slot], vs[pl.ds(koff, 2 * ATT_QB), :])
        num = jnp.where(head0, r[0:ATT_QB, 0:LANES], r[ATT_QB:, 0:LANES])
        den = jnp.where(head0, r[0:ATT_QB, LANES:], r[ATT_QB:, LANES:])
        mx = jnp.where(head0, m_scr[slot, 0:ATT_QB, :], m_scr[slot, ATT_QB:, :])
        rows = pl.ds(pl.multiple_of(g * ATT_QB, ATT_QB), ATT_QB)
        o_scr[rows, :] = num
        d_scr[rows, :] = den
        l_scr[rows, :] = mx

    width = ATT_PIPE

    def group(c, do_values, do_softmax, do_logits):
        if do_values:
            for u in range(width):
                stage_values(c - width + u, u)
        if do_softmax:
            for u in range(width):
                stage_softmax(u)
        if do_logits:
            for u in range(width):
                stage_logits(c + width + u, u)

    group(-width, False, False, True)
    group(0, False, True, True)

    def trip(t, carry):
        group((t + 1) * width, True, True, True)
        return carry

    lax.fori_loop(0, total // width - 2, trip, 0)
    group(total - width, True, True, False)
    group(total, True, False, False)

    big = WINDOWS[-1][1]
    rows = ATT_TILE // big
    dil0, m0, _, _, qb0 = layout[0]
    assert dil0 == 1 and big == mid * mid
    for r4 in range(mid):
        k4[r4 * main4:(r4 + 1) * main4, :] = _strided(o_scr, qb0 + r4, main4, mid)
        k4[m0 + r4 * main4:m0 + (r4 + 1) * main4, :] = _strided(d_scr, qb0 + r4, main4, mid)
        v4[r4 * main4:(r4 + 1) * main4, :] = _strided(l_scr, qb0 + r4, main4, mid)
    _, m1, _, _, qb1 = layout[1]
    _, m2, _, _, qb2 = layout[2]
    for r in range(big):
        r4, s = r % mid, r // mid
        first = r4 * main4 + s
        nums = [_strided(k4, first, rows, mid), _strided(o_scr, qb1 + r4 * m1 + s, rows, mid),
                o_scr[qb2 + r * m2:qb2 + (r + 1) * m2, :]]
        dens = [_strided(k4, m0 + first, rows, mid), _strided(d_scr, qb1 + r4 * m1 + s, rows, mid),
                d_scr[qb2 + r * m2:qb2 + (r + 1) * m2, :]]
        mxs = [_strided(v4, first, rows, mid), _strided(l_scr, qb1 + r4 * m1 + s, rows, mid),
               l_scr[qb2 + r * m2:qb2 + (r + 1) * m2, :]]
        top = jnp.maximum(jnp.maximum(mxs[0], mxs[1]), mxs[2])
        ws = [jnp.exp(mx - top) for mx in mxs]
        num = ws[0] * nums[0] + ws[1] * nums[1] + ws[2] * nums[2]
        den = ws[0] * dens[0] + ws[1] * dens[1] + ws[2] * dens[2]
        q4[pl.ds(first, rows, stride=mid), :] = num / den
    for r4 in range(mid):
        o_ref[pl.ds(r4, main4, stride=mid), :] = q4[r4 * main4:(r4 + 1) * main4, :]


def _attention(q, k, v, rel_bias, seq_len):
    t, attn_w = q.shape
    n_heads = attn_w // HEAD_DIM
    pair = LANES // HEAD_DIM
    assert seq_len % ATT_TILE == 0 and t % seq_len == 0 and ATT_TILE == 2 * ATT_HALO
    assert ATT_TILE // WINDOWS[-1][1] == ATT_QB and pair == 2
    layout, krows = _branch_layout()

    nb = len(WINDOWS)
    span = 2 * ATT_SIDE + 1
    jj = np.arange(2 * ATT_QB)[None, :]
    offsets = np.arange(span) - ATT_SIDE
    table_bucket = np.stack([_t5_bucket(offsets * dil) for _, dil in WINDOWS])
    table = jnp.zeros((n_heads, nb, span), F32)
    for b in range(N_BUCKETS):
        table = jnp.where(jnp.asarray(table_bucket == b)[None], rel_bias[b].astype(F32)[:, None, None], table)
    row_len = 3 * ATT_QB
    period = row_len + 1
    padded = jnp.concatenate([table, jnp.full((n_heads, nb, period - span), NEG_INF, F32)], axis=-1)
    flat = jnp.tile(padded, (1, 1, ATT_QB + 1))[..., :ATT_QB * row_len]
    bias = flat.reshape(n_heads, nb, ATT_QB, row_len)[..., :2 * ATT_QB]
    bias = bias.reshape(n_heads // pair, pair, nb, ATT_QB, 2 * ATT_QB).transpose(0, 2, 1, 3, 4)
    bias = bias.reshape(n_heads // pair, nb, 1, pair * ATT_QB, 2 * ATT_QB)
    left = jnp.asarray((jj < ATT_SIDE)[None, None, None] & (np.arange(4) % 2 == 1)[None, None, :, None, None])
    right = jnp.asarray((jj >= 2 * ATT_QB - ATT_SIDE)[None, None, None]
                        & (np.arange(4) // 2 == 1)[None, None, :, None, None])
    bias = jnp.where(left | right, NEG_INF, bias)

    halo_blocks = t // ATT_HALO
    per_tile = ATT_TILE // ATT_HALO
    mid = WINDOWS[1][1]
    main = pl.BlockSpec((ATT_TILE, LANES), lambda i, j: (i, j))
    prev = pl.BlockSpec((ATT_HALO, LANES), lambda i, j: (jnp.maximum(i * per_tile - 1, 0), j))
    nxt = pl.BlockSpec((ATT_HALO, LANES), lambda i, j: (jnp.minimum((i + 1) * per_tile, halo_blocks - 1), j))
    n_q = nb * ATT_TILE
    return pl.pallas_call(
        functools.partial(_attn_kernel, seq_len // ATT_TILE),
        grid=(t // ATT_TILE, attn_w // LANES),
        in_specs=[main, main, prev, nxt, main, prev, nxt,
                  pl.BlockSpec((None, nb, 4, pair * ATT_QB, 2 * ATT_QB), lambda i, j: (j, 0, 0, 0, 0))],
        out_specs=main,
        out_shape=jax.ShapeDtypeStruct((t, attn_w), F32),
        scratch_shapes=[pltpu.VMEM((n_q, LANES), BF16),
                        pltpu.VMEM((krows // ATT_QB, LANES, ATT_QB), BF16),
                        pltpu.VMEM((krows, 2 * LANES), BF16),
                        pltpu.VMEM((n_q, LANES), F32), pltpu.VMEM((n_q, LANES), F32),
                        pltpu.VMEM((n_q, LANES), F32),
                        pltpu.VMEM((ATT_PIPE, pair * ATT_QB, 2 * ATT_QB), F32),
                        pltpu.VMEM((ATT_PIPE, pair * ATT_QB, 2 * ATT_QB), BF16),
                        pltpu.VMEM((ATT_PIPE, pair * ATT_QB, LANES), F32),
                        pltpu.VMEM(((ATT_TILE + 2 * ATT_HALO), LANES), F32),
                        pltpu.VMEM(((ATT_TILE + 2 * ATT_HALO), LANES), F32),
                        pltpu.VMEM((ATT_TILE, LANES), F32)],
        compiler_params=pltpu.CompilerParams(dimension_semantics=("arbitrary", "arbitrary"),
                                             vmem_limit_bytes=VMEM_LIMIT),
        name="attn",
    )(q, k, k, k, v, v, v, bias)


def _outproj_kernel(n_tiles, sgu_w, n_exp, *refs):
    h2_ref = refs[11]
    i = pl.program_id(0)

    @pl.when(i < n_tiles)
    def _():
        _outproj_tile(sgu_w, n_exp, *refs)

    @pl.when(i >= n_tiles)
    def _():
        h2_ref[...] = jnp.zeros_like(h2_ref)


def _outproj_tile(sgu_w, n_exp, a_ref, attn_ref, x_ref, onb_ref, wo_ref, norm2_ref, wrh_ref, wrl_ref,
                  br_ref, tri_ref, x1_ref, h2_ref, meta_ref, cnt_ref, route_ref, carry):
    i = pl.program_id(0)

    @pl.when(i == 0)
    def _():
        carry[...] = jnp.zeros_like(carry)

    bn = _rms(attn_ref[...], onb_ref[...]).astype(BF16)
    x1 = x_ref[...] + _dot(a_ref[...], wo_ref[0:sgu_w, :]) + _dot(bn, wo_ref[sgu_w:, :])
    x1_ref[...] = x1
    h2 = _rms(x1, norm2_ref[...])
    h2_ref[...] = h2.astype(BF16)
    hi, lo = _split_bf16(h2)
    logits = _dot(hi, wrh_ref[...]) + _dot(lo, wrh_ref[...]) + _dot(hi, wrl_ref[...]) + br_ref[...]

    lane = lax.broadcasted_iota(jnp.int32, logits.shape, 1)
    work = jnp.where(lane < n_exp, logits, -jnp.inf)
    chosen = jnp.zeros(logits.shape, jnp.bool_)
    experts, values = [], []
    for _ in range(TOP_K):
        top = jnp.max(work, axis=-1, keepdims=True)
        idx = jnp.min(jnp.where(work == top, lane, LANES), axis=-1, keepdims=True)
        hit = lane == idx
        chosen = jnp.logical_or(chosen, hit)
        work = jnp.where(hit, -jnp.inf, work)
        experts.append(idx)
        values.append(top)
    exps = [jnp.exp(v - values[0]) for v in values]
    den = exps[0] + exps[1] + exps[2] + exps[3]

    onehot = jnp.where(chosen, 1.0, 0.0)
    before = _dot(tri_ref[...], onehot.astype(BF16)) + carry[0:1, :]
    carry[0:1, :] = carry[0:1, :] + jnp.sum(onehot, axis=0, keepdims=True)
    cnt_ref[...] = carry[...]

    meta = jnp.zeros(logits.shape, F32)
    for kk in range(TOP_K):
        rank = jnp.sum(jnp.where(lane == experts[kk], before, 0.0), axis=-1, keepdims=True)
        meta = jnp.where(lane == kk, experts[kk].astype(F32), meta)
        meta = jnp.where(lane == TOP_K + kk, exps[kk] / den, meta)
        meta = jnp.where(lane == 2 * TOP_K + kk, rank, meta)
    meta_ref[...] = meta
    route_ref[...] = meta.T[0:route_ref.shape[0], :]


def _outproj(a_n, attn, x, out_norm_b, w_o, norm2, w_router, b_router):
    sgu_w = a_n.shape[1]
    attn_w = attn.shape[1]
    t, d_model = x.shape
    n_exp = w_router.shape[1]
    assert n_exp <= LANES and t % TOK_TILE == 0
    wr = jnp.pad(w_router, ((0, 0), (0, LANES - n_exp)))
    wr_hi = wr.astype(BF16)
    wr_lo = (wr - wr_hi.astype(F32)).astype(BF16)
    br = jnp.pad(b_router, (0, LANES - n_exp))[None]
    tri = np.tril(np.ones((TOK_TILE, TOK_TILE), np.float32), -1)
    n_tiles = t // TOK_TILE
    h2_rows = max(t, GATHER_MIN_ROWS)
    assert h2_rows % TOK_TILE == 0
    last = n_tiles - 1

    const = lambda shape: pl.BlockSpec(shape, lambda i: (0,) * len(shape))
    tok = lambda w: pl.BlockSpec((TOK_TILE, w), lambda i: (jnp.minimum(i, last), 0))
    return pl.pallas_call(
        functools.partial(_outproj_kernel, n_tiles, sgu_w, n_exp),
        grid=(h2_rows // TOK_TILE,),
        in_specs=[
            tok(sgu_w), tok(attn_w), tok(d_model),
            const((1, attn_w)), const(w_o.shape), const((1, d_model)), const((d_model, LANES)),
            const((d_model, LANES)), const((1, LANES)), const((TOK_TILE, TOK_TILE)),
        ],
        out_specs=[tok(d_model), pl.BlockSpec((TOK_TILE, d_model), lambda i: (i, 0)), tok(LANES), const((8, LANES)),
                   pl.BlockSpec((4 * TOP_K, TOK_TILE), lambda i: (0, jnp.minimum(i, last)))],
        out_shape=[jax.ShapeDtypeStruct((t, d_model), F32), jax.ShapeDtypeStruct((h2_rows, d_model), BF16),
                   jax.ShapeDtypeStruct((t, LANES), F32), jax.ShapeDtypeStruct((8, LANES), F32),
                   jax.ShapeDtypeStruct((4 * TOP_K, t), F32)],
        scratch_shapes=[pltpu.VMEM((8, LANES), F32)],
        compiler_params=pltpu.CompilerParams(dimension_semantics=("arbitrary",), vmem_limit_bytes=VMEM_LIMIT),
        name="outproj",
    )(a_n, attn, x, out_norm_b[None], w_o.astype(BF16), norm2[None], wr_hi, wr_lo, br, jnp.asarray(tri, BF16))


def _moe_kernel(d_exp, blk_ref, exp_ref, lo_ref, hi_ref, slot_ref, nxt_ref, x_ref, wgu_hbm, bgu_ref, wd_hbm, bd_ref,
                y_ref, wgu_f, wd_f, wgu_s, wd_s, act_s, sem):
    it = pl.program_id(0)
    prev = jnp.maximum(it - 1, 0)
    new_expert = jnp.logical_or(it == 0, exp_ref[it] != exp_ref[prev])
    new_block = jnp.logical_or(it == 0, blk_ref[it] != blk_ref[prev])
    lo, hi = lo_ref[it], hi_ref[it]
    active = hi > lo

    def weight_copies(e, slot):
        return (pltpu.make_async_copy(wgu_hbm.at[e], wgu_f.at[slot], sem.at[0, slot]),
                pltpu.make_async_copy(wd_hbm.at[e], wd_f.at[slot], sem.at[1, slot]))

    @pl.when(it == 0)
    def _():
        for c in weight_copies(exp_ref[0], slot_ref[0]):
            c.start()

    @pl.when(jnp.logical_and(new_expert, active))
    def _():
        slot = slot_ref[it]
        for c in weight_copies(exp_ref[it], slot):
            c.wait()

        @pl.when(nxt_ref[it] >= 0)
        def _():
            for c in weight_copies(nxt_ref[it], 1 - slot):
                c.start()

        wgu_s[...] = wgu_f[slot].astype(BF16)
        wd_s[...] = wd_f[slot].astype(BF16)

    def expert_rows(off, m):
        rows = pl.ds(off, m)
        x = x_ref[rows, :]
        step = 512
        for n in range(0, d_exp, step):
            gate = _dot(x, wgu_s[:, n:n + step]) + bgu_ref[:, n:n + step]
            up = _dot(x, wgu_s[:, d_exp + n:d_exp + n + step]) + bgu_ref[:, d_exp + n:d_exp + n + step]
            gate = jnp.minimum(gate, SWIGLU_LIMIT)
            up = jnp.clip(up, -SWIGLU_LIMIT, SWIGLU_LIMIT)
            glu = gate * jax.nn.sigmoid(SWIGLU_ALPHA * gate)
            act_s[0:m, n:n + step] = ((up + 1.0) * glu).astype(BF16)
        y = (_dot(act_s[0:m, :], wd_s[...]) + bd_ref[...]).astype(y_ref.dtype)
        row = off + lax.broadcasted_iota(jnp.int32, (m, 1), 0)
        mine = jnp.logical_and(row >= lo, row < hi)

        @pl.when(new_block)
        def _():
            y_ref[rows, :] = jnp.where(mine, y, jnp.zeros_like(y))

        @pl.when(jnp.logical_not(new_block))
        def _():
            y_ref[rows, :] = jnp.where(mine, y, y_ref[rows, :])

    half = MOE_ROWS // 2
    in_upper = lo >= half
    one_half = jnp.logical_and(active, jnp.logical_or(hi <= half, in_upper))

    @pl.when(jnp.logical_and(active, jnp.logical_not(one_half)))
    def _():
        expert_rows(0, MOE_ROWS)

    @pl.when(one_half)
    def _():
        off = pl.multiple_of(jnp.where(in_upper, half, 0), half)
        expert_rows(off, half)

        @pl.when(new_block)
        def _():
            y_ref[pl.ds(pl.multiple_of(half - off, half), half), :] = jnp.zeros((half, y_ref.shape[1]), y_ref.dtype)


def _moe_experts(x_rows, items, w_gu, b_gu, w_down, b_down):
    n_rows, d_model = x_rows.shape
    n_exp, _, two_de = w_gu.shape
    d_exp = two_de // 2
    n_items = items[0].shape[0]
    grid_spec = pltpu.PrefetchScalarGridSpec(
        num_scalar_prefetch=len(items),
        grid=(n_items,),
        in_specs=[
            pl.BlockSpec((MOE_ROWS, d_model), lambda i, blk, ex, *_: (blk[i], 0)),
            pl.BlockSpec(memory_space=pl.ANY),
            pl.BlockSpec((None, 1, two_de), lambda i, blk, ex, *_: (ex[i], 0, 0)),
            pl.BlockSpec(memory_space=pl.ANY),
            pl.BlockSpec((None, 1, d_model), lambda i, blk, ex, *_: (ex[i], 0, 0)),
        ],
        out_specs=pl.BlockSpec((MOE_ROWS, d_model), lambda i, blk, ex, *_: (blk[i], 0)),
        scratch_shapes=[pltpu.VMEM((2, d_model, two_de), F32), pltpu.VMEM((2, d_exp, d_model), F32),
                        pltpu.VMEM((d_model, two_de), BF16), pltpu.VMEM((d_exp, d_model), BF16),
                        pltpu.VMEM((MOE_ROWS, d_exp), BF16), pltpu.SemaphoreType.DMA((2, 2))],
    )
    return pl.pallas_call(
        functools.partial(_moe_kernel, d_exp),
        grid_spec=grid_spec,
        out_shape=jax.ShapeDtypeStruct((n_rows, d_model), BF16),
        compiler_params=pltpu.CompilerParams(dimension_semantics=("arbitrary",), vmem_limit_bytes=VMEM_LIMIT),
        name="moe",
    )(*items, x_rows, w_gu, b_gu[:, None, :], w_down, b_down[:, None, :])


def _route(meta, counts_f, n_exp):
    t = meta.shape[1]
    n_assign = t * TOP_K
    assert n_assign % MOE_ROWS == 0
    top_e = meta[0:TOP_K].astype(jnp.int32)
    rank = meta[2 * TOP_K:3 * TOP_K].astype(jnp.int32)
    counts = counts_f[0, :n_exp].astype(jnp.int32)
    end = jnp.cumsum(counts)
    start = end - counts
    experts = jnp.arange(n_exp, dtype=jnp.int32)
    start_of = jnp.zeros_like(top_e)
    for e in range(n_exp):
        start_of = jnp.where(top_e == e, start[e], start_of)
    pos = (start_of + rank).reshape(-1)

    shift = (n_assign - 1).bit_length()
    assert (n_exp << shift) < 2 ** 31
    assign = jnp.arange(t, dtype=jnp.int32)[None, :] * TOP_K + jnp.arange(TOP_K, dtype=jnp.int32)[:, None]
    keys = ((top_e << shift) + assign).reshape(-1)
    sorted_tok = (jnp.sort(keys) & ((1 << shift) - 1)) // TOP_K

    n_blocks = n_assign // MOE_ROWS
    n_items = n_blocks + n_exp
    first_blk = start // MOE_ROWS
    n_it = jnp.where(counts > 0, (end - 1) // MOE_ROWS - first_blk + 1, 0)
    it_end = jnp.cumsum(n_it)
    it_start = it_end - n_it
    i = jnp.arange(n_items, dtype=jnp.int32)
    valid = i < it_end[-1]
    e_i = jnp.sum(jnp.minimum(i, it_end[-1] - 1)[:, None] >= it_end[None, :], axis=1).astype(jnp.int32)
    e_i = jnp.minimum(e_i, n_exp - 1)
    pick = lambda table: jnp.sum(jnp.where(e_i[:, None] == experts[None, :], table[None, :], 0), axis=1)
    blk_i = jnp.where(valid, pick(first_blk) + i - pick(it_start), n_blocks - 1)
    lo_i = jnp.where(valid, jnp.clip(pick(start) - blk_i * MOE_ROWS, 0, MOE_ROWS), 0)
    hi_i = jnp.where(valid, jnp.clip(pick(end) - blk_i * MOE_ROWS, 0, MOE_ROWS), 0)
    has = counts > 0
    slot_e = (jnp.cumsum(has.astype(jnp.int32)) - 1) % 2
    later = jnp.where(jnp.logical_and(has[None, :], experts[None, :] > experts[:, None]), experts[None, :], n_exp)
    next_e = jnp.min(later, axis=1)
    next_e = jnp.where(next_e == n_exp, -1, next_e)
    items = tuple(a.astype(jnp.int32) for a in (blk_i, e_i, lo_i, hi_i, pick(slot_e), pick(next_e)))
    return pos, sorted_tok, items


def _combine_kernel(x1_ref, yg_ref, meta_ref, o_ref):
    acc = x1_ref[...]
    for kk in range(TOP_K):
        gate = meta_ref[:, TOP_K + kk:TOP_K + kk + 1]
        acc = acc + gate * yg_ref[kk].astype(F32)
    o_ref[...] = acc


def _combine(x1, yg, meta):
    rows, d_model = x1.shape
    tile = 2 * TOK_TILE
    assert rows % tile == 0
    tok = lambda w: pl.BlockSpec((tile, w), lambda i: (i, 0))
    return pl.pallas_call(
        _combine_kernel,
        grid=(rows // tile,),
        in_specs=[tok(d_model), pl.BlockSpec((TOP_K, tile, d_model), lambda i: (0, i, 0)), tok(LANES)],
        out_specs=tok(d_model),
        out_shape=jax.ShapeDtypeStruct((rows, d_model), F32),
        compiler_params=pltpu.CompilerParams(dimension_semantics=("arbitrary",), vmem_limit_bytes=VMEM_LIMIT),
        name="combine",
    )(x1, yg, meta)


def kernel(x_prompt, x_sample, norm1, w_in, q_gain, k_gain, rel_bias, sgu_norm, w_s, b_s, out_norm_a, out_norm_b,
           w_o, norm2, w_router, b_router, w_gu, b_gu, w_down, b_down):
    assert norm1.shape[0] == 1, "single-layer trunk"
    bp, sp, d_model = x_prompt.shape
    bs, ss, _ = x_sample.shape
    n_exp = w_router.shape[-1]

    batches = sorted(((x_prompt.reshape(bp * sp, d_model), sp, 0), (x_sample.reshape(bs * ss, d_model), ss, 1)),
                     key=lambda c: -c[0].shape[0])
    staged, counts = [], None
    for x, seq_len, key in batches:
        if counts is not None:
            x, _ = lax.optimization_barrier((x, counts))
        q, k, v, a_n = _inproj(x, norm1[0], w_in[0], q_gain[0], k_gain[0], sgu_norm[0], w_s[0], b_s[0], out_norm_a[0])
        attn = _attention(q, k, v, rel_bias, seq_len)
        x1, h2, meta, counts, route = _outproj(a_n, attn, x, out_norm_b[0], w_o[0], norm2[0], w_router[0],
                                               b_router[0])
        pos, sorted_tok, items = _route(route, counts, n_exp)
        staged.append((key, x1, meta, pos, items, h2.at[sorted_tok].get(mode="promise_in_bounds")))

    outs, order_y, order_out = {}, counts, None
    for key, x1, meta, pos, items, x_rows in staged:
        x_rows, _ = lax.optimization_barrier((x_rows, order_y))
        y_rows = _moe_experts(x_rows, items, w_gu[0], b_gu[0], w_down[0], b_down[0])
        yg = y_rows.at[pos].get(mode="promise_in_bounds").reshape(TOP_K, x1.shape[0], d_model)
        if order_out is not None:
            yg, _ = lax.optimization_barrier((yg, order_out))
        outs[key] = _combine(x1, yg, meta)
        order_y, order_out = y_rows, outs[key]
    return (outs[0].reshape(bp, sp, d_model), outs[1].reshape(bs, ss, d_model))
```

```python
import functools
import math

import numpy as np
import jax
import jax.numpy as jnp
from jax import lax
from jax.experimental import pallas as pl
from jax.experimental.pallas import tpu as pltpu

F32 = jnp.float32
BF16 = jnp.bfloat16

EPS = 1e-6
NEG_INF = -1e30
HEAD_DIM = 64
SGU_CHUNK = 128
WINDOWS = ((128, 1), (512, 4), (2048, 16))
N_BUCKETS = 32
MAX_DISTANCE = 1024
TOP_K = 4
SWIGLU_LIMIT = 7.0
SWIGLU_ALPHA = 1.702

LANES = 128
TOK_TILE = 512
ATT_TILE = 2048
ATT_HALO = 1024
ATT_QB = 128
ATT_SIDE = 64
ATT_PIPE = 8
MOE_ROWS = 512
GATHER_MIN_ROWS = 32768
VMEM_LIMIT = 56 * 1024 * 1024


def _dot(a, b):
    return jnp.dot(a, b, preferred_element_type=F32)


def _dot_nt(a, b):
    return lax.dot_general(a, b, (((1,), (1,)), ((), ())), preferred_element_type=F32)


def _rms(x, g):
    ms = jnp.mean(x * x, axis=-1, keepdims=True)
    return x * lax.rsqrt(ms + EPS) * g


def _gelu(x):
    return 0.5 * x * (1.0 + lax.erf(x * (1.0 / math.sqrt(2.0))))


def _split_bf16(x):
    hi = x.astype(BF16)
    lo = (x - hi.astype(F32)).astype(BF16)
    return hi, lo


def _inproj_kernel(attn_w, sgu_w, x_ref, norm1_ref, win_ref, gsum_ref, gexp_ref,
                   qkg_ref, sgun_ref, wsp_ref, bsb_ref, ona_ref, q_ref, k_ref, v_ref, a_ref):
    h = _rms(x_ref[...], norm1_ref[...]).astype(BF16)

    zqk = _dot(h, win_ref[:, 0:2 * attn_w])
    ss = _dot((zqk * zqk).astype(BF16), gsum_ref[...])
    inv = lax.rsqrt(ss * (1.0 / HEAD_DIM) + EPS)
    inv_hi, inv_lo = _split_bf16(inv)
    invb = _dot(inv_hi, gexp_ref[...]) + _dot(inv_lo, gexp_ref[...])
    qk = zqk * invb * qkg_ref[...]
    q_ref[...] = qk[:, 0:attn_w]
    k_ref[...] = qk[:, attn_w:2 * attn_w]
    v_ref[...] = _dot(h, win_ref[:, 2 * attn_w:3 * attn_w])

    c0 = 3 * attn_w
    u = _gelu(_dot(h, win_ref[:, c0:c0 + sgu_w]))
    gv = _gelu(_dot(h, win_ref[:, c0 + sgu_w:c0 + 2 * sgu_w]))
    vsn = _rms(gv, sgun_ref[...]).astype(BF16)
    lane = lax.broadcasted_iota(jnp.int32, (SGU_CHUNK, LANES), 1)
    lo_half = lane < HEAD_DIM
    zero = jnp.zeros((SGU_CHUNK, LANES), BF16)
    for c in range(TOK_TILE // SGU_CHUNK):
        r0 = c * SGU_CHUNK
        parts = []
        for j in range(sgu_w // LANES):
            blk = vsn[r0:r0 + SGU_CHUNK, j * LANES:(j + 1) * LANES]
            rhs = jnp.concatenate([jnp.where(lo_half, blk, zero), jnp.where(lo_half, zero, blk)], axis=0)
            parts.append(_dot(wsp_ref[j], rhs))
        s = jnp.concatenate(parts, axis=1) + bsb_ref[...]
        a = u[r0:r0 + SGU_CHUNK, :] * s
        a_ref[r0:r0 + SGU_CHUNK, :] = _rms(a, ona_ref[...]).astype(BF16)


def _inproj(x, norm1, w_in, q_gain, k_gain, sgu_norm, w_s, b_s, out_norm_a):
    t, d_model = x.shape
    n_heads_w = q_gain.shape[0]
    sgu_w = sgu_norm.shape[0]
    attn_w = (w_in.shape[1] - 2 * sgu_w) // 3
    n_heads = attn_w // n_heads_w
    n_groups = w_s.shape[0]
    assert n_heads_w == HEAD_DIM and sgu_w // n_groups == HEAD_DIM and w_s.shape[1] == SGU_CHUNK
    assert t % TOK_TILE == 0 and 2 * n_heads <= LANES

    heads = np.arange(2 * attn_w) // HEAD_DIM
    gsum = (heads[:, None] == np.arange(LANES)[None, :]).astype(np.float32)
    gexp = gsum.T
    qkg = jnp.concatenate([jnp.tile(q_gain, n_heads) * (HEAD_DIM ** -0.5), jnp.tile(k_gain, n_heads)])
    wsp = jnp.concatenate([w_s[0::2], w_s[1::2]], axis=2).astype(BF16)
    bsb = jnp.repeat(b_s.T, HEAD_DIM, axis=1)

    const = lambda shape: pl.BlockSpec(shape, lambda i: (0,) * len(shape))
    tok = lambda w: pl.BlockSpec((TOK_TILE, w), lambda i: (i, 0))
    return pl.pallas_call(
        functools.partial(_inproj_kernel, attn_w, sgu_w),
        grid=(t // TOK_TILE,),
        in_specs=[
            tok(d_model), const((1, d_model)), const(w_in.shape), const((2 * attn_w, LANES)), const((LANES, 2 * attn_w)),
            const((1, 2 * attn_w)), const((1, sgu_w)), const(wsp.shape), const((SGU_CHUNK, sgu_w)),
            const((1, sgu_w)),
        ],
        out_specs=[tok(attn_w), tok(attn_w), tok(attn_w), tok(sgu_w)],
        out_shape=[jax.ShapeDtypeStruct((t, attn_w), F32)] * 3 + [jax.ShapeDtypeStruct((t, sgu_w), BF16)],
        compiler_params=pltpu.CompilerParams(dimension_semantics=("arbitrary",), vmem_limit_bytes=VMEM_LIMIT),
        name="inproj",
    )(x, norm1[None], w_in.astype(BF16), jnp.asarray(gsum, BF16), jnp.asarray(gexp, BF16),
      qkg[None], sgu_norm[None], wsp, bsb, out_norm_a[None])


def _t5_bucket(rel):
    nb = N_BUCKETS // 2
    bucket = (rel > 0).astype(np.int32) * nb
    n = np.abs(rel)
    max_exact = nb // 2
    large = max_exact + (np.log(np.maximum(n, 1) / max_exact)
                         / np.log(MAX_DISTANCE / max_exact) * (nb - max_exact)).astype(np.int32)
    large = np.minimum(large, nb - 1)
    return (bucket + np.where(n < max_exact, n, large)).astype(np.int32)


def _branch_layout():
    out, kbase = [], 0
    for bi, (window, dil) in enumerate(WINDOWS):
        assert window // (2 * dil) == ATT_SIDE
        m = ATT_TILE // dil
        seg = m + 2 * ATT_SIDE
        out.append((dil, m, seg, kbase, bi * ATT_TILE))
        kbase += dil * seg
    return out, kbase


def _strided(ref, start, size, stride):
    if stride == 1:
        return ref[start:start + size, :]
    return ref[pl.ds(start, size, stride=stride), :]


def _attn_kernel(per_seq, q_ref, km_ref, kp_ref, kn_ref, vm_ref, vp_ref, vn_ref,
                 bias_ref, o_ref, qs, kts, vs, o_scr, d_scr, l_scr, s_scr, p_scr, m_scr, k4, v4, q4):
    i = pl.program_id(0)
    w = i % per_seq
    left_ok = w > 0
    right_ok = w < per_seq - 1
    layout, _ = _branch_layout()
    mid = WINDOWS[1][1]
    assert WINDOWS[2][1] == mid * mid
    halo4, main4 = ATT_HALO // mid, ATT_TILE // mid
    win4 = 2 * halo4 + main4

    @pl.when(jnp.logical_and(i == 0, pl.program_id(1) == 0))
    def _():
        vs[:, LANES:] = jnp.ones((vs.shape[0], LANES), BF16)

    for r in range(mid):
        for dst, main, prev, nxt in ((k4, km_ref, kp_ref, kn_ref), (v4, vm_ref, vp_ref, vn_ref)):
            dst[r * win4:r * win4 + halo4, :] = _strided(prev, r, halo4, mid)
            dst[r * win4 + halo4:r * win4 + halo4 + main4, :] = _strided(main, r, main4, mid)
            dst[r * win4 + halo4 + main4:(r + 1) * win4, :] = _strided(nxt, r, halo4, mid)
        q4[r * main4:(r + 1) * main4, :] = _strided(q_ref, r, main4, mid)

    def key_rows(src_k, src_v, start, stride, kbase_rows):
        kc = _strided(src_k, start, ATT_QB, stride)
        kts[kbase_rows // ATT_QB] = kc.T.astype(BF16)
        vs[kbase_rows:kbase_rows + ATT_QB, 0:LANES] = _strided(src_v, start, ATT_QB, stride).astype(BF16)

    for bi, (dil, m, seg, kbase, qbase) in enumerate(layout):
        nchunk = seg // ATT_QB
        for r in range(dil):
            o = kbase + r * seg
            if dil == 1:
                for c in range(nchunk):
                    lo = c * ATT_QB - ATT_SIDE
                    if c == 0:
                        kc = jnp.concatenate([kp_ref[ATT_HALO - ATT_SIDE:, :], km_ref[0:ATT_QB - ATT_SIDE, :]], axis=0)
                        vc = jnp.concatenate([vp_ref[ATT_HALO - ATT_SIDE:, :], vm_ref[0:ATT_QB - ATT_SIDE, :]], axis=0)
                    elif c == nchunk - 1:
                        kc = jnp.concatenate([km_ref[lo:, :], kn_ref[0:ATT_SIDE, :]], axis=0)
                        vc = jnp.concatenate([vm_ref[lo:, :], vn_ref[0:ATT_SIDE, :]], axis=0)
                    else:
                        kc = km_ref[lo:lo + ATT_QB, :]
                        vc = vm_ref[lo:lo + ATT_QB, :]
                    kts[(o + c * ATT_QB) // ATT_QB] = kc.T.astype(BF16)
                    vs[o + c * ATT_QB:o + (c + 1) * ATT_QB, 0:LANES] = vc.astype(BF16)
                qs[qbase:qbase + m, :] = q_ref[...].astype(BF16)
            else:
                r4, s = r % mid, r // mid
                step = dil // mid
                first = r4 * win4 + (halo4 - ATT_SIDE * step) + s
                for c in range(nchunk):
                    key_rows(k4, v4, first + c * ATT_QB * step, step, o + c * ATT_QB)
                qs[qbase + r * m:qbase + (r + 1) * m, :] = _strided(q4, r4 * main4 + s, m, step).astype(BF16)

    lane = lax.broadcasted_iota(jnp.int32, (ATT_QB, LANES), 1)
    head0 = lane < HEAD_DIM
    cells = ATT_TILE // ATT_QB
    total = len(layout) * cells
    blk_bits = [(m // ATT_QB).bit_length() - 1 for _, m, _, _, _ in layout]
    seg_chunks = [seg // ATT_QB for _, _, seg, _, _ in layout]
    base_chunks = [kbase // ATT_QB for _, _, _, kbase, _ in layout]

    def cell_params(g):
        bi = g // cells
        idx = g % cells
        if isinstance(g, int):
            pick = lambda vals: vals[bi]
        else:
            pick = lambda vals: jnp.where(bi == 0, vals[0], jnp.where(bi == 1, vals[1], vals[2]))
        bits = pick(blk_bits)
        last_blk = (1 << bits) - 1
        blk = idx & last_blk
        return bi, blk, last_blk, pick(base_chunks) + (idx >> bits) * pick(seg_chunks) + blk

    def stage_logits(g, slot):
        bi, blk, last_blk, kchunk = cell_params(g)
        left_bad = jnp.logical_and(blk == 0, jnp.logical_not(left_ok))
        right_bad = jnp.logical_and(blk == last_blk, jnp.logical_not(right_ok))
        variant = left_bad.astype(jnp.int32) + 2 * right_bad.astype(jnp.int32)
        qc = qs[pl.ds(pl.multiple_of(g * ATT_QB, ATT_QB), ATT_QB), :]
        zero = jnp.zeros_like(qc)
        lhs = jnp.concatenate([jnp.where(head0, qc, zero), jnp.where(head0, zero, qc)], axis=0)
        kt = jnp.concatenate([kts[kchunk], kts[kchunk + 1]], axis=1)
        s_scr[slot] = _dot(lhs, kt) + bias_ref[bi, variant]

    def stage_softmax(slot):
        part = ATT_QB // 2
        for c in range(2 * ATT_QB // part):
            s = s_scr[slot, c * part:(c + 1) * part, :]
            mx = jnp.max(s, axis=-1, keepdims=True)
            p_scr[slot, c * part:(c + 1) * part, :] = jnp.exp(s - mx).astype(BF16)
            m_scr[slot, c * part:(c + 1) * part, :] = jnp.broadcast_to(mx, (part, LANES))

    def stage_values(g, slot):
        _, _, _, kchunk = cell_params(g)
        koff = pl.multiple_of(kchunk * ATT_QB, ATT_QB)
        r = _dot(p_scr[slot], vs[pl.ds(koff, 2 * ATT_QB), :])
        num = jnp.where(head0, r[0:ATT_QB, 0:LANES], r[ATT_QB:, 0:LANES])
        den = jnp.where(head0, r[0:ATT_QB, LANES:], r[ATT_QB:, LANES:])
        mx = jnp.where(head0, m_scr[slot, 0:ATT_QB, :], m_scr[slot, ATT_QB:, :])
        rows = pl.ds(pl.multiple_of(g * ATT_QB, ATT_QB), ATT_QB)
        o_scr[rows, :] = num
        d_scr[rows, :] = den
        l_scr[rows, :] = mx

    width = ATT_PIPE

    def group(c, do_values, do_softmax, do_logits):
        if do_values:
            for u in range(width):
                stage_values(c - width + u, u)
        if do_softmax:
            for u in range(width):
                stage_softmax(u)
        if do_logits:
            for u in range(width):
                stage_logits(c + width + u, u)

    group(-width, False, False, True)
    group(0, False, True, True)

    def trip(t, carry):
        group((t + 1) * width, True, True, True)
        return carry

    lax.fori_loop(0, total // width - 2, trip, 0)
    group(total - width, True, True, False)
    group(total, True, False, False)

    big = WINDOWS[-1][1]
    rows = ATT_TILE // big
    dil0, m0, _, _, qb0 = layout[0]
    assert dil0 == 1 and big == mid * mid
    for r4 in range(mid):
        k4[r4 * main4:(r4 + 1) * main4, :] = _strided(o_scr, qb0 + r4, main4, mid)
        k4[m0 + r4 * main4:m0 + (r4 + 1) * main4, :] = _strided(d_scr, qb0 + r4, main4, mid)
        v4[r4 * main4:(r4 + 1) * main4, :] = _strided(l_scr, qb0 + r4, main4, mid)
    _, m1, _, _, qb1 = layout[1]
    _, m2, _, _, qb2 = layout[2]
    for r in range(big):
        r4, s = r % mid, r // mid
        first = r4 * main4 + s
        nums = [_strided(k4, first, rows, mid), _strided(o_scr, qb1 + r4 * m1 + s, rows, mid),
                o_scr[qb2 + r * m2:qb2 + (r + 1) * m2, :]]
        dens = [_strided(k4, m0 + first, rows, mid), _strided(d_scr, qb1 + r4 * m1 + s, rows, mid),
                d_scr[qb2 + r * m2:qb2 + (r + 1) * m2, :]]
        mxs = [_strided(v4, first, rows, mid), _strided(l_scr, qb1 + r4 * m1 + s, rows, mid),
               l_scr[qb2 + r * m2:qb2 + (r + 1) * m2, :]]
        top = jnp.maximum(jnp.maximum(mxs[0], mxs[1]), mxs[2])
        ws = [jnp.exp(mx - top) for mx in mxs]
        num = ws[0] * nums[0] + ws[1] * nums[1] + ws[2] * nums[2]
        den = ws[0] * dens[0] + ws[1] * dens[1] + ws[2] * dens[2]
        q4[pl.ds(first, rows, stride=mid), :] = num / den
    for r4 in range(mid):
        o_ref[pl.ds(r4, main4, stride=mid), :] = q4[r4 * main4:(r4 + 1) * main4, :]


def _attention(q, k, v, rel_bias, seq_len):
    t, attn_w = q.shape
    n_heads = attn_w // HEAD_DIM
    pair = LANES // HEAD_DIM
    assert seq_len % ATT_TILE == 0 and t % seq_len == 0 and ATT_TILE == 2 * ATT_HALO
    assert ATT_TILE // WINDOWS[-1][1] == ATT_QB and pair == 2
    layout, krows = _branch_layout()

    nb = len(WINDOWS)
    span = 2 * ATT_SIDE + 1
    jj = np.arange(2 * ATT_QB)[None, :]
    offsets = np.arange(span) - ATT_SIDE
    table_bucket = np.stack([_t5_bucket(offsets * dil) for _, dil in WINDOWS])
    table = jnp.zeros((n_heads, nb, span), F32)
    for b in range(N_BUCKETS):
        table = jnp.where(jnp.asarray(table_bucket == b)[None], rel_bias[b].astype(F32)[:, None, None], table)
    row_len = 3 * ATT_QB
    period = row_len + 1
    padded = jnp.concatenate([table, jnp.full((n_heads, nb, period - span), NEG_INF, F32)], axis=-1)
    flat = jnp.tile(padded, (1, 1, ATT_QB + 1))[..., :ATT_QB * row_len]
    bias = flat.reshape(n_heads, nb, ATT_QB, row_len)[..., :2 * ATT_QB]
    bias = bias.reshape(n_heads // pair, pair, nb, ATT_QB, 2 * ATT_QB).transpose(0, 2, 1, 3, 4)
    bias = bias.reshape(n_heads // pair, nb, 1, pair * ATT_QB, 2 * ATT_QB)
    left = jnp.asarray((jj < ATT_SIDE)[None, None, None] & (np.arange(4) % 2 == 1)[None, None, :, None, None])
    right = jnp.asarray((jj >= 2 * ATT_QB - ATT_SIDE)[None, None, None]
                        & (np.arange(4) // 2 == 1)[None, None, :, None, None])
    bias = jnp.where(left | right, NEG_INF, bias)

    halo_blocks = t // ATT_HALO
    per_tile = ATT_TILE // ATT_HALO
    mid = WINDOWS[1][1]
    main = pl.BlockSpec((ATT_TILE, LANES), lambda i, j: (i, j))
    prev = pl.BlockSpec((ATT_HALO, LANES), lambda i, j: (jnp.maximum(i * per_tile - 1, 0), j))
    nxt = pl.BlockSpec((ATT_HALO, LANES), lambda i, j: (jnp.minimum((i + 1) * per_tile, halo_blocks - 1), j))
    n_q = nb * ATT_TILE
    return pl.pallas_call(
        functools.partial(_attn_kernel, seq_len // ATT_TILE),
        grid=(t // ATT_TILE, attn_w // LANES),
        in_specs=[main, main, prev, nxt, main, prev, nxt,
                  pl.BlockSpec((None, nb, 4, pair * ATT_QB, 2 * ATT_QB), lambda i, j: (j, 0, 0, 0, 0))],
        out_specs=main,
        out_shape=jax.ShapeDtypeStruct((t, attn_w), F32),
        scratch_shapes=[pltpu.VMEM((n_q, LANES), BF16),
                        pltpu.VMEM((krows // ATT_QB, LANES, ATT_QB), BF16),
                        pltpu.VMEM((krows, 2 * LANES), BF16),
                        pltpu.VMEM((n_q, LANES), F32), pltpu.VMEM((n_q, LANES), F32),
                        pltpu.VMEM((n_q, LANES), F32),
                        pltpu.VMEM((ATT_PIPE, pair * ATT_QB, 2 * ATT_QB), F32),
                        pltpu.VMEM((ATT_PIPE, pair * ATT_QB, 2 * ATT_QB), BF16),
                        pltpu.VMEM((ATT_PIPE, pair * ATT_QB, LANES), F32),
                        pltpu.VMEM(((ATT_TILE + 2 * ATT_HALO), LANES), F32),
                        pltpu.VMEM(((ATT_TILE + 2 * ATT_HALO), LANES), F32),
                        pltpu.VMEM((ATT_TILE, LANES), F32)],
        compiler_params=pltpu.CompilerParams(dimension_semantics=("arbitrary", "arbitrary"),
                                             vmem_limit_bytes=VMEM_LIMIT),
        name="attn",
    )(q, k, k, k, v, v, v, bias)


def _outproj_kernel(n_tiles, sgu_w, n_exp, *refs):
    h2_ref = refs[10]
    i = pl.program_id(0)

    @pl.when(i < n_tiles)
    def _():
        _outproj_tile(sgu_w, n_exp, *refs)

    @pl.when(i >= n_tiles)
    def _():
        h2_ref[...] = jnp.zeros_like(h2_ref)


def _outproj_tile(sgu_w, n_exp, a_ref, attn_ref, x_ref, onb_ref, wo_ref, norm2_ref, wr_ref,
                  br_ref, tri_ref, x1_ref, h2_ref, meta_ref, cnt_ref, route_ref, carry):
    i = pl.program_id(0)

    @pl.when(i == 0)
    def _():
        carry[...] = jnp.zeros_like(carry)

    bn = _rms(attn_ref[...], onb_ref[...]).astype(BF16)
    x1 = x_ref[...] + _dot(a_ref[...], wo_ref[0:sgu_w, :]) + _dot(bn, wo_ref[sgu_w:, :])
    x1_ref[...] = x1
    h2 = _rms(x1, norm2_ref[...])
    h2_bf = h2.astype(BF16)
    h2_ref[...] = h2_bf
    logits = _dot(h2_bf, wr_ref[...]) + br_ref[...]

    lane = lax.broadcasted_iota(jnp.int32, logits.shape, 1)
    work = jnp.where(lane < n_exp, logits, -jnp.inf)
    chosen = jnp.zeros(logits.shape, jnp.bool_)
    experts, values = [], []
    for _ in range(TOP_K):
        top = jnp.max(work, axis=-1, keepdims=True)
        idx = jnp.min(jnp.where(work == top, lane, LANES), axis=-1, keepdims=True)
        hit = lane == idx
        chosen = jnp.logical_or(chosen, hit)
        work = jnp.where(hit, -jnp.inf, work)
        experts.append(idx)
        values.append(top)
    exps = [jnp.exp(v - values[0]) for v in values]
    den = exps[0] + exps[1] + exps[2] + exps[3]

    onehot = jnp.where(chosen, 1.0, 0.0)
    before = _dot(tri_ref[...], onehot.astype(BF16)) + carry[0:1, :]
    carry[0:1, :] = carry[0:1, :] + jnp.sum(onehot, axis=0, keepdims=True)
    cnt_ref[...] = carry[...]

    meta = jnp.zeros(logits.shape, F32)
    for kk in range(TOP_K):
        rank = jnp.sum(jnp.where(lane == experts[kk], before, 0.0), axis=-1, keepdims=True)
        meta = jnp.where(lane == kk, experts[kk].astype(F32), meta)
        meta = jnp.where(lane == TOP_K + kk, exps[kk] / den, meta)
        meta = jnp.where(lane == 2 * TOP_K + kk, rank, meta)
    meta_ref[...] = meta
    route_ref[...] = meta.T[0:route_ref.shape[0], :]


def _outproj(a_n, attn, x, out_norm_b, w_o, norm2, w_router, b_router):
    sgu_w = a_n.shape[1]
    attn_w = attn.shape[1]
    t, d_model = x.shape
    n_exp = w_router.shape[1]
    assert n_exp <= LANES and t % TOK_TILE == 0
    wr = jnp.pad(w_router, ((0, 0), (0, LANES - n_exp))).astype(BF16)
    br =jnp.pad(b_router, (0, LANES - n_exp))[None]
    tri = np.tril(np.ones((TOK_TILE, TOK_TILE), np.float32), -1)
    n_tiles = t // TOK_TILE
    h2_rows = max(t, GATHER_MIN_ROWS)
    assert h2_rows % TOK_TILE == 0
    last = n_tiles - 1

    const = lambda shape: pl.BlockSpec(shape, lambda i: (0,) * len(shape))
    tok = lambda w: pl.BlockSpec((TOK_TILE, w), lambda i: (jnp.minimum(i, last), 0))
    return pl.pallas_call(
        functools.partial(_outproj_kernel, n_tiles, sgu_w, n_exp),
        grid=(h2_rows // TOK_TILE,),
        in_specs=[
            tok(sgu_w), tok(attn_w), tok(d_model),
            const((1, attn_w)), const(w_o.shape), const((1, d_model)), const((d_model, LANES)),
            const((1, LANES)), const((TOK_TILE, TOK_TILE)),
        ],
        out_specs=[tok(d_model), pl.BlockSpec((TOK_TILE, d_model), lambda i: (i, 0)), tok(LANES), const((8, LANES)),
                   pl.BlockSpec((4 * TOP_K, TOK_TILE), lambda i: (0, jnp.minimum(i, last)))],
        out_shape=[jax.ShapeDtypeStruct((t, d_model), F32), jax.ShapeDtypeStruct((h2_rows, d_model), BF16),
                   jax.ShapeDtypeStruct((t, LANES), F32), jax.ShapeDtypeStruct((8, LANES), F32),
                   jax.ShapeDtypeStruct((4 * TOP_K, t), F32)],
        scratch_shapes=[pltpu.VMEM((8, LANES), F32)],
        compiler_params=pltpu.CompilerParams(dimension_semantics=("arbitrary",), vmem_limit_bytes=VMEM_LIMIT),
        name="outproj",
    )(a_n, attn, x, out_norm_b[None], w_o.astype(BF16), norm2[None], wr, br, jnp.asarray(tri, BF16))


def _moe_kernel(d_exp, blk_ref, exp_ref, lo_ref, hi_ref, slot_ref, nxt_ref, x_ref, wgu_hbm, bgu_ref, wd_hbm, bd_ref,
                y_ref, wgu_f, wd_f, wgu_s, wd_s, act_s, sem):
    it = pl.program_id(0)
    prev = jnp.maximum(it - 1, 0)
    new_expert = jnp.logical_or(it == 0, exp_ref[it] != exp_ref[prev])
    new_block = jnp.logical_or(it == 0, blk_ref[it] != blk_ref[prev])
    lo, hi = lo_ref[it], hi_ref[it]
    active = hi > lo

    def weight_copies(e, slot):
        return (pltpu.make_async_copy(wgu_hbm.at[e], wgu_f.at[slot], sem.at[0, slot]),
                pltpu.make_async_copy(wd_hbm.at[e], wd_f.at[slot], sem.at[1, slot]))

    @pl.when(it == 0)
    def _():
        for c in weight_copies(exp_ref[0], slot_ref[0]):
            c.start()

    @pl.when(jnp.logical_and(new_expert, active))
    def _():
        slot = slot_ref[it]
        for c in weight_copies(exp_ref[it], slot):
            c.wait()

        @pl.when(nxt_ref[it] >= 0)
        def _():
            for c in weight_copies(nxt_ref[it], 1 - slot):
                c.start()

        wgu_s[...] = wgu_f[slot].astype(BF16)
        wd_s[...] = wd_f[slot].astype(BF16)

    def expert_rows(off, m):
        rows = pl.ds(off, m)
        x = x_ref[rows, :]
        step = 512
        for n in range(0, d_exp, step):
            gate = _dot(x, wgu_s[:, n:n + step]) + bgu_ref[:, n:n + step]
            up = _dot(x, wgu_s[:, d_exp + n:d_exp + n + step]) + bgu_ref[:, d_exp + n:d_exp + n + step]
            gate = jnp.minimum(gate, SWIGLU_LIMIT)
            up = jnp.clip(up, -SWIGLU_LIMIT, SWIGLU_LIMIT)
            glu = gate * jax.nn.sigmoid(SWIGLU_ALPHA * gate)
            act_s[0:m, n:n + step] = ((up + 1.0) * glu).astype(BF16)
        y = (_dot(act_s[0:m, :], wd_s[...]) + bd_ref[...]).astype(y_ref.dtype)
        row = off + lax.broadcasted_iota(jnp.int32, (m, 1), 0)
        mine = jnp.logical_and(row >= lo, row < hi)

        @pl.when(new_block)
        def _():
            y_ref[rows, :] = jnp.where(mine, y, jnp.zeros_like(y))

        @pl.when(jnp.logical_not(new_block))
        def _():
            y_ref[rows, :] = jnp.where(mine, y, y_ref[rows, :])

    half = MOE_ROWS // 2
    in_upper = lo >= half
    one_half = jnp.logical_and(active, jnp.logical_or(hi <= half, in_upper))

    @pl.when(jnp.logical_and(active, jnp.logical_not(one_half)))
    def _():
        expert_rows(0, MOE_ROWS)

    @pl.when(one_half)
    def _():
        off = pl.multiple_of(jnp.where(in_upper, half, 0), half)
        expert_rows(off, half)

        @pl.when(new_block)
        def _():
            y_ref[pl.ds(pl.multiple_of(half - off, half), half), :] = jnp.zeros((half, y_ref.shape[1]), y_ref.dtype)


def _moe_experts(x_rows, items, w_gu, b_gu, w_down, b_down):
    n_rows, d_model = x_rows.shape
    n_exp, _, two_de = w_gu.shape
    d_exp = two_de // 2
    n_items = items[0].shape[0]
    grid_spec = pltpu.PrefetchScalarGridSpec(
        num_scalar_prefetch=len(items),
        grid=(n_items,),
        in_specs=[
            pl.BlockSpec((MOE_ROWS, d_model), lambda i, blk, ex, *_: (blk[i], 0)),
            pl.BlockSpec(memory_space=pl.ANY),
            pl.BlockSpec((None, 1, two_de), lambda i, blk, ex, *_: (ex[i], 0, 0)),
            pl.BlockSpec(memory_space=pl.ANY),
            pl.BlockSpec((None, 1, d_model), lambda i, blk, ex, *_: (ex[i], 0, 0)),
        ],
        out_specs=pl.BlockSpec((MOE_ROWS, d_model), lambda i, blk, ex, *_: (blk[i], 0)),
        scratch_shapes=[pltpu.VMEM((2, d_model, two_de), F32), pltpu.VMEM((2, d_exp, d_model), F32),
                        pltpu.VMEM((d_model, two_de), BF16), pltpu.VMEM((d_exp, d_model), BF16),
                        pltpu.VMEM((MOE_ROWS, d_exp), BF16), pltpu.SemaphoreType.DMA((2, 2))],
    )
    return pl.pallas_call(
        functools.partial(_moe_kernel, d_exp),
        grid_spec=grid_spec,
        out_shape=jax.ShapeDtypeStruct((n_rows, d_model), BF16),
        compiler_params=pltpu.CompilerParams(dimension_semantics=("arbitrary",), vmem_limit_bytes=VMEM_LIMIT),
        name="moe",
    )(*items, x_rows, w_gu, b_gu[:, None, :], w_down, b_down[:, None, :])


def _route(meta, counts_f, n_exp):
    t = meta.shape[1]
    n_assign = t * TOP_K
    assert n_assign % MOE_ROWS == 0
    top_e = meta[0:TOP_K].astype(jnp.int32)
    rank = meta[2 * TOP_K:3 * TOP_K].astype(jnp.int32)
    counts = counts_f[0, :n_exp].astype(jnp.int32)
    end = jnp.cumsum(counts)
    start = end - counts
    experts = jnp.arange(n_exp, dtype=jnp.int32)
    start_of = jnp.zeros_like(top_e)
    for e in range(n_exp):
        start_of = jnp.where(top_e == e, start[e], start_of)
    pos = (start_of + rank).reshape(-1)

    shift = (n_assign - 1).bit_length()
    assert (n_exp << shift) < 2 ** 31
    assign = jnp.arange(t, dtype=jnp.int32)[None, :] * TOP_K + jnp.arange(TOP_K, dtype=jnp.int32)[:, None]
    keys = ((top_e << shift) + assign).reshape(-1)
    sorted_tok = (jnp.sort(keys) & ((1 << shift) - 1)) // TOP_K

    n_blocks = n_assign // MOE_ROWS
    n_items = n_blocks + n_exp
    first_blk = start // MOE_ROWS
    n_it = jnp.where(counts > 0, (end - 1) // MOE_ROWS - first_blk + 1, 0)
    it_end = jnp.cumsum(n_it)
    it_start = it_end - n_it
    i = jnp.arange(n_items, dtype=jnp.int32)
    valid = i < it_end[-1]
    e_i = jnp.sum(jnp.minimum(i, it_end[-1] - 1)[:, None] >= it_end[None, :], axis=1).astype(jnp.int32)
    e_i = jnp.minimum(e_i, n_exp - 1)
    pick = lambda table: jnp.sum(jnp.where(e_i[:, None] == experts[None, :], table[None, :], 0), axis=1)
    blk_i = jnp.where(valid, pick(first_blk) + i - pick(it_start), n_blocks - 1)
    lo_i = jnp.where(valid, jnp.clip(pick(start) - blk_i * MOE_ROWS, 0, MOE_ROWS), 0)
    hi_i = jnp.where(valid, jnp.clip(pick(end) - blk_i * MOE_ROWS, 0, MOE_ROWS), 0)
    has = counts > 0
    slot_e = (jnp.cumsum(has.astype(jnp.int32)) - 1) % 2
    later = jnp.where(jnp.logical_and(has[None, :], experts[None, :] > experts[:, None]), experts[None, :], n_exp)
    next_e = jnp.min(later, axis=1)
    next_e = jnp.where(next_e == n_exp, -1, next_e)
    items = tuple(a.astype(jnp.int32) for a in (blk_i, e_i, lo_i, hi_i, pick(slot_e), pick(next_e)))
    return pos, sorted_tok, items


def _combine_kernel(x1_ref, yg_ref, meta_ref, o_ref):
    acc = x1_ref[...]
    for kk in range(TOP_K):
        gate = meta_ref[:, TOP_K + kk:TOP_K + kk + 1]
        acc = acc + gate * yg_ref[kk].astype(F32)
    o_ref[...] = acc


def _combine(x1, yg, meta):
    rows, d_model = x1.shape
    tile = 2 * TOK_TILE
    assert rows % tile == 0
    tok = lambda w: pl.BlockSpec((tile, w), lambda i: (i, 0))
    return pl.pallas_call(
        _combine_kernel,
        grid=(rows // tile,),
        in_specs=[tok(d_model), pl.BlockSpec((TOP_K, tile, d_model), lambda i: (0, i, 0)), tok(LANES)],
        out_specs=tok(d_model),
        out_shape=jax.ShapeDtypeStruct((rows, d_model), F32),
        compiler_params=pltpu.CompilerParams(dimension_semantics=("arbitrary",), vmem_limit_bytes=VMEM_LIMIT),
        name="combine",
    )(x1, yg, meta)


def kernel(x_prompt, x_sample, norm1, w_in, q_gain, k_gain, rel_bias, sgu_norm, w_s, b_s, out_norm_a, out_norm_b,
           w_o, norm2, w_router, b_router, w_gu, b_gu, w_down, b_down):
    assert norm1.shape[0] == 1, "single-layer trunk"
    bp, sp, d_model = x_prompt.shape
    bs, ss, _ = x_sample.shape
    n_exp = w_router.shape[-1]

    batches = sorted(((x_prompt.reshape(bp * sp, d_model), sp, 0), (x_sample.reshape(bs * ss, d_model), ss, 1)),
                     key=lambda c: -c[0].shape[0])
    staged, counts = [], None
    for x, seq_len, key in batches:
        if counts is not None:
            x, _ = lax.optimization_barrier((x, counts))
        q, k, v, a_n = _inproj(x, norm1[0], w_in[0], q_gain[0], k_gain[0], sgu_norm[0], w_s[0], b_s[0], out_norm_a[0])
        attn = _attention(q, k, v, rel_bias, seq_len)
        x1, h2, meta, counts, route = _outproj(a_n, attn, x, out_norm_b[0], w_o[0], norm2[0], w_router[0],
                                               b_router[0])
        pos, sorted_tok, items = _route(route, counts, n_exp)
        staged.append((key, x1, meta, pos, items, h2.at[sorted_tok].get(mode="promise_in_bounds")))

    outs, order_y, order_out = {}, counts, None
    for key, x1, meta, pos, items, x_rows in staged:
        x_rows, _ = lax.optimization_barrier((x_rows, order_y))
        y_rows = _moe_experts(x_rows, items, w_gu[0], b_gu[0], w_down[0], b_down[0])
        yg = y_rows.at[pos].get(mode="promise_in_bounds").reshape(TOP_K, x1.shape[0], d_model)
        if order_out is not None:
            yg, _ = lax.optimization_barrier((yg, order_out))
        outs[key] = _combine(x1, yg, meta)
        order_y, order_out = y_rows, outs[key]
    return (outs[0].reshape(bp, sp, d_model), outs[1].reshape(bs, ss, d_model))
```

```python
import functools
import math

import numpy as np
import jax
import jax.numpy as jnp
from jax import lax
from jax.experimental import pallas as pl
from jax.experimental.pallas import tpu as pltpu

F32 = jnp.float32
BF16 = jnp.bfloat16

EPS = 1e-6
NEG_INF = -1e30
HEAD_DIM = 64
SGU_CHUNK = 128
WINDOWS = ((128, 1), (512, 4), (2048, 16))
N_BUCKETS = 32
MAX_DISTANCE = 1024
TOP_K = 4
SWIGLU_LIMIT = 7.0
SWIGLU_ALPHA = 1.702

LANES = 128
TOK_TILE = 512
ATT_TILE = 2048
ATT_HALO = 1024
ATT_QB = 128
ATT_SIDE = 64
ATT_PIPE = 8
MOE_ROWS = 512
GATHER_MIN_ROWS = 32768
VMEM_LIMIT = 56 * 1024 * 1024


def _dot(a, b):
    return jnp.dot(a, b, preferred_element_type=F32)


def _rms(x, g):
    ms = jnp.mean(x * x, axis=-1, keepdims=True)
    return x * lax.rsqrt(ms + EPS) * g


def _gelu(x):
    return 0.5 * x * (1.0 + lax.erf(x * (1.0 / math.sqrt(2.0))))


def _split_bf16(x):
    hi = x.astype(BF16)
    lo = (x - hi.astype(F32)).astype(BF16)
    return hi, lo


def _inproj_kernel(attn_w, sgu_w, x_ref, norm1_ref, win_ref, gsum_ref, gexp_ref,
                   qkg_ref, sgun_ref, wsp_ref, bsb_ref, ona_ref, q_ref, k_ref, v_ref, a_ref):
    h = _rms(x_ref[...], norm1_ref[...]).astype(BF16)

    zqk = _dot(h, win_ref[:, 0:2 * attn_w])
    ss = _dot((zqk * zqk).astype(BF16), gsum_ref[...])
    inv = lax.rsqrt(ss * (1.0 / HEAD_DIM) + EPS)
    inv_hi, inv_lo = _split_bf16(inv)
    invb = _dot(inv_hi, gexp_ref[...]) + _dot(inv_lo, gexp_ref[...])
    qk = zqk * invb * qkg_ref[...]
    q_ref[...] = qk[:, 0:attn_w]
    k_ref[...] = qk[:, attn_w:2 * attn_w]
    v_ref[...] = _dot(h, win_ref[:, 2 * attn_w:3 * attn_w])

    c0 = 3 * attn_w
    u = _gelu(_dot(h, win_ref[:, c0:c0 + sgu_w]))
    gv = _gelu(_dot(h, win_ref[:, c0 + sgu_w:c0 + 2 * sgu_w]))
    vsn = _rms(gv, sgun_ref[...]).astype(BF16)
    lane = lax.broadcasted_iota(jnp.int32, (SGU_CHUNK, LANES), 1)
    lo_half = lane < HEAD_DIM
    zero = jnp.zeros((SGU_CHUNK, LANES), BF16)
    for c in range(TOK_TILE // SGU_CHUNK):
        r0 = c * SGU_CHUNK
        parts = []
        for j in range(sgu_w // LANES):
            blk = vsn[r0:r0 + SGU_CHUNK, j * LANES:(j + 1) * LANES]
            rhs = jnp.concatenate([jnp.where(lo_half, blk, zero), jnp.where(lo_half, zero, blk)], axis=0)
            parts.append(_dot(wsp_ref[j], rhs))
        s = jnp.concatenate(parts, axis=1) + bsb_ref[...]
        a = u[r0:r0 + SGU_CHUNK, :] * s
        a_ref[r0:r0 + SGU_CHUNK, :] = _rms(a, ona_ref[...]).astype(BF16)


def _inproj(x, norm1, w_in, q_gain, k_gain, sgu_norm, w_s, b_s, out_norm_a):
    t, d_model = x.shape
    n_heads_w = q_gain.shape[0]
    sgu_w = sgu_norm.shape[0]
    attn_w = (w_in.shape[1] - 2 * sgu_w) // 3
    n_heads = attn_w // n_heads_w
    n_groups = w_s.shape[0]
    assert n_heads_w == HEAD_DIM and sgu_w // n_groups == HEAD_DIM and w_s.shape[1] == SGU_CHUNK
    assert t % TOK_TILE == 0 and 2 * n_heads <= LANES

    heads = np.arange(2 * attn_w) // HEAD_DIM
    gsum = (heads[:, None] == np.arange(LANES)[None, :]).astype(np.float32)
    gexp = gsum.T
    qkg = jnp.concatenate([jnp.tile(q_gain, n_heads) * (HEAD_DIM ** -0.5), jnp.tile(k_gain, n_heads)])
    wsp = jnp.concatenate([w_s[0::2], w_s[1::2]], axis=2).astype(BF16)
    bsb = jnp.repeat(b_s.T, HEAD_DIM, axis=1)

    const = lambda shape: pl.BlockSpec(shape, lambda i: (0,) * len(shape))
    tok = lambda w: pl.BlockSpec((TOK_TILE, w), lambda i: (i, 0))
    return pl.pallas_call(
        functools.partial(_inproj_kernel, attn_w, sgu_w),
        grid=(t // TOK_TILE,),
        in_specs=[
            tok(d_model), const((1, d_model)), const(w_in.shape), const((2 * attn_w, LANES)), const((LANES, 2 * attn_w)),
            const((1, 2 * attn_w)), const((1, sgu_w)), const(wsp.shape), const((SGU_CHUNK, sgu_w)),
            const((1, sgu_w)),
        ],
        out_specs=[tok(attn_w), tok(attn_w), tok(attn_w), tok(sgu_w)],
        out_shape=[jax.ShapeDtypeStruct((t, attn_w), F32)] * 3 + [jax.ShapeDtypeStruct((t, sgu_w), BF16)],
        compiler_params=pltpu.CompilerParams(dimension_semantics=("arbitrary",), vmem_limit_bytes=VMEM_LIMIT),
        name="inproj",
    )(x, norm1[None], w_in.astype(BF16), jnp.asarray(gsum, BF16), jnp.asarray(gexp, BF16),
      qkg[None], sgu_norm[None], wsp, bsb, out_norm_a[None])


def _t5_bucket(rel):
    nb = N_BUCKETS // 2
    bucket = (rel > 0).astype(np.int32) * nb
    n = np.abs(rel)
    max_exact = nb // 2
    large = max_exact + (np.log(np.maximum(n, 1) / max_exact)
                         / np.log(MAX_DISTANCE / max_exact) * (nb - max_exact)).astype(np.int32)
    large = np.minimum(large, nb - 1)
    return (bucket + np.where(n < max_exact, n, large)).astype(np.int32)


def _branch_layout():
    out, kbase = [], 0
    for bi, (window, dil) in enumerate(WINDOWS):
        assert window // (2 * dil) == ATT_SIDE
        m = ATT_TILE // dil
        seg = m + 2 * ATT_SIDE
        out.append((dil, m, seg, kbase, bi * ATT_TILE))
        kbase += dil * seg
    return out, kbase


def _strided(ref, start, size, stride):
    if stride == 1:
        return ref[start:start + size, :]
    return ref[pl.ds(start, size, stride=stride), :]


def _attn_kernel(per_seq, q_ref, km_ref, kp_ref, kn_ref, vm_ref, vp_ref, vn_ref,
                 bias_ref, o_ref, qs, kts, vs, o_scr, d_scr, l_scr, s_scr, p_scr, m_scr, k4, v4, q4):
    i = pl.program_id(0)
    w = i % per_seq
    left_ok = w > 0
    right_ok = w < per_seq - 1
    layout, _ = _branch_layout()
    mid = WINDOWS[1][1]
    assert WINDOWS[2][1] == mid * mid
    halo4, main4 = ATT_HALO // mid, ATT_TILE // mid
    win4 = 2 * halo4 + main4

    @pl.when(jnp.logical_and(i == 0, pl.program_id(1) == 0))
    def _():
        vs[:, LANES:] = jnp.ones((vs.shape[0], LANES), BF16)

    for r in range(mid):
        for dst, main, prev, nxt in ((k4, km_ref, kp_ref, kn_ref), (v4, vm_ref, vp_ref, vn_ref)):
            dst[r * win4:r * win4 + halo4, :] = _strided(prev, r, halo4, mid)
            dst[r * win4 + halo4:r * win4 + halo4 + main4, :] = _strided(main, r, main4, mid)
            dst[r * win4 + halo4 + main4:(r + 1) * win4, :] = _strided(nxt, r, halo4, mid)
        q4[r * main4:(r + 1) * main4, :] = _strided(q_ref, r, main4, mid)

    def key_rows(src_k, src_v, start, stride, kbase_rows):
        kc = _strided(src_k, start, ATT_QB, stride)
        kts[kbase_rows // ATT_QB] = kc.T.astype(BF16)
        vs[kbase_rows:kbase_rows + ATT_QB, 0:LANES] = _strided(src_v, start, ATT_QB, stride).astype(BF16)

    for bi, (dil, m, seg, kbase, qbase) in enumerate(layout):
        nchunk = seg // ATT_QB
        for r in range(dil):
            o = kbase + r * seg
            if dil == 1:
                for c in range(nchunk):
                    lo = c * ATT_QB - ATT_SIDE
                    if c == 0:
                        kc = jnp.concatenate([kp_ref[ATT_HALO - ATT_SIDE:, :], km_ref[0:ATT_QB - ATT_SIDE, :]], axis=0)
                        vc = jnp.concatenate([vp_ref[ATT_HALO - ATT_SIDE:, :], vm_ref[0:ATT_QB - ATT_SIDE, :]], axis=0)
                    elif c == nchunk - 1:
                        kc = jnp.concatenate([km_ref[lo:, :], kn_ref[0:ATT_SIDE, :]], axis=0)
                        vc = jnp.concatenate([vm_ref[lo:, :], vn_ref[0:ATT_SIDE, :]], axis=0)
                    else:
                        kc = km_ref[lo:lo + ATT_QB, :]
                        vc = vm_ref[lo:lo + ATT_QB, :]
                    kts[(o + c * ATT_QB) // ATT_QB] = kc.T.astype(BF16)
                    vs[o + c * ATT_QB:o + (c + 1) * ATT_QB, 0:LANES] = vc.astype(BF16)
                qs[qbase:qbase + m, :] = q_ref[...].astype(BF16)
            else:
                r4, s = r % mid, r // mid
                step = dil // mid
                first = r4 * win4 + (halo4 - ATT_SIDE * step) + s
                for c in range(nchunk):
                    key_rows(k4, v4, first + c * ATT_QB * step, step, o + c * ATT_QB)
                qs[qbase + r * m:qbase + (r + 1) * m, :] = _strided(q4, r4 * main4 + s, m, step).astype(BF16)

    lane = lax.broadcasted_iota(jnp.int32, (ATT_QB, LANES), 1)
    head0 = lane < HEAD_DIM
    cells = ATT_TILE // ATT_QB
    total = len(layout) * cells
    blk_bits = [(m // ATT_QB).bit_length() - 1 for _, m, _, _, _ in layout]
    seg_chunks = [seg // ATT_QB for _, _, seg, _, _ in layout]
    base_chunks = [kbase // ATT_QB for _, _, _, kbase, _ in layout]

    def cell_params(g):
        bi = g // cells
        idx = g % cells
        if isinstance(g, int):
            pick = lambda vals: vals[bi]
        else:
            pick = lambda vals: jnp.where(bi == 0, vals[0], jnp.where(bi == 1, vals[1], vals[2]))
        bits = pick(blk_bits)
        last_blk = (1 << bits) - 1
        blk = idx & last_blk
        return bi, blk, last_blk, pick(base_chunks) + (idx >> bits) * pick(seg_chunks) + blk

    def stage_logits(g, slot):
        bi, blk, last_blk, kchunk = cell_params(g)
        left_bad = jnp.logical_and(blk == 0, jnp.logical_not(left_ok))
        right_bad = jnp.logical_and(blk == last_blk, jnp.logical_not(right_ok))
        variant = left_bad.astype(jnp.int32) + 2 * right_bad.astype(jnp.int32)
        qc = qs[pl.ds(pl.multiple_of(g * ATT_QB, ATT_QB), ATT_QB), :]
        zero = jnp.zeros_like(qc)
        lhs = jnp.concatenate([jnp.where(head0, qc, zero), jnp.where(head0, zero, qc)], axis=0)
        kt = jnp.concatenate([kts[kchunk], kts[kchunk + 1]], axis=1)
        s_scr[slot] = _dot(lhs, kt) + bias_ref[bi, variant]

    def stage_softmax(slot):
        part = ATT_QB // 2
        for c in range(2 * ATT_QB // part):
            s = s_scr[slot, c * part:(c + 1) * part, :]
            mx = jnp.max(s, axis=-1, keepdims=True)
            p_scr[slot, c * part:(c + 1) * part, :] = jnp.exp(s - mx).astype(BF16)
            m_scr[slot, c * part:(c + 1) * part, :] = jnp.broadcast_to(mx, (part, LANES))

    def stage_values(g, slot):
        _, _, _, kchunk = cell_params(g)
        koff = pl.multiple_of(kchunk * ATT_QB, ATT_QB)
        r = _dot(p_scr[slot], vs[pl.ds(koff, 2 * ATT_QB), :])
        num = jnp.where(head0, r[0:ATT_QB, 0:LANES], r[ATT_QB:, 0:LANES])
        den = jnp.where(head0, r[0:ATT_QB, LANES:], r[ATT_QB:, LANES:])
        mx = jnp.where(head0, m_scr[slot, 0:ATT_QB, :], m_scr[slot, ATT_QB:, :])
        rows = pl.ds(pl.multiple_of(g * ATT_QB, ATT_QB), ATT_QB)
        o_scr[rows, :] = num
        d_scr[rows, :] = den
        l_scr[rows, :] = mx

    width = ATT_PIPE

    def group(c, do_values, do_softmax, do_logits):
        if do_values:
            for u in range(width):
                stage_values(c - width + u, u)
        if do_softmax:
            for u in range(width):
                stage_softmax(u)
        if do_logits:
            for u in range(width):
                stage_logits(c + width + u, u)

    group(-width, False, False, True)
    group(0, False, True, True)

    def trip(t, carry):
        group((t + 1) * width, True, True, True)
        return carry

    lax.fori_loop(0, total // width - 2, trip, 0)
    group(total - width, True, True, False)
    group(total, True, False, False)

    big = WINDOWS[-1][1]
    rows = ATT_TILE // big
    dil0, m0, _, _, qb0 = layout[0]
    assert dil0 == 1 and big == mid * mid
    for r4 in range(mid):
        k4[r4 * main4:(r4 + 1) * main4, :] = _strided(o_scr, qb0 + r4, main4, mid)
        k4[m0 + r4 * main4:m0 + (r4 + 1) * main4, :] = _strided(d_scr, qb0 + r4, main4, mid)
        v4[r4 * main4:(r4 + 1) * main4, :] = _strided(l_scr, qb0 + r4, main4, mid)
    _, m1, _, _, qb1 = layout[1]
    _, m2, _, _, qb2 = layout[2]
    for r in range(big):
        r4, s = r % mid, r // mid
        first = r4 * main4 + s
        nums = [_strided(k4, first, rows, mid), _strided(o_scr, qb1 + r4 * m1 + s, rows, mid),
                o_scr[qb2 + r * m2:qb2 + (r + 1) * m2, :]]
        dens = [_strided(k4, m0 + first, rows, mid), _strided(d_scr, qb1 + r4 * m1 + s, rows, mid),
                d_scr[qb2 + r * m2:qb2 + (r + 1) * m2, :]]
        mxs = [_strided(v4, first, rows, mid), _strided(l_scr, qb1 + r4 * m1 + s, rows, mid),
               l_scr[qb2 + r * m2:qb2 + (r + 1) * m2, :]]
        top = jnp.maximum(jnp.maximum(mxs[0], mxs[1]), mxs[2])
        ws = [jnp.exp(mx - top) for mx in mxs]
        num = ws[0] * nums[0] + ws[1] * nums[1] + ws[2] * nums[2]
        den = ws[0] * dens[0] + ws[1] * dens[1] + ws[2] * dens[2]
        q4[pl.ds(first, rows, stride=mid), :] = num / den
    for r4 in range(mid):
        o_ref[pl.ds(r4, main4, stride=mid), :] = q4[r4 * main4:(r4 + 1) * main4, :]


def _attention(q, k, v, rel_bias, seq_len):
    t, attn_w = q.shape
    n_heads = attn_w // HEAD_DIM
    pair = LANES // HEAD_DIM
    assert seq_len % ATT_TILE == 0 and t % seq_len == 0 and ATT_TILE == 2 * ATT_HALO
    assert ATT_TILE // WINDOWS[-1][1] == ATT_QB and pair == 2
    layout, krows = _branch_layout()

    nb = len(WINDOWS)
    span = 2 * ATT_SIDE + 1
    jj = np.arange(2 * ATT_QB)[None, :]
    offsets = np.arange(span) - ATT_SIDE
    table_bucket = np.stack([_t5_bucket(offsets * dil) for _, dil in WINDOWS])
    table = jnp.zeros((n_heads, nb, span), F32)
    for b in range(N_BUCKETS):
        table = jnp.where(jnp.asarray(table_bucket == b)[None], rel_bias[b].astype(F32)[:, None, None], table)
    row_len = 3 * ATT_QB
    period = row_len + 1
    padded = jnp.concatenate([table, jnp.full((n_heads, nb, period - span), NEG_INF, F32)], axis=-1)
    flat = jnp.tile(padded, (1, 1, ATT_QB + 1))[..., :ATT_QB * row_len]
    bias = flat.reshape(n_heads, nb, ATT_QB, row_len)[..., :2 * ATT_QB]
    bias = bias.reshape(n_heads // pair, pair, nb, ATT_QB, 2 * ATT_QB).transpose(0, 2, 1, 3, 4)
    bias = bias.reshape(n_heads // pair, nb, 1, pair * ATT_QB, 2 * ATT_QB)
    left = jnp.asarray((jj < ATT_SIDE)[None, None, None] & (np.arange(4) % 2 == 1)[None, None, :, None, None])
    right = jnp.asarray((jj >= 2 * ATT_QB - ATT_SIDE)[None, None, None]
                        & (np.arange(4) // 2 == 1)[None, None, :, None, None])
    bias = jnp.where(left | right, NEG_INF, bias)

    halo_blocks = t // ATT_HALO
    per_tile = ATT_TILE // ATT_HALO
    mid = WINDOWS[1][1]
    main = pl.BlockSpec((ATT_TILE, LANES), lambda i, j: (i, j))
    prev = pl.BlockSpec((ATT_HALO, LANES), lambda i, j: (jnp.maximum(i * per_tile - 1, 0), j))
    nxt = pl.BlockSpec((ATT_HALO, LANES), lambda i, j: (jnp.minimum((i + 1) * per_tile, halo_blocks - 1), j))
    n_q = nb * ATT_TILE
    return pl.pallas_call(
        functools.partial(_attn_kernel, seq_len // ATT_TILE),
        grid=(t // ATT_TILE, attn_w // LANES),
        in_specs=[main, main, prev, nxt, main, prev, nxt,
                  pl.BlockSpec((None, nb, 4, pair * ATT_QB, 2 * ATT_QB), lambda i, j: (j, 0, 0, 0, 0))],
        out_specs=main,
        out_shape=jax.ShapeDtypeStruct((t, attn_w), F32),
        scratch_shapes=[pltpu.VMEM((n_q, LANES), BF16),
                        pltpu.VMEM((krows // ATT_QB, LANES, ATT_QB), BF16),
                        pltpu.VMEM((krows, 2 * LANES), BF16),
                        pltpu.VMEM((n_q, LANES), F32), pltpu.VMEM((n_q, LANES), F32),
                        pltpu.VMEM((n_q, LANES), F32),
                        pltpu.VMEM((ATT_PIPE, pair * ATT_QB, 2 * ATT_QB), F32),
                        pltpu.VMEM((ATT_PIPE, pair * ATT_QB, 2 * ATT_QB), BF16),
                        pltpu.VMEM((ATT_PIPE, pair * ATT_QB, LANES), F32),
                        pltpu.VMEM(((ATT_TILE + 2 * ATT_HALO), LANES), F32),
                        pltpu.VMEM(((ATT_TILE + 2 * ATT_HALO), LANES), F32),
                        pltpu.VMEM((ATT_TILE, LANES), F32)],
        compiler_params=pltpu.CompilerParams(dimension_semantics=("arbitrary", "arbitrary"),
                                             vmem_limit_bytes=VMEM_LIMIT),
        name="attn",
    )(q, k, k, k, v, v, v, bias)


def _outproj_kernel(n_tiles, sgu_w, n_exp, *refs):
    h2_ref = refs[10]
    i = pl.program_id(0)

    @pl.when(i < n_tiles)
    def _():
        _outproj_tile(sgu_w, n_exp, *refs)

    @pl.when(i >= n_tiles)
    def _():
        h2_ref[...] = jnp.zeros_like(h2_ref)


def _outproj_tile(sgu_w, n_exp, a_ref, attn_ref, x_ref, onb_ref, wo_ref, norm2_ref, wr_ref,
                  br_ref, tri_ref, x1_ref, h2_ref, meta_ref, cnt_ref, route_ref, carry):
    i = pl.program_id(0)

    @pl.when(i == 0)
    def _():
        carry[...] = jnp.zeros_like(carry)

    bn = _rms(attn_ref[...], onb_ref[...]).astype(BF16)
    x1 = x_ref[...] + _dot(a_ref[...], wo_ref[0:sgu_w, :]) + _dot(bn, wo_ref[sgu_w:, :])
    x1_ref[...] = x1
    h2 = _rms(x1, norm2_ref[...])
    h2_bf = h2.astype(BF16)
    h2_ref[...] = h2_bf
    logits = _dot(h2_bf, wr_ref[...]) + br_ref[...]

    lane = lax.broadcasted_iota(jnp.int32, logits.shape, 1)
    work = jnp.where(lane < n_exp, logits, -jnp.inf)
    chosen = jnp.zeros(logits.shape, jnp.bool_)
    experts, values = [], []
    for _ in range(TOP_K):
        top = jnp.max(work, axis=-1, keepdims=True)
        idx = jnp.min(jnp.where(work == top, lane, LANES), axis=-1, keepdims=True)
        hit = lane == idx
        chosen = jnp.logical_or(chosen, hit)
        work = jnp.where(hit, -jnp.inf, work)
        experts.append(idx)
        values.append(top)
    exps = [jnp.exp(v - values[0]) for v in values]
    den = exps[0] + exps[1] + exps[2] + exps[3]

    onehot = jnp.where(chosen, 1.0, 0.0)
    before = _dot(tri_ref[...], onehot.astype(BF16)) + carry[0:1, :]
    carry[0:1, :] = carry[0:1, :] + jnp.sum(onehot, axis=0, keepdims=True)
    cnt_ref[...] = carry[...]

    meta = jnp.zeros(logits.shape, F32)
    for kk in range(TOP_K):
        rank = jnp.sum(jnp.where(lane == experts[kk], before, 0.0), axis=-1, keepdims=True)
        meta = jnp.where(lane == kk, experts[kk].astype(F32), meta)
        meta = jnp.where(lane == TOP_K + kk, exps[kk] / den, meta)
        meta = jnp.where(lane == 2 * TOP_K + kk, rank, meta)
    meta_ref[...] = meta
    route_ref[...] = meta.T[0:route_ref.shape[0], :]


def _outproj(a_n, attn, x, out_norm_b, w_o, norm2, w_router, b_router):
    sgu_w = a_n.shape[1]
    attn_w = attn.shape[1]
    t, d_model = x.shape
    n_exp = w_router.shape[1]
    assert n_exp <= LANES and t % TOK_TILE == 0
    wr = jnp.pad(w_router, ((0, 0), (0, LANES - n_exp))).astype(BF16)
    br =jnp.pad(b_router, (0, LANES - n_exp))[None]
    tri = np.tril(np.ones((TOK_TILE, TOK_TILE), np.float32), -1)
    n_tiles = t // TOK_TILE
    h2_rows = max(t, GATHER_MIN_ROWS)
    assert h2_rows % TOK_TILE == 0
    last = n_tiles - 1

    const = lambda shape: pl.BlockSpec(shape, lambda i: (0,) * len(shape))
    tok = lambda w: pl.BlockSpec((TOK_TILE, w), lambda i: (jnp.minimum(i, last), 0))
    return pl.pallas_call(
        functools.partial(_outproj_kernel, n_tiles, sgu_w, n_exp),
        grid=(h2_rows // TOK_TILE,),
        in_specs=[
            tok(sgu_w), tok(attn_w), tok(d_model),
            const((1, attn_w)), const(w_o.shape), const((1, d_model)), const((d_model, LANES)),
            const((1, LANES)), const((TOK_TILE, TOK_TILE)),
        ],
        out_specs=[tok(d_model), pl.BlockSpec((TOK_TILE, d_model), lambda i: (i, 0)), tok(LANES), const((8, LANES)),
                   pl.BlockSpec((4 * TOP_K, TOK_TILE), lambda i: (0, jnp.minimum(i, last)))],
        out_shape=[jax.ShapeDtypeStruct((t, d_model), F32), jax.ShapeDtypeStruct((h2_rows, d_model), BF16),
                   jax.ShapeDtypeStruct((t, LANES), F32), jax.ShapeDtypeStruct((8, LANES), F32),
                   jax.ShapeDtypeStruct((4 * TOP_K, t), F32)],
        scratch_shapes=[pltpu.VMEM((8, LANES), F32)],
        compiler_params=pltpu.CompilerParams(dimension_semantics=("arbitrary",), vmem_limit_bytes=VMEM_LIMIT),
        name="outproj",
    )(a_n, attn, x, out_norm_b[None], w_o.astype(BF16), norm2[None], wr, br, jnp.asarray(tri, BF16))


def _moe_kernel(d_exp, blk_ref, exp_ref, lo_ref, hi_ref, slot_ref, nxt_ref, x_ref, wgu_hbm, bgu_ref, wd_hbm, bd_ref,
                y_ref, wgu_f, wd_f, wgu_s, wd_s, act_s, sem):
    it = pl.program_id(0)
    prev = jnp.maximum(it - 1, 0)
    new_expert = jnp.logical_or(it == 0, exp_ref[it] != exp_ref[prev])
    new_block = jnp.logical_or(it == 0, blk_ref[it] != blk_ref[prev])
    lo, hi = lo_ref[it], hi_ref[it]
    active = hi > lo

    def weight_copies(e, slot):
        return (pltpu.make_async_copy(wgu_hbm.at[e], wgu_f.at[slot], sem.at[0, slot]),
                pltpu.make_async_copy(wd_hbm.at[e], wd_f.at[slot], sem.at[1, slot]))

    @pl.when(it == 0)
    def _():
        for c in weight_copies(exp_ref[0], slot_ref[0]):
            c.start()

    @pl.when(jnp.logical_and(new_expert, active))
    def _():
        slot = slot_ref[it]
        for c in weight_copies(exp_ref[it], slot):
            c.wait()

        @pl.when(nxt_ref[it] >= 0)
        def _():
            for c in weight_copies(nxt_ref[it], 1 - slot):
                c.start()

        wgu_s[...] = wgu_f[slot].astype(BF16)
        wd_s[...] = wd_f[slot].astype(BF16)

    def expert_rows(off, m):
        rows = pl.ds(off, m)
        x = x_ref[rows, :]
        step = 512
        for n in range(0, d_exp, step):
            gate = _dot(x, wgu_s[:, n:n + step]) + bgu_ref[:, n:n + step]
            up = _dot(x, wgu_s[:, d_exp + n:d_exp + n + step]) + bgu_ref[:, d_exp + n:d_exp + n + step]
            gate = jnp.minimum(gate, SWIGLU_LIMIT)
            up = jnp.clip(up, -SWIGLU_LIMIT, SWIGLU_LIMIT)
            glu = gate * jax.nn.sigmoid(SWIGLU_ALPHA * gate)
            act_s[0:m, n:n + step] = ((up + 1.0) * glu).astype(BF16)
        y = (_dot(act_s[0:m, :], wd_s[...]) + bd_ref[...]).astype(y_ref.dtype)
        row = off + lax.broadcasted_iota(jnp.int32, (m, 1), 0)
        mine = jnp.logical_and(row >= lo, row < hi)

        @pl.when(new_block)
        def _():
            y_ref[rows, :] = jnp.where(mine, y, jnp.zeros_like(y))

        @pl.when(jnp.logical_not(new_block))
        def _():
            y_ref[rows, :] = jnp.where(mine, y, y_ref[rows, :])

    half = MOE_ROWS // 2
    in_upper = lo >= half
    one_half = jnp.logical_and(active, jnp.logical_or(hi <= half, in_upper))

    @pl.when(jnp.logical_and(active, jnp.logical_not(one_half)))
    def _():
        expert_rows(0, MOE_ROWS)

    @pl.when(one_half)
    def _():
        off = pl.multiple_of(jnp.where(in_upper, half, 0), half)
        expert_rows(off, half)

        @pl.when(new_block)
        def _():
            y_ref[pl.ds(pl.multiple_of(half - off, half), half), :] = jnp.zeros((half, y_ref.shape[1]), y_ref.dtype)


def _moe_experts(x_rows, items, w_gu, b_gu, w_down, b_down):
    n_rows, d_model = x_rows.shape
    n_exp, _, two_de = w_gu.shape
    d_exp = two_de // 2
    n_items = items[0].shape[0]
    grid_spec = pltpu.PrefetchScalarGridSpec(
        num_scalar_prefetch=len(items),
        grid=(n_items,),
        in_specs=[
            pl.BlockSpec((MOE_ROWS, d_model), lambda i, blk, ex, *_: (blk[i], 0)),
            pl.BlockSpec(memory_space=pl.ANY),
            pl.BlockSpec((None, 1, two_de), lambda i, blk, ex, *_: (ex[i], 0, 0)),
            pl.BlockSpec(memory_space=pl.ANY),
            pl.BlockSpec((None, 1, d_model), lambda i, blk, ex, *_: (ex[i], 0, 0)),
        ],
        out_specs=pl.BlockSpec((MOE_ROWS, d_model), lambda i, blk, ex, *_: (blk[i], 0)),
        scratch_shapes=[pltpu.VMEM((2, d_model, two_de), F32), pltpu.VMEM((2, d_exp, d_model), F32),
                        pltpu.VMEM((d_model, two_de), BF16), pltpu.VMEM((d_exp, d_model), BF16),
                        pltpu.VMEM((MOE_ROWS, d_exp), BF16), pltpu.SemaphoreType.DMA((2, 2))],
    )
    return pl.pallas_call(
        functools.partial(_moe_kernel, d_exp),
        grid_spec=grid_spec,
        out_shape=jax.ShapeDtypeStruct((n_rows, d_model), BF16),
        compiler_params=pltpu.CompilerParams(dimension_semantics=("arbitrary",), vmem_limit_bytes=VMEM_LIMIT),
        name="moe",
    )(*items, x_rows, w_gu, b_gu[:, None, :], w_down, b_down[:, None, :])


def _route(meta, counts_f, n_exp):
    t = meta.shape[1]
    n_assign = t * TOP_K
    assert n_assign % MOE_ROWS == 0
    top_e = meta[0:TOP_K].astype(jnp.int32)
    rank = meta[2 * TOP_K:3 * TOP_K].astype(jnp.int32)
    counts = counts_f[0, :n_exp].astype(jnp.int32)
    end = jnp.cumsum(counts)
    start = end - counts
    experts = jnp.arange(n_exp, dtype=jnp.int32)
    start_of = jnp.zeros_like(top_e)
    for e in range(n_exp):
        start_of = jnp.where(top_e == e, start[e], start_of)
    pos = (start_of + rank).reshape(-1)

    shift = (n_assign - 1).bit_length()
    assert (n_exp << shift) < 2 ** 31
    assign = jnp.arange(t, dtype=jnp.int32)[None, :] * TOP_K + jnp.arange(TOP_K, dtype=jnp.int32)[:, None]
    keys = ((top_e << shift) + assign).reshape(-1)
    sorted_tok = (jnp.sort(keys) & ((1 << shift) - 1)) // TOP_K

    n_blocks = n_assign // MOE_ROWS
    n_items = n_blocks + n_exp
    first_blk = start // MOE_ROWS
    n_it = jnp.where(counts > 0, (end - 1) // MOE_ROWS - first_blk + 1, 0)
    it_end = jnp.cumsum(n_it)
    it_start = it_end - n_it
    i = jnp.arange(n_items, dtype=jnp.int32)
    valid = i < it_end[-1]
    e_i = jnp.sum(jnp.minimum(i, it_end[-1] - 1)[:, None] >= it_end[None, :], axis=1).astype(jnp.int32)
    e_i = jnp.minimum(e_i, n_exp - 1)
    pick = lambda table: jnp.sum(jnp.where(e_i[:, None] == experts[None, :], table[None, :], 0), axis=1)
    blk_i = jnp.where(valid, pick(first_blk) + i - pick(it_start), n_blocks - 1)
    lo_i = jnp.where(valid, jnp.clip(pick(start) - blk_i * MOE_ROWS, 0, MOE_ROWS), 0)
    hi_i = jnp.where(valid, jnp.clip(pick(end) - blk_i * MOE_ROWS, 0, MOE_ROWS), 0)
    has = counts > 0
    slot_e = (jnp.cumsum(has.astype(jnp.int32)) - 1) % 2
    later = jnp.where(jnp.logical_and(has[None, :], experts[None, :] > experts[:, None]), experts[None, :], n_exp)
    next_e = jnp.min(later, axis=1)
    next_e = jnp.where(next_e == n_exp, -1, next_e)
    items = tuple(a.astype(jnp.int32) for a in (blk_i, e_i, lo_i, hi_i, pick(slot_e), pick(next_e)))
    return pos, sorted_tok, items


def _combine_kernel(x1_ref, yg_ref, meta_ref, o_ref):
    acc = x1_ref[...]
    for kk in range(TOP_K):
        gate = meta_ref[:, TOP_K + kk:TOP_K + kk + 1]
        acc = acc + gate * yg_ref[kk].astype(F32)
    o_ref[...] = acc


def _combine(x1, yg, meta):
    rows, d_model = x1.shape
    tile = 2 * TOK_TILE
    assert rows % tile == 0
    tok = lambda w: pl.BlockSpec((tile, w), lambda i: (i, 0))
    return pl.pallas_call(
        _combine_kernel,
        grid=(rows // tile,),
        in_specs=[tok(d_model), pl.BlockSpec((TOP_K, tile, d_model), lambda i: (0, i, 0)), tok(LANES)],
        out_specs=tok(d_model),
        out_shape=jax.ShapeDtypeStruct((rows, d_model), F32),
        compiler_params=pltpu.CompilerParams(dimension_semantics=("arbitrary",), vmem_limit_bytes=VMEM_LIMIT),
        name="combine",
    )(x1, yg, meta)


def kernel(x_prompt, x_sample, norm1, w_in, q_gain, k_gain, rel_bias, sgu_norm, w_s, b_s, out_norm_a, out_norm_b,
           w_o, norm2, w_router, b_router, w_gu, b_gu, w_down, b_down):
    assert norm1.shape[0] == 1, "single-layer trunk"
    bp, sp, d_model = x_prompt.shape
    bs, ss, _ = x_sample.shape
    n_exp = w_router.shape[-1]

    batches = sorted(((x_prompt.reshape(bp * sp, d_model), sp, 0), (x_sample.reshape(bs * ss, d_model), ss, 1)),
                     key=lambda c: -c[0].shape[0])
    staged, counts = [], None
    for x, seq_len, key in batches:
        if counts is not None:
            x, _ = lax.optimization_barrier((x, counts))
        q, k, v, a_n = _inproj(x, norm1[0], w_in[0], q_gain[0], k_gain[0], sgu_norm[0], w_s[0], b_s[0], out_norm_a[0])
        attn = _attention(q, k, v, rel_bias, seq_len)
        x1, h2, meta, counts, route = _outproj(a_n, attn, x, out_norm_b[0], w_o[0], norm2[0], w_router[0],
                                               b_router[0])
        pos, sorted_tok, items = _route(route, counts, n_exp)
        staged.append((key, x1, meta, pos, items, h2.at[sorted_tok].get(mode="promise_in_bounds")))

    outs, order_y, order_out = {}, counts, None
    for key, x1, meta, pos, items, x_rows in staged:
        x_rows, _ = lax.optimization_barrier((x_rows, order_y))
        y_rows = _moe_experts(x_rows, items, w_gu[0], b_gu[0], w_down[0], b_down[0])
        yg = y_rows.at[pos].get(mode="promise_in_bounds").reshape(TOP_K, x1.shape[0], d_model)
        if order_out is not None:
            yg, _ = lax.optimization_barrier((yg, order_out))
        outs[key] = _combine(x1, yg, meta)
        order_y, order_out = y_rows, outs[key]
    return (outs[0].reshape(bp, sp, d_model), outs[1].reshape(bs, ss, d_model))
```
